```python
import math
import numpy as np
import jax
import jax.numpy as jnp
from jax import lax

D_MODEL = 2048
BATCH = 8
SEQ = 2048
DEPTH = 1
DEC_BATCH = 32
DEC_SEQ = 1
PAST_LEN = 16384
PAGE_SIZE = 128

HEAD_DIM = 128
SB_HEADS = D_MODEL // (2 * HEAD_DIM)
NSA_HEADS = D_MODEL // (2 * HEAD_DIM)
NSA_KV_HEADS = NSA_HEADS // 4
NSA_GROUP = NSA_HEADS // NSA_KV_HEADS
MEM_HEADS = 4
MEM_TOKENS = 256
D_FF = 4 * D_MODEL
SB_WIDTH = SB_HEADS * HEAD_DIM
NSA_WIDTH = NSA_HEADS * HEAD_DIM
KV_WIDTH = NSA_KV_HEADS * HEAD_DIM
MEM_WIDTH = MEM_HEADS * HEAD_DIM
MIX_WIDTH = SB_WIDTH + NSA_WIDTH
N_GATES = 3 * NSA_HEADS
IN_SPLITS = (SB_WIDTH, SB_WIDTH, SB_WIDTH, NSA_WIDTH, KV_WIDTH, KV_WIDTH, KV_WIDTH, KV_WIDTH, KV_WIDTH, KV_WIDTH, N_GATES)
IN_COLS = sum(IN_SPLITS)
CMP_LEN = 32
CMP_STRIDE = 16
CMP_HIDDEN = HEAD_DIM
SEL_BLOCK = 64
N_SEL = 16
WINDOW = 512
N_BUCKETS = 32
MAX_DISTANCE = 1024
Q_BLOCK = 128
RMS_EPS = 1e-6
NEG_INF = -1e30
BIG = 1e30
PAD_POS = -(2 ** 30)

kernel_name = 'hybrid_sb_nsa_decoder_step'


def rmsnorm(x, g):
    xf = x.astype(jnp.float32)
    xf = xf * lax.rsqrt(jnp.mean(xf * xf, axis=-1, keepdims=True) + RMS_EPS)
    return xf.astype(x.dtype) * g


def masked_softmax(logits, mask):
    p = jax.nn.softmax(jnp.where(mask, logits, NEG_INF), axis=-1)
    return jnp.where(mask, p, 0.0)


def t5_bucket(rel):
    n = jnp.maximum(rel, 0)
    max_exact = N_BUCKETS // 2
    log_ratio = jnp.log(jnp.maximum(n, max_exact).astype(jnp.float32) / max_exact) / math.log(MAX_DISTANCE / max_exact)
    large = jnp.minimum(max_exact + (log_ratio * (N_BUCKETS - max_exact)).astype(jnp.int32), N_BUCKETS - 1)
    return jnp.where(n < max_exact, n, large)


def stick_breaking(q, k, v, q_pos, k_pos):
    z = jnp.einsum('qhd,khd->hqk', q, k, preferred_element_type=jnp.float32) * (1.0 / math.sqrt(HEAD_DIM))
    mask = k_pos[None, :] < q_pos[:, None]
    neg_log_keep = jnp.where(mask, jax.nn.softplus(z), 0.0)
    between = lax.cumsum(neg_log_keep, axis=2, reverse=True) - neg_log_keep
    a = jnp.where(mask, jnp.exp(jax.nn.log_sigmoid(z) - between), 0.0)
    return jnp.einsum('hqk,khd->qhd', a.astype(v.dtype), v)


def compress(kv, pe, w1, w2):
    n_cmp = (kv.shape[0] - CMP_LEN) // CMP_STRIDE + 1
    idx = np.arange(n_cmp)[:, None] * CMP_STRIDE + np.arange(CMP_LEN)[None, :]
    blocks = kv[idx] + pe[None, :, None, :]
    flat = blocks.transpose(0, 2, 1, 3).reshape(n_cmp, NSA_KV_HEADS, CMP_LEN * HEAD_DIM)
    return jax.nn.gelu(flat @ w1) @ w2


def mix_sequence(q_sb, k_sb, v_sb, q_ns, k_cmp, v_cmp, k_sel, v_sel, k_win, v_win, gates,
                 q_pos0, win_pos0, rel_table, pe_k, pe_v, w_ck1, w_ck2, w_cv1, w_cv2):
    tq, tk, tw = q_sb.shape[0], k_sb.shape[0], k_win.shape[0]
    k_pos = jnp.arange(tk, dtype=jnp.int32)
    scale = 1.0 / math.sqrt(HEAD_DIM)
    kc = compress(k_cmp, pe_k, w_ck1, w_ck2)
    vc = compress(v_cmp, pe_v, w_cv1, w_cv2)
    n_cmp = kc.shape[0]
    cmp_end = jnp.arange(n_cmp, dtype=jnp.int32) * CMP_STRIDE + (CMP_LEN - 1)
    n_blk = -(-tk // SEL_BLOCK)
    n_pick = min(N_SEL, n_blk)
    pad = n_blk * SEL_BLOCK - tk
    ks = jnp.pad(k_sel, ((0, pad), (0, 0), (0, 0)))
    vs = jnp.pad(v_sel, ((0, pad), (0, 0), (0, 0)))
    blk_np = np.arange(n_blk)
    lo = np.clip((blk_np * SEL_BLOCK - CMP_LEN) // CMP_STRIDE + 1, 0, n_cmp)
    hi = np.clip(-(-((blk_np + 1) * SEL_BLOCK) // CMP_STRIDE), 0, n_cmp)
    blk_j = jnp.arange(n_blk, dtype=jnp.int32)
    kw = jnp.pad(k_win, ((WINDOW, 0), (0, 0), (0, 0)))
    vw = jnp.pad(v_win, ((WINDOW, 0), (0, 0), (0, 0)))
    w_pos = jnp.concatenate([jnp.full((WINDOW,), PAD_POS, jnp.int32), win_pos0 + jnp.arange(tw, dtype=jnp.int32)])
    qb = Q_BLOCK if tq % Q_BLOCK == 0 else tq
    n_qb = tq // qb
    kv_idx = jnp.arange(NSA_KV_HEADS)[:, None, None]
    table_g = rel_table.reshape(N_BUCKETS, NSA_KV_HEADS, NSA_GROUP)

    def group_bias(rel):
        return jnp.transpose(table_g[t5_bucket(rel)], (2, 3, 0, 1)).astype(jnp.float32)

    def one_block(args):
        i, qs, qn, g = args
        q_pos = q_pos0 + i * qb + jnp.arange(qb, dtype=jnp.int32)
        o_sb = stick_breaking(qs, k_sb, v_sb, q_pos, k_pos)
        qg = qn.reshape(qb, NSA_KV_HEADS, NSA_GROUP, HEAD_DIM)
        rel_c = q_pos[:, None] - cmp_end[None, :]
        lc = jnp.einsum('qkgd,ckd->kgqc', qg, kc, preferred_element_type=jnp.float32) * scale + group_bias(rel_c)
        pc = masked_softmax(lc, rel_c >= 0)
        o_cmp = jnp.einsum('kgqc,ckd->qkgd', pc.astype(vc.dtype), vc)
        cs = jnp.pad(jnp.cumsum(jnp.sum(pc, axis=1), axis=-1), ((0, 0), (0, 0), (1, 0)))
        score = cs[..., hi] - cs[..., lo]
        q_blk = (q_pos // SEL_BLOCK)[:, None]
        forced = (blk_j == 0) | (blk_j == q_blk) | (blk_j == q_blk - 1)
        eligible = blk_j * SEL_BLOCK <= q_pos[:, None]
        score = jnp.where(forced, BIG, jnp.where(eligible, score, NEG_INF))
        _, pick = lax.top_k(score, n_pick)
        tok = (pick[..., None] * SEL_BLOCK + jnp.arange(SEL_BLOCK, dtype=jnp.int32)).reshape(NSA_KV_HEADS, qb, n_pick * SEL_BLOCK)
        kg = ks[tok, kv_idx]
        vg = vs[tok, kv_idx]
        rel_s = q_pos[None, :, None] - tok
        bias_s = jnp.transpose(table_g[t5_bucket(rel_s), kv_idx], (0, 3, 1, 2)).astype(jnp.float32)
        ls = jnp.einsum('qkgd,kqld->kgql', qg, kg, preferred_element_type=jnp.float32) * scale + bias_s
        ps = masked_softmax(ls, (rel_s >= 0)[:, None])
        o_sel = jnp.einsum('kgql,kqld->qkgd', ps.astype(vg.dtype), vg)
        start = q_pos0 + i * qb - win_pos0
        kwb = lax.dynamic_slice_in_dim(kw, start, WINDOW + qb, axis=0)
        vwb = lax.dynamic_slice_in_dim(vw, start, WINDOW + qb, axis=0)
        pwb = lax.dynamic_slice_in_dim(w_pos, start, WINDOW + qb, axis=0)
        rel_w = q_pos[:, None] - pwb[None, :]
        lw = jnp.einsum('qkgd,lkd->kgql', qg, kwb, preferred_element_type=jnp.float32) * scale + group_bias(rel_w)
        pw = masked_softmax(lw, (rel_w >= 0) & (rel_w < WINDOW))
        o_win = jnp.einsum('kgql,lkd->qkgd', pw.astype(vwb.dtype), vwb)
        g = g.reshape(qb, NSA_KV_HEADS, NSA_GROUP, 3)
        o_ns = g[..., 0:1] * o_cmp + g[..., 1:2] * o_sel + g[..., 2:3] * o_win
        return o_sb, o_ns.reshape(qb, NSA_HEADS, HEAD_DIM)

    o_sb, o_ns = lax.map(one_block, (jnp.arange(n_qb, dtype=jnp.int32),
                                     q_sb.reshape(n_qb, qb, SB_HEADS, HEAD_DIM),
                                     q_ns.reshape(n_qb, qb, NSA_HEADS, HEAD_DIM),
                                     gates.reshape(n_qb, qb, NSA_HEADS, 3)))
    return o_sb.reshape(tq, SB_HEADS, HEAD_DIM), o_ns.reshape(tq, NSA_HEADS, HEAD_DIM)


def split_projection(proj):
    lead = proj.shape[:-1]
    parts = jnp.split(proj, np.cumsum(IN_SPLITS)[:-1].tolist(), axis=-1)
    q_sb, k_sb, v_sb = [p.reshape(*lead, SB_HEADS, HEAD_DIM) for p in parts[:3]]
    q_ns = parts[3].reshape(*lead, NSA_HEADS, HEAD_DIM)
    kvs = [p.reshape(*lead, NSA_KV_HEADS, HEAD_DIM) for p in parts[4:10]]
    gates = jax.nn.sigmoid(parts[10]).reshape(*lead, NSA_HEADS, 3)
    return q_sb, k_sb, v_sb, q_ns, kvs, gates


def merge_groups(o_sb, o_ns, g_sb, g_ns, w_o):
    lead = o_sb.shape[:-2]
    o_sb = rmsnorm(o_sb, g_sb.reshape(SB_HEADS, HEAD_DIM)).reshape(*lead, SB_WIDTH)
    o_ns = rmsnorm(o_ns, g_ns.reshape(NSA_HEADS, HEAD_DIM)).reshape(*lead, NSA_WIDTH)
    return jnp.concatenate([o_sb, o_ns], axis=-1) @ w_o


def cross_attend(h, mem_kv, w_q, w_o):
    lead = h.shape[:-1]
    q = (h @ w_q).reshape(*lead, MEM_HEADS, HEAD_DIM)
    logits = jnp.einsum('bqhd,bmhd->bhqm', q, mem_kv[:, :, 0], preferred_element_type=jnp.float32) * (1.0 / math.sqrt(HEAD_DIM))
    p = jax.nn.softmax(logits, axis=-1)
    o = jnp.einsum('bhqm,bmhd->bqhd', p.astype(mem_kv.dtype), mem_kv[:, :, 1])
    return o.reshape(*lead, MEM_WIDTH) @ w_o


def sq_relu_mlp(h, w_up, w_down):
    return jnp.square(jax.nn.relu(h @ w_up)) @ w_down


def setup_inputs(seed: int = 0) -> dict:
    key = jax.random.key(seed)
    ks = jax.random.split(key, 32)
    n_pages = PAST_LEN // PAGE_SIZE
    n_pool = (5 * DEC_BATCH * n_pages) // 4
    w_buf = min(WINDOW, PAST_LEN)

    def nrm(i, shape, scale=1.0):
        return jax.random.normal(ks[i], shape, jnp.float32) * scale

    def gain(i, shape):
        return 1.0 + 0.01 * jax.random.normal(ks[i], shape, jnp.float32)

    page_table = jax.random.permutation(ks[6], n_pool)[:DEC_BATCH * n_pages].reshape(DEC_BATCH, n_pages).astype(jnp.int32)
    return {
        'x_prompt': nrm(0, (BATCH, SEQ, D_MODEL)),
        'x_sample': nrm(1, (DEC_BATCH, DEC_SEQ, D_MODEL)),
        'cache_sb_kv': nrm(2, (DEPTH, n_pool, PAGE_SIZE, 2, SB_HEADS, HEAD_DIM)),
        'cache_nsa_kv': nrm(3, (DEPTH, n_pool, PAGE_SIZE, 4, NSA_KV_HEADS, HEAD_DIM)),
        'cache_win_kv': nrm(4, (DEPTH, DEC_BATCH, w_buf, 2, NSA_KV_HEADS, HEAD_DIM)),
        'cache_mem_kv': nrm(5, (DEPTH, DEC_BATCH, MEM_TOKENS, 2, MEM_HEADS, HEAD_DIM)),
        'page_table': page_table,
        'mem_prompt': nrm(7, (BATCH, MEM_TOKENS, D_MODEL)),
        'norm_mix': gain(8, (DEPTH, D_MODEL)),
        'w_in': nrm(9, (DEPTH, D_MODEL, IN_COLS), D_MODEL ** -0.5),
        'norm_sb_out': gain(10, (DEPTH, SB_WIDTH)),
        'norm_nsa_out': gain(11, (DEPTH, NSA_WIDTH)),
        'w_out': nrm(12, (DEPTH, MIX_WIDTH, D_MODEL), MIX_WIDTH ** -0.5),
        'rel_bias_table': nrm(13, (N_BUCKETS, NSA_HEADS), 0.5),
        'cmp_pe_k': nrm(14, (DEPTH, CMP_LEN, HEAD_DIM), 0.1),
        'cmp_pe_v': nrm(15, (DEPTH, CMP_LEN, HEAD_DIM), 0.1),
        'w_cmp_k1': nrm(16, (DEPTH, CMP_LEN * HEAD_DIM, CMP_HIDDEN), (CMP_LEN * HEAD_DIM) ** -0.5),
        'w_cmp_k2': nrm(17, (DEPTH, CMP_HIDDEN, HEAD_DIM), CMP_HIDDEN ** -0.5),
        'w_cmp_v1': nrm(18, (DEPTH, CMP_LEN * HEAD_DIM, CMP_HIDDEN), (CMP_LEN * HEAD_DIM) ** -0.5),
        'w_cmp_v2': nrm(19, (DEPTH, CMP_HIDDEN, HEAD_DIM), CMP_HIDDEN ** -0.5),
        'norm_cross': gain(20, (DEPTH, D_MODEL)),
        'norm_mem': gain(21, (DEPTH, D_MODEL)),
        'w_cross_q': nrm(22, (DEPTH, D_MODEL, MEM_WIDTH), D_MODEL ** -0.5),
        'w_mem_kv': nrm(23, (DEPTH, D_MODEL, 2 * MEM_WIDTH), D_MODEL ** -0.5),
        'w_cross_o': nrm(24, (DEPTH, MEM_WIDTH, D_MODEL), MEM_WIDTH ** -0.5),
        'norm_ffn': gain(25, (DEPTH, D_MODEL)),
        'w_up': nrm(26, (DEPTH, D_MODEL, D_FF), D_MODEL ** -0.5),
        'w_down': nrm(27, (DEPTH, D_FF, D_MODEL), D_FF ** -0.5),
        'norm_final': gain(28, (D_MODEL,)),
    }


def reference(x_prompt, x_sample, cache_sb_kv, cache_nsa_kv, cache_win_kv, cache_mem_kv, page_table, mem_prompt,
              norm_mix, w_in, norm_sb_out, norm_nsa_out, w_out, rel_bias_table,
              cmp_pe_k, cmp_pe_v, w_cmp_k1, w_cmp_k2, w_cmp_v1, w_cmp_v2,
              norm_cross, norm_mem, w_cross_q, w_mem_kv, w_cross_o,
              norm_ffn, w_up, w_down, norm_final):
    w_buf = cache_win_kv.shape[2]
    win_keep = min(WINDOW, SEQ)
    xp, xs = x_prompt, x_sample
    sb_p, sb_s, ns_p, ns_s, win_p, win_s, mem_p = [], [], [], [], [], [], []
    for l in range(DEPTH):
        mp = (rel_bias_table, cmp_pe_k[l], cmp_pe_v[l], w_cmp_k1[l], w_cmp_k2[l], w_cmp_v1[l], w_cmp_v2[l])

        h = rmsnorm(xp, norm_mix[l])
        q_sb, k_sb, v_sb, q_ns, kvs, gates = split_projection(h @ w_in[l])
        o_sb, o_ns = lax.map(lambda a, mp=mp: mix_sequence(*a, 0, 0, *mp),
                             (q_sb, k_sb, v_sb, q_ns, *kvs, gates))
        xp = xp + merge_groups(o_sb, o_ns, norm_sb_out[l], norm_nsa_out[l], w_out[l])
        sb_p.append(jnp.stack([k_sb, v_sb], axis=2))
        ns_p.append(jnp.stack(kvs[:4], axis=2))
        win_p.append(jnp.stack(kvs[4:6], axis=2)[:, SEQ - win_keep:])

        h = rmsnorm(xs, norm_mix[l])
        q_sb, k_sb, v_sb, q_ns, kvs, gates = split_projection(h @ w_in[l])
        sb_new = jnp.stack([k_sb, v_sb], axis=2)
        ns_new = jnp.stack(kvs[:4], axis=2)
        win_all = jnp.concatenate([cache_win_kv[l], jnp.stack(kvs[4:6], axis=2)], axis=1)

        def sample_seq(a, l=l, mp=mp):
            pt, qsb, qns, g, sbn, nsn, wa = a
            sb_all = jnp.concatenate([cache_sb_kv[l, pt].reshape(PAST_LEN, 2, SB_HEADS, HEAD_DIM), sbn], axis=0)
            ns_all = jnp.concatenate([cache_nsa_kv[l, pt].reshape(PAST_LEN, 4, NSA_KV_HEADS, HEAD_DIM), nsn], axis=0)
            return mix_sequence(qsb, sb_all[:, 0], sb_all[:, 1], qns,
                                ns_all[:, 0], ns_all[:, 1], ns_all[:, 2], ns_all[:, 3],
                                wa[:, 0], wa[:, 1], g, PAST_LEN, PAST_LEN - w_buf, *mp)

        o_sb, o_ns = lax.map(sample_seq, (page_table, q_sb, q_ns, gates, sb_new, ns_new, win_all))
        xs = xs + merge_groups(o_sb, o_ns, norm_sb_out[l], norm_nsa_out[l], w_out[l])
        sb_s.append(sb_new)
        ns_s.append(ns_new)
        win_s.append(win_all[:, win_all.shape[1] - w_buf:])

        mem_kv = (rmsnorm(mem_prompt, norm_mem[l]) @ w_mem_kv[l]).reshape(mem_prompt.shape[0], MEM_TOKENS, 2, MEM_HEADS, HEAD_DIM)
        mem_p.append(mem_kv)
        xp = xp + cross_attend(rmsnorm(xp, norm_cross[l]), mem_kv, w_cross_q[l], w_cross_o[l])
        xs = xs + cross_attend(rmsnorm(xs, norm_cross[l]), cache_mem_kv[l], w_cross_q[l], w_cross_o[l])

        xp = xp + sq_relu_mlp(rmsnorm(xp, norm_ffn[l]), w_up[l], w_down[l])
        xs = xs + sq_relu_mlp(rmsnorm(xs, norm_ffn[l]), w_up[l], w_down[l])

    y_prompt = rmsnorm(xp, norm_final)
    y_sample = rmsnorm(xs, norm_final)
    sb_kv_prompt = jnp.stack(sb_p, axis=0)
    sb_kv_sample = jnp.stack(sb_s, axis=0)
    nsa_kv_prompt = jnp.stack(ns_p, axis=0)
    nsa_kv_sample = jnp.stack(ns_s, axis=0)
    win_kv_prompt = jnp.stack(win_p, axis=0)
    win_kv_sample = jnp.stack(win_s, axis=0)
    mem_kv_prompt = jnp.stack(mem_p, axis=0)
    return (y_prompt, y_sample, sb_kv_prompt, sb_kv_sample, nsa_kv_prompt, nsa_kv_sample, win_kv_prompt, win_kv_sample, mem_kv_prompt)
```

```python
import functools
import math

import numpy as np
import jax
import jax.numpy as jnp
from jax import lax
from jax.experimental import pallas as pl
from jax.experimental.pallas import tpu as pltpu

HEAD_DIM = 128
NSA_GROUP = 4
MEM_HEADS = 4
PAGE_SIZE = 128
CMP_LEN = 32
CMP_STRIDE = 16
SEL_BLOCK = 64
N_SEL = 16
WINDOW = 512
N_BUCKETS = 32
MAX_DISTANCE = 1024
RMS_EPS = 1e-6
NEG_INF = -1e30
BIG = 1e30
LANES = 128
SUBLANES = 8
VMEM_LIMIT = 56 * 1024 * 1024

F32 = jnp.float32
BF16 = jnp.bfloat16
_NT = (((1,), (1,)), ((), ()))


def _bucket_thresholds():
    max_exact = N_BUCKETS // 2
    n_log = N_BUCKETS - max_exact
    ratio = MAX_DISTANCE // max_exact
    th = list(range(max_exact)) + [max_exact]
    for m in range(1, n_log):
        n = th[-1]
        while n ** n_log < (max_exact ** n_log) * (ratio ** m):
            n += 1
        th.append(n)
    return tuple(th)


THRESHOLDS = _bucket_thresholds()


def _bias_chain(rel, tab_ref, h):
    v = jnp.full(rel.shape, tab_ref[0, h], F32)
    for b in range(1, N_BUCKETS):
        v = jnp.where(rel >= THRESHOLDS[b], tab_ref[b, h], v)
    return v


def _rms(x, g):
    return x * lax.rsqrt(jnp.mean(x * x, axis=-1, keepdims=True) + RMS_EPS) * g


def _softplus(z):
    return jnp.maximum(z, 0.0) + jnp.log1p(jnp.exp(-jnp.abs(z)))


def _split2(x):
    hi = x.astype(BF16)
    lo = (x - hi.astype(F32)).astype(BF16)
    return hi, lo


def _split3(x):
    hi = x.astype(BF16)
    r = x - hi.astype(F32)
    mid = r.astype(BF16)
    lo = (r - mid.astype(F32)).astype(BF16)
    return hi, mid, lo


def _masked_softmax(lg, valid):
    lgm = jnp.where(valid, lg, NEG_INF)
    m = jnp.max(lgm, axis=-1, keepdims=True)
    e = jnp.where(valid, jnp.exp(lgm - m), 0.0)
    s = jnp.sum(e, axis=-1, keepdims=True)
    return e / jnp.where(s > 0.0, s, 1.0)


def _params(*sem):
    return pltpu.CompilerParams(dimension_semantics=sem, vmem_limit_bytes=VMEM_LIMIT)


def _resident(shape):
    nd = len(shape)
    return pl.BlockSpec(shape, lambda *_: (0,) * nd, pipeline_mode=pl.Buffered(1))


def _proj_body(x_ref, g_ref, w_ref, qsb_ref, sbkv_ref, sbkv16_ref, qns_ref, nsakv_ref,
               nsakv16_ref, winkv_ref, winkv16_ref, gates_ref, *, sbw, nsw, kvw, scale):
    hb = _rms(x_ref[...], g_ref[...]).astype(BF16)

    def mm(c0, n):
        return jnp.dot(hb, w_ref[:, c0:c0 + n], preferred_element_type=F32)

    c = 0
    qsb_ref[...] = (mm(c, sbw) * scale).astype(BF16)
    c += sbw
    for half in range(2):
        kv = mm(c, sbw)
        sbkv_ref[:, half * sbw:(half + 1) * sbw] = kv
        sbkv16_ref[:, half * sbw:(half + 1) * sbw] = kv.astype(BF16)
        c += sbw
    qns_ref[...] = (mm(c, nsw) * scale).astype(BF16)
    c += nsw
    kv = mm(c, 4 * kvw)
    nsakv_ref[...] = kv
    nsakv16_ref[...] = kv.astype(BF16)
    c += 4 * kvw
    kv = mm(c, 2 * kvw)
    winkv_ref[...] = kv
    winkv16_ref[...] = kv.astype(BF16)
    c += 2 * kvw
    gates_ref[...] = jax.nn.sigmoid(mm(c, LANES))


def _project(x, g, w_pad, tm):
    m, d = x.shape
    sbw = nsw = d // 2
    kvw = nsw // NSA_GROUP
    widths = (sbw, 2 * sbw, 2 * sbw, nsw, 4 * kvw, 4 * kvw, 2 * kvw, 2 * kvw, LANES)
    dtypes = (BF16, F32, BF16, BF16, F32, BF16, F32, BF16, F32)
    body = functools.partial(_proj_body, sbw=sbw, nsw=nsw, kvw=kvw, scale=1.0 / math.sqrt(HEAD_DIM))
    return pl.pallas_call(
        body,
        grid=(m // tm,),
        in_specs=[pl.BlockSpec((tm, d), lambda i: (i, 0)),
                  pl.BlockSpec((1, d), lambda i: (0, 0)),
                  _resident(w_pad.shape)],
        out_specs=[pl.BlockSpec((tm, w), lambda i: (i, 0)) for w in widths],
        out_shape=[jax.ShapeDtypeStruct((m, w), dt) for w, dt in zip(widths, dtypes)],
        compiler_params=_params("parallel"),
        name="in_proj",
    )(x, g.reshape(1, d), w_pad)


def _bias_tile_body(tab_ref, o_ref, *, q_stride, k_stride, k_off):
    m = pl.program_id(0)
    h = pl.program_id(1)
    i = lax.broadcasted_iota(jnp.int32, (LANES, LANES), 0)
    j = lax.broadcasted_iota(jnp.int32, (LANES, LANES), 1)
    rel = m * q_stride + i - (k_stride * j + k_off)
    o_ref[0, 0] = _bias_chain(rel, tab_ref, h)


def _bias_tiles(table, n_tiles, q_stride, k_stride, k_off):
    n_heads = table.shape[1]
    body = functools.partial(_bias_tile_body, q_stride=q_stride, k_stride=k_stride, k_off=k_off)
    out = pl.pallas_call(
        body,
        grid=(n_tiles, n_heads),
        in_specs=[pl.BlockSpec(memory_space=pltpu.SMEM)],
        out_specs=pl.BlockSpec((1, 1, LANES, LANES), lambda m, h: (m, h, 0, 0)),
        out_shape=jax.ShapeDtypeStruct((n_tiles, n_heads, LANES, LANES), F32),
        compiler_params=_params("parallel", "parallel"),
        name="bias_tiles",
    )(table)
    return out.reshape(n_tiles, n_heads // NSA_GROUP, NSA_GROUP * LANES, LANES)


def _cumsum_rhs(t):
    u = np.tril(np.ones((t, t), np.float32))
    half = np.concatenate([u, np.ones((t, t), np.float32)], axis=1)
    return jnp.asarray(np.concatenate([half, half], axis=0), BF16)


def _sb_tile(q, k, v, uo, carry, acc, valid):
    t = q.shape[0]
    z = lax.dot_general(q, k, _NT, preferred_element_type=F32)
    sp = _softplus(z)
    if valid is not None:
        sp = jnp.where(valid, sp, 0.0)
    hi, lo = _split2(sp)
    ct = jnp.dot(jnp.concatenate([hi, lo], axis=1), uo, preferred_element_type=F32)
    a = jnp.exp(z - (ct[:, :t] + carry))
    if valid is not None:
        a = jnp.where(valid, a, 0.0)
    acc = acc + jnp.dot(a.astype(BF16), v, preferred_element_type=F32)
    return carry + ct[:, t:], acc


def _sb_prompt_body(q_ref, k_ref, v_ref, g_ref, uo_ref, o_ref, *, tq):
    qi = pl.program_id(2)
    q = q_ref[...]
    uo = uo_ref[...]
    row = lax.broadcasted_iota(jnp.int32, (tq, tq), 0)
    col = lax.broadcasted_iota(jnp.int32, (tq, tq), 1)
    zero = jnp.zeros((tq, HEAD_DIM), F32)
    start = pl.multiple_of(qi * tq, tq)
    carry, acc = _sb_tile(q, k_ref[pl.ds(start, tq), :], v_ref[pl.ds(start, tq), :], uo,
                          jnp.zeros((tq, tq), F32), zero, col < row)

    def body(t, ca):
        s = pl.multiple_of((qi - 1 - t) * tq, tq)
        return _sb_tile(q, k_ref[pl.ds(s, tq), :], v_ref[pl.ds(s, tq), :], uo, ca[0], ca[1], None)

    carry, acc = lax.fori_loop(0, qi, body, (carry, acc))
    o_ref[...] = _rms(acc, g_ref[0]).astype(BF16)


def _sb_prompt(qsb, sbkv16, gain, batch, seq):
    m, sbw = qsb.shape
    heads = sbw // HEAD_DIM
    tq = LANES
    nq = seq // tq
    return pl.pallas_call(
        functools.partial(_sb_prompt_body, tq=tq),
        grid=(batch, heads, nq),
        in_specs=[pl.BlockSpec((tq, HEAD_DIM), lambda b, h, i: (b * nq + i, h)),
                  pl.BlockSpec((seq, HEAD_DIM), lambda b, h, i: (b, h)),
                  pl.BlockSpec((seq, HEAD_DIM), lambda b, h, i: (b, heads + h)),
                  pl.BlockSpec((1, 1, HEAD_DIM), lambda b, h, i: (h, 0, 0)),
                  pl.BlockSpec((2 * tq, 2 * tq), lambda b, h, i: (0, 0))],
        out_specs=pl.BlockSpec((tq, HEAD_DIM), lambda b, h, i: (b * nq + i, h)),
        out_shape=jax.ShapeDtypeStruct((m, sbw), BF16),
        compiler_params=_params("parallel", "parallel", "arbitrary"),
        name="sb_prompt",
    )(qsb, sbkv16, sbkv16, gain.reshape(heads, 1, HEAD_DIM), _cumsum_rhs(tq))


def _sb_decode_body(pt_ref, *refs, n_pages_step, n_steps, heads):
    pages = refs[:n_pages_step]
    q_ref, uo_ref, g_ref, o_ref, acc_ref, carry_ref = refs[n_pages_step:]
    s = pl.program_id(1)
    width = heads * HEAD_DIM

    @pl.when(s == 0)
    def _():
        acc_ref[...] = jnp.zeros_like(acc_ref)
        carry_ref[...] = jnp.zeros_like(carry_ref)

    q = q_ref[0]
    uo = uo_ref[...]
    for p in range(n_pages_step):
        k = pages[p][0, :, 0:width].astype(BF16)
        v = pages[p][0, :, width:2 * width].astype(BF16)
        z = lax.dot_general(q, k, _NT, preferred_element_type=F32)
        hi, lo = _split2(_softplus(z))
        ct = jnp.dot(jnp.concatenate([hi, lo], axis=1), uo, preferred_element_type=F32)
        a = jnp.exp(z - (ct[:, :PAGE_SIZE] + carry_ref[...]))
        acc_ref[...] += jnp.dot(a.astype(BF16), v, preferred_element_type=F32)
        carry_ref[...] += ct[:, PAGE_SIZE:]

    @pl.when(s == n_steps - 1)
    def _():
        acc = acc_ref[...]
        rowi = lax.broadcasted_iota(jnp.int32, (heads, HEAD_DIM), 0)
        o = jnp.zeros((heads, HEAD_DIM), F32)
        for h in range(heads):
            o = jnp.where(rowi == h, acc[:, h * HEAD_DIM:(h + 1) * HEAD_DIM], o)
        o_ref[0] = _rms(o, g_ref[...])


def _sb_decode(q_bd, cache, page_table, gain, pages_per_step):
    db, heads, width = q_bd.shape
    n_pages = page_table.shape[1]
    n_steps = n_pages // pages_per_step
    row = cache.shape[-1]

    def page_spec(p):
        def idx(b, s, pt):
            return (pt[b * n_pages + n_pages - 1 - (s * pages_per_step + p)], 0, 0)
        return pl.BlockSpec((1, PAGE_SIZE, row), idx)

    grid_spec = pltpu.PrefetchScalarGridSpec(
        num_scalar_prefetch=1,
        grid=(db, n_steps),
        in_specs=[page_spec(p) for p in range(pages_per_step)] + [
            pl.BlockSpec((1, heads, width), lambda b, s, pt: (b, 0, 0)),
            pl.BlockSpec((2 * PAGE_SIZE, 2 * PAGE_SIZE), lambda b, s, pt: (0, 0)),
            pl.BlockSpec((heads, HEAD_DIM), lambda b, s, pt: (0, 0))],
        out_specs=pl.BlockSpec((1, heads, HEAD_DIM), lambda b, s, pt: (b, 0, 0)),
        scratch_shapes=[pltpu.VMEM((heads, width), F32), pltpu.VMEM((heads, PAGE_SIZE), F32)])
    body = functools.partial(_sb_decode_body, n_pages_step=pages_per_step, n_steps=n_steps, heads=heads)
    return pl.pallas_call(
        body, grid_spec=grid_spec,
        out_shape=jax.ShapeDtypeStruct((db, heads, HEAD_DIM), F32),
        compiler_params=_params("parallel", "arbitrary"),
        name="sb_decode",
    )(page_table.reshape(-1), *([cache] * pages_per_step), q_bd, _cumsum_rhs(PAGE_SIZE),
      gain.reshape(heads, HEAD_DIM))


def _pe_term_body(pe_ref, w_ref, o_ref):
    o_ref[0] = jnp.dot(pe_ref[0].astype(BF16), w_ref[0].astype(BF16), preferred_element_type=F32)


def _pe_term(pe, w1):
    n, _, kdim = pe.shape
    hid = w1.shape[-1]
    return pl.pallas_call(
        _pe_term_body, grid=(n,),
        in_specs=[pl.BlockSpec((1, SUBLANES, kdim), lambda t: (t, 0, 0)),
                  pl.BlockSpec((1, kdim, hid), lambda t: (t, 0, 0))],
        out_specs=pl.BlockSpec((1, SUBLANES, hid), lambda t: (t, 0, 0)),
        out_shape=jax.ShapeDtypeStruct((n, SUBLANES, hid), F32),
        compiler_params=_params("parallel"),
        name="cmp_pe_term",
    )(pe, w1)


def _compress_body(pt_ref, *refs, n_pages_step, kv_heads):
    pages = refs[:n_pages_step]
    w1_ref, w2_ref, pe_ref, o_ref, carry_ref, tok_ref = refs[n_pages_step:]
    chunks_page = PAGE_SIZE // CMP_STRIDE
    rows = n_pages_step * chunks_page
    step_tokens = n_pages_step * PAGE_SIZE

    for idx in range(2 * kv_heads):
        for p in range(n_pages_step):
            tok_ref[pl.ds(idx * step_tokens + p * PAGE_SIZE, PAGE_SIZE), :] = (
                pages[p][0, :, idx * HEAD_DIM:(idx + 1) * HEAD_DIM])

    @pl.when(pl.program_id(1) == 0)
    def _():
        carry_ref[...] = jnp.zeros_like(carry_ref)

    rowi = lax.broadcasted_iota(jnp.int32, (rows, HEAD_DIM), 0)
    for t in range(2):
        for k in range(kv_heads):
            idx = t * kv_heads + k
            c0 = idx * HEAD_DIM
            acc = None
            for lp in range(CMP_STRIDE // 2):
                halves = [tok_ref[pl.ds(idx * step_tokens + 2 * lp + dl, rows, stride=CMP_STRIDE), :]
                          for dl in range(2)]
                x = jnp.concatenate(halves, axis=1).astype(BF16)
                d = jnp.dot(x, w1_ref[t, lp], preferred_element_type=F32)
                acc = d if acc is None else acc + d
            first = acc[:, :HEAD_DIM]
            second = acc[:, HEAD_DIM:]
            prev_first = jnp.where(rowi == 0, carry_ref[idx], pltpu.roll(first, 1, axis=0))
            carry_ref[idx] = first[rows - 1:rows, :]
            hid = jax.nn.gelu(prev_first + second + pe_ref[t, 0:1, :])
            o_ref[0, :, c0:c0 + HEAD_DIM] = jnp.dot(hid.astype(BF16), w2_ref[t], preferred_element_type=F32)


def _compress(cache, page_table, w1cat, w2, pe_term, kv_heads, pages_per_step):
    nb, n_pages = page_table.shape
    n_steps = n_pages // pages_per_step
    chunks = n_pages * (PAGE_SIZE // CMP_STRIDE)
    rows_step = pages_per_step * (PAGE_SIZE // CMP_STRIDE)
    width = 2 * kv_heads * HEAD_DIM

    def page_spec(p):
        return pl.BlockSpec((1, PAGE_SIZE, width),
                            lambda b, s, pt: (pt[b * n_pages + s * pages_per_step + p], 0, 0))

    grid_spec = pltpu.PrefetchScalarGridSpec(
        num_scalar_prefetch=1,
        grid=(nb, n_steps),
        in_specs=[page_spec(p) for p in range(pages_per_step)] + [
            pl.BlockSpec(w1cat.shape, lambda b, s, pt: (0, 0, 0, 0)),
            pl.BlockSpec(w2.shape, lambda b, s, pt: (0, 0, 0)),
            pl.BlockSpec(pe_term.shape, lambda b, s, pt: (0, 0, 0))],
        out_specs=pl.BlockSpec((1, rows_step, width), lambda b, s, pt: (b, s, 0)),
        scratch_shapes=[pltpu.VMEM((2 * kv_heads, 1, HEAD_DIM), F32),
                        pltpu.VMEM((2 * kv_heads * pages_per_step * PAGE_SIZE, HEAD_DIM), F32)])
    body = functools.partial(_compress_body, n_pages_step=pages_per_step, kv_heads=kv_heads)
    return pl.pallas_call(
        body, grid_spec=grid_spec,
        out_shape=jax.ShapeDtypeStruct((nb, chunks, width), F32),
        compiler_params=_params("parallel", "arbitrary"),
        name="nsa_compress",
    )(page_table.reshape(-1), *([cache] * pages_per_step), w1cat, w2, pe_term)


def _flash_tile(q, k, v, bias, valid, m, l, acc):
    lg = lax.dot_general(q, k, _NT, preferred_element_type=F32) + bias
    lgm = jnp.where(valid, lg, NEG_INF)
    m_new = jnp.maximum(m, jnp.max(lgm, axis=-1, keepdims=True))
    p = jnp.where(valid, jnp.exp(lgm - m_new), 0.0)
    alpha = jnp.exp(m - m_new)
    l = alpha * l + jnp.sum(p, axis=-1, keepdims=True)
    acc = alpha * acc + jnp.dot(p.astype(BF16), v, preferred_element_type=F32)
    return m_new, l, acc


def _nsa_prompt_body(q_ref, kc_ref, vc_ref, ks_ref, vs_ref, kw_ref, vw_ref, tsel_ref, tcmp_ref,
                     ind_ref, exp_ref, gates_ref, gn_ref, o_ref, mask_ref, *, n_blk, n_pick, n_kt):
    i = pl.program_id(2)
    t = LANES
    rows = NSA_GROUP * t
    q = jnp.concatenate([q_ref[:, g * HEAD_DIM:(g + 1) * HEAD_DIM] for g in range(NSA_GROUP)], axis=0)
    qrow = lax.broadcasted_iota(jnp.int32, (rows, t), 0) & (t - 1)
    col = lax.broadcasted_iota(jnp.int32, (rows, t), 1)
    qpos = i * t + qrow

    lc = lax.dot_general(q, kc_ref[0].astype(BF16), _NT, preferred_element_type=F32) + tcmp_ref[0, 0]
    valid_c = (col >= 1) & (qpos >= CMP_STRIDE * col + (CMP_LEN - CMP_STRIDE - 1))
    pc = _masked_softmax(lc, valid_c)
    o_cmp = jnp.dot(pc.astype(BF16), vc_ref[0].astype(BF16), preferred_element_type=F32)

    pcs = pc[0:t]
    for g in range(1, NSA_GROUP):
        pcs = pcs + pc[g * t:(g + 1) * t]
    p3 = jnp.concatenate(_split3(pcs), axis=1)
    score = lax.dot_general(ind_ref[...], p3, _NT, preferred_element_type=F32)[:n_blk]
    jj = lax.broadcasted_iota(jnp.int32, (n_blk, t), 0)
    qq = i * t + lax.broadcasted_iota(jnp.int32, (n_blk, t), 1)
    qblk = qq // SEL_BLOCK
    forced = (jj == 0) | (jj == qblk) | (jj == qblk - 1)
    score = jnp.where(forced, BIG, jnp.where(jj * SEL_BLOCK <= qq, score, NEG_INF))
    rank = jnp.zeros((n_blk, t), jnp.int32)
    for jp in range(n_blk):
        r = score[jp:jp + 1, :]
        beats = (r > score) | ((r == score) & (jj > jp))
        rank = rank + beats.astype(jnp.int32)
    sel_t = jnp.concatenate([(rank < n_pick).astype(F32), jnp.zeros((t - n_blk, t), F32)], axis=0)
    maskf = jnp.dot(sel_t.T.astype(BF16), exp_ref[...], preferred_element_type=F32)
    for kt in range(n_kt):
        mask_ref[kt] = maskf[:, kt * t:(kt + 1) * t]

    m0 = jnp.full((rows, 1), NEG_INF, F32)
    l0 = jnp.zeros((rows, 1), F32)
    a0 = jnp.zeros((rows, HEAD_DIM), F32)
    causal = col <= qrow

    def sel_tile(jt, carry, diag):
        s = pl.multiple_of(jt * t, t)
        valid = jnp.concatenate([mask_ref[jt]] * NSA_GROUP, axis=0) > 0.5
        if diag:
            valid = valid & causal
        return _flash_tile(q, ks_ref[pl.ds(s, t), :], vs_ref[pl.ds(s, t), :], tsel_ref[i - jt, 0],
                           valid, *carry)

    carry = sel_tile(i, (m0, l0, a0), True)
    carry = lax.fori_loop(0, i, lambda jt, c: sel_tile(jt, c, False), carry)
    o_sel = carry[2] / carry[1]

    carry = (m0, l0, a0)
    n_back = WINDOW // t
    for w in range(n_back + 1):
        jt = jnp.maximum(i - w, 0)
        s = pl.multiple_of(jt * t, t)
        inside = col < jnp.where(i >= w, t, 0)
        if w == 0:
            valid = causal
        elif w == n_back:
            valid = inside & (col > qrow)
        else:
            valid = inside
        carry = _flash_tile(q, kw_ref[pl.ds(s, t), :], vw_ref[pl.ds(s, t), :], tsel_ref[w, 0],
                            valid, *carry)
    o_win = carry[2] / carry[1]

    gates = gates_ref[0]
    gn = gn_ref[0]
    for g in range(NSA_GROUP):
        sl = slice(g * t, (g + 1) * t)
        o = (gates[:, 3 * g:3 * g + 1] * o_cmp[sl] + gates[:, 3 * g + 1:3 * g + 2] * o_sel[sl]
             + gates[:, 3 * g + 2:3 * g + 3] * o_win[sl])
        o_ref[:, g * HEAD_DIM:(g + 1) * HEAD_DIM] = _rms(o, gn[:, g * HEAD_DIM:(g + 1) * HEAD_DIM]).astype(BF16)


def _sel_constants(n_chunks, n_blk, n_cmp, key_len):
    blk = np.arange(n_blk)
    lo = np.clip((blk * SEL_BLOCK - CMP_LEN) // CMP_STRIDE + 1, 0, n_cmp)
    hi = np.clip(-(-((blk + 1) * SEL_BLOCK) // CMP_STRIDE), 0, n_cmp)
    c = np.arange(n_chunks) - 1
    ind = ((c[None, :] >= lo[:, None]) & (c[None, :] < hi[:, None]) & (c[None, :] >= 0)).astype(np.float32)
    expand = (np.arange(key_len)[None, :] // SEL_BLOCK == blk[:, None]).astype(np.float32)
    return ind, expand


def _nsa_prompt(qns, kcvc, nsakv16, winkv16, tsel, tcmp, gates_k, gain, batch, seq):
    m, nsw = qns.shape
    kv_heads = nsw // (NSA_GROUP * HEAD_DIM)
    t = LANES
    nq = seq // t
    n_chunks = seq // CMP_STRIDE
    assert n_chunks == t, "the compressed branch is tiled as a single 128-column tile"
    n_cmp = (seq - CMP_LEN) // CMP_STRIDE + 1
    n_blk = -(-seq // SEL_BLOCK)
    n_pick = min(N_SEL, n_blk)
    ind, expand = _sel_constants(n_chunks, n_blk, n_cmp, seq)
    ind_pad = np.zeros((t, n_chunks), np.float32)
    ind_pad[:n_blk] = ind
    exp_pad = np.zeros((t, seq), np.float32)
    exp_pad[:n_blk] = expand
    ind3 = jnp.asarray(np.concatenate([ind_pad] * 3, axis=1), BF16)
    gw = NSA_GROUP * HEAD_DIM
    body = functools.partial(_nsa_prompt_body, n_blk=n_blk, n_pick=n_pick, n_kt=nq)
    return pl.pallas_call(
        body,
        grid=(kv_heads, batch, nq),
        in_specs=[pl.BlockSpec((t, gw), lambda k, b, i: (b * nq + i, k)),
                  pl.BlockSpec((1, n_chunks, HEAD_DIM), lambda k, b, i: (b, 0, k)),
                  pl.BlockSpec((1, n_chunks, HEAD_DIM), lambda k, b, i: (b, 0, kv_heads + k)),
                  pl.BlockSpec((seq, HEAD_DIM), lambda k, b, i: (b, 2 * kv_heads + k)),
                  pl.BlockSpec((seq, HEAD_DIM), lambda k, b, i: (b, 3 * kv_heads + k)),
                  pl.BlockSpec((seq, HEAD_DIM), lambda k, b, i: (b, k)),
                  pl.BlockSpec((seq, HEAD_DIM), lambda k, b, i: (b, kv_heads + k)),
                  pl.BlockSpec((nq, 1, gw, t), lambda k, b, i: (0, k, 0, 0)),
                  pl.BlockSpec((1, 1, gw, t), lambda k, b, i: (i, k, 0, 0)),
                  pl.BlockSpec((t, 3 * n_chunks), lambda k, b, i: (0, 0)),
                  pl.BlockSpec((t, seq), lambda k, b, i: (0, 0)),
                  pl.BlockSpec((1, t, LANES), lambda k, b, i: (k, b * nq + i, 0)),
                  pl.BlockSpec((1, 1, gw), lambda k, b, i: (k, 0, 0))],
        out_specs=pl.BlockSpec((t, gw), lambda k, b, i: (b * nq + i, k)),
        out_shape=jax.ShapeDtypeStruct((m, nsw), BF16),
        scratch_shapes=[pltpu.VMEM((nq, t, t), F32)],
        compiler_params=_params("parallel", "parallel", "arbitrary"),
        name="nsa_prompt",
    )(qns, kcvc, kcvc, nsakv16, nsakv16, winkv16, winkv16, tsel, tcmp, ind3,
      jnp.asarray(exp_pad, BF16), gates_k, gain.reshape(kv_heads, 1, gw))


def _cmp_decode_body(tab_ref, q_ref, kcvc_ref, ind_ref, ocmp_ref, score_ref, *, past, kv_heads):
    n_chunks = kcvc_ref.shape[1]
    r = lax.broadcasted_iota(jnp.int32, (1, n_chunks), 1)
    rel = past - (CMP_STRIDE * r + (CMP_LEN - CMP_STRIDE - 1))
    valid = (r >= 1) & (rel >= 0)
    score_ref[...] = jnp.zeros_like(score_ref)
    for k in range(kv_heads):
        kc = kcvc_ref[0, :, k * HEAD_DIM:(k + 1) * HEAD_DIM].astype(BF16)
        vc = kcvc_ref[0, :, (kv_heads + k) * HEAD_DIM:(kv_heads + k + 1) * HEAD_DIM].astype(BF16)
        bias = jnp.concatenate([_bias_chain(rel, tab_ref, k * NSA_GROUP + g) for g in range(NSA_GROUP)], axis=0)
        lg = lax.dot_general(q_ref[0, k], kc, _NT, preferred_element_type=F32) + bias
        p = _masked_softmax(lg, valid)
        ocmp_ref[0, k * NSA_GROUP:(k + 1) * NSA_GROUP, :] = jnp.dot(p.astype(BF16), vc, preferred_element_type=F32)
        p3 = jnp.concatenate(_split3(jnp.sum(p, axis=0, keepdims=True)), axis=1)
        score_ref[0, k:k + 1, :] = jnp.dot(p3, ind_ref[...], preferred_element_type=F32)


def _cmp_decode(table, q, kcvc, ind3, past, kv_heads):
    db = q.shape[0]
    heads = kv_heads * NSA_GROUP
    n_chunks = kcvc.shape[1]
    nj = ind3.shape[1]
    body = functools.partial(_cmp_decode_body, past=past, kv_heads=kv_heads)
    return pl.pallas_call(
        body, grid=(db,),
        in_specs=[pl.BlockSpec(memory_space=pltpu.SMEM),
                  pl.BlockSpec((1, kv_heads, NSA_GROUP, HEAD_DIM), lambda b: (b, 0, 0, 0)),
                  pl.BlockSpec((1, n_chunks, kcvc.shape[2]), lambda b: (b, 0, 0)),
                  pl.BlockSpec(ind3.shape, lambda b: (0, 0))],
        out_specs=[pl.BlockSpec((1, heads, HEAD_DIM), lambda b: (b, 0, 0)),
                   pl.BlockSpec((1, SUBLANES, nj), lambda b: (b, 0, 0))],
        out_shape=[jax.ShapeDtypeStruct((db, heads, HEAD_DIM), F32),
                   jax.ShapeDtypeStruct((db, SUBLANES, nj), F32)],
        compiler_params=_params("parallel"),
        name="nsa_cmp_decode",
    )(table, q, kcvc, ind3)


def _topk_body(s_ref, o_ref, *, n_cache_blk, n_pick):
    s = s_ref[...]
    j = lax.broadcasted_iota(jnp.int32, s.shape, 1)
    lane = lax.broadcasted_iota(jnp.int32, o_ref.shape, 1)
    s = jnp.where((j == 0) | (j == n_cache_blk - 1), BIG, s)
    s = jnp.where(j < n_cache_blk, s, -jnp.inf)
    picks = jnp.zeros(o_ref.shape, jnp.int32)
    for t in range(n_pick):
        m = jnp.max(s, axis=-1, keepdims=True)
        idx = jnp.min(jnp.where(s == m, j, 2 ** 30), axis=-1, keepdims=True)
        picks = jnp.where(lane == t, idx, picks)
        s = jnp.where(j == idx, -jnp.inf, s)
    o_ref[...] = picks


def _topk(scores, n_cache_blk, n_pick):
    rows = scores.shape[0]
    return pl.pallas_call(
        functools.partial(_topk_body, n_cache_blk=n_cache_blk, n_pick=n_pick),
        out_shape=jax.ShapeDtypeStruct((rows, LANES), jnp.int32),
        name="nsa_topk",
    )(scores)


def _sel_decode_body(pt_ref, pk_ref, *refs, n_pick, past, w_buf):
    ks = refs[:n_pick]
    vs = refs[n_pick:2 * n_pick]
    (knew_ref, vnew_ref, wk_ref, wv_ref, wknew_ref, wvnew_ref, q_ref, gates_ref, ocmp_ref, gn_ref,
     tab_ref, o_ref, ksc, vsc, kwc, vwc) = refs[2 * n_pick:]
    b = pl.program_id(0)
    k = pl.program_id(1)
    kv_heads = pl.num_programs(1)
    q = q_ref[0, 0]
    n_slot = n_pick + 1
    sel_len = n_slot * SEL_BLOCK

    first64 = lax.broadcasted_iota(jnp.int32, (SEL_BLOCK, HEAD_DIM), 0) == 0
    for n in range(n_pick):
        ksc[n * SEL_BLOCK:(n + 1) * SEL_BLOCK, :] = ks[n][0, 0].astype(BF16)
        vsc[n * SEL_BLOCK:(n + 1) * SEL_BLOCK, :] = vs[n][0, 0].astype(BF16)
    ksc[n_pick * SEL_BLOCK:sel_len, :] = jnp.where(first64, knew_ref[0], 0.0).astype(BF16)
    vsc[n_pick * SEL_BLOCK:sel_len, :] = jnp.where(first64, vnew_ref[0], 0.0).astype(BF16)

    lane = lax.broadcasted_iota(jnp.int32, (1, sel_len), 1)
    slot = lane // SEL_BLOCK
    base = jnp.full((1, sel_len), past, jnp.int32)
    for n in range(n_pick):
        base = jnp.where(slot == n, pk_ref[(b * kv_heads + k) * n_pick + n] * SEL_BLOCK, base)
    rel = past - (base + (lane & (SEL_BLOCK - 1)))
    bias = jnp.concatenate([_bias_chain(rel, tab_ref, k * NSA_GROUP + g) for g in range(NSA_GROUP)], axis=0)
    lg = lax.dot_general(q, ksc[...], _NT, preferred_element_type=F32) + bias
    p = _masked_softmax(lg, rel >= 0)
    o_sel = jnp.dot(p.astype(BF16), vsc[...], preferred_element_type=F32)

    win_len = kwc.shape[0]
    firstw = lax.broadcasted_iota(jnp.int32, (win_len - w_buf, HEAD_DIM), 0) == 0
    kwc[0:w_buf, :] = wk_ref[0].astype(BF16)
    vwc[0:w_buf, :] = wv_ref[0].astype(BF16)
    kwc[w_buf:win_len, :] = jnp.where(firstw, wknew_ref[0], 0.0).astype(BF16)
    vwc[w_buf:win_len, :] = jnp.where(firstw, wvnew_ref[0], 0.0).astype(BF16)
    relw = w_buf - lax.broadcasted_iota(jnp.int32, (1, win_len), 1)
    bias = jnp.concatenate([_bias_chain(relw, tab_ref, k * NSA_GROUP + g) for g in range(NSA_GROUP)], axis=0)
    lg = lax.dot_general(q, kwc[...], _NT, preferred_element_type=F32) + bias
    p = _masked_softmax(lg, (relw >= 0) & (relw < WINDOW))
    o_win = jnp.dot(p.astype(BF16), vwc[...], preferred_element_type=F32)

    o = gates_ref[0, 0, 0] * ocmp_ref[0, 0] + gates_ref[0, 0, 1] * o_sel + gates_ref[0, 0, 2] * o_win
    o_ref[0, 0] = _rms(o, gn_ref[0])


def _sel_decode(page_table, picks, cache4, nsa_new, win_cache, win_new, q, gates, ocmp, gain, table,
                n_pick, past):
    db, kv_heads = q.shape[:2]
    n_pages = page_table.shape[1]
    w_buf = win_cache.shape[1]
    halves = PAGE_SIZE // SEL_BLOCK

    def pick_spec(n, col0):
        def idx(b, k, pt, pk):
            j = pk[(b * kv_heads + k) * n_pick + n]
            return (pt[b * n_pages + j // halves], j % halves, 0, col0 + k)
        return pl.BlockSpec((1, 1, SEL_BLOCK, HEAD_DIM), idx)

    def row_spec(col0):
        return pl.BlockSpec((1, 1, HEAD_DIM), lambda b, k, pt, pk: (b, 0, col0 + k))

    def head_spec(shape):
        nd = len(shape)
        return pl.BlockSpec((1, 1) + shape, lambda b, k, pt, pk: (b, k) + (0,) * nd)

    win_len = w_buf + LANES
    grid_spec = pltpu.PrefetchScalarGridSpec(
        num_scalar_prefetch=2,
        grid=(db, kv_heads),
        in_specs=[pick_spec(n, 2 * kv_heads) for n in range(n_pick)]
        + [pick_spec(n, 3 * kv_heads) for n in range(n_pick)]
        + [row_spec(2 * kv_heads), row_spec(3 * kv_heads),
           pl.BlockSpec((1, w_buf, HEAD_DIM), lambda b, k, pt, pk: (b, 0, k)),
           pl.BlockSpec((1, w_buf, HEAD_DIM), lambda b, k, pt, pk: (b, 0, kv_heads + k)),
           row_spec(0), row_spec(kv_heads),
           head_spec((NSA_GROUP, HEAD_DIM)), head_spec((3, NSA_GROUP, 1)), head_spec((NSA_GROUP, HEAD_DIM)),
           pl.BlockSpec((1, NSA_GROUP, HEAD_DIM), lambda b, k, pt, pk: (k, 0, 0)),
           pl.BlockSpec(memory_space=pltpu.SMEM)],
        out_specs=head_spec((NSA_GROUP, HEAD_DIM)),
        scratch_shapes=[pltpu.VMEM(((n_pick + 1) * SEL_BLOCK, HEAD_DIM), BF16),
                        pltpu.VMEM(((n_pick + 1) * SEL_BLOCK, HEAD_DIM), BF16),
                        pltpu.VMEM((win_len, HEAD_DIM), BF16),
                        pltpu.VMEM((win_len, HEAD_DIM), BF16)])
    body = functools.partial(_sel_decode_body, n_pick=n_pick, past=past, w_buf=w_buf)
    return pl.pallas_call(
        body, grid_spec=grid_spec,
        out_shape=jax.ShapeDtypeStruct((db, kv_heads, NSA_GROUP, HEAD_DIM), F32),
        compiler_params=_params("parallel", "parallel"),
        name="nsa_sel_decode",
    )(page_table.reshape(-1), picks, *([cache4] * (2 * n_pick)), nsa_new, nsa_new, win_cache, win_cache,
      win_new, win_new, q, gates, ocmp, gain.reshape(kv_heads, NSA_GROUP, HEAD_DIM), table)


def _out_proj_body(x_ref, a_ref, b_ref, w_ref, o_ref):
    half = a_ref.shape[1]
    o_ref[...] = (x_ref[...] + jnp.dot(a_ref[...], w_ref[0:half, :], preferred_element_type=F32)
                  + jnp.dot(b_ref[...], w_ref[half:2 * half, :], preferred_element_type=F32))


def _out_proj(x, a, b, w, tm):
    m, d = x.shape
    half = a.shape[1]
    return pl.pallas_call(
        _out_proj_body, grid=(m // tm,),
        in_specs=[pl.BlockSpec((tm, d), lambda i: (i, 0)),
                  pl.BlockSpec((tm, half), lambda i: (i, 0)),
                  pl.BlockSpec((tm, half), lambda i: (i, 0)),
                  _resident(w.shape)],
        out_specs=pl.BlockSpec((tm, d), lambda i: (i, 0)),
        out_shape=jax.ShapeDtypeStruct((m, d), F32),
        compiler_params=_params("parallel"),
        name="out_proj",
    )(x, a, b, w)


def _norm_matmul_body(x_ref, g_ref, w_ref, o_ref):
    o_ref[...] = jnp.dot(_rms(x_ref[...], g_ref[...]).astype(BF16), w_ref[...], preferred_element_type=F32)


def _norm_matmul(x, g, w, tm):
    m, d = x.shape
    n = w.shape[1]
    return pl.pallas_call(
        _norm_matmul_body, grid=(m // tm,),
        in_specs=[pl.BlockSpec((tm, d), lambda i: (i, 0)),
                  pl.BlockSpec((1, d), lambda i: (0, 0)),
                  _resident(w.shape)],
        out_specs=pl.BlockSpec((tm, n), lambda i: (i, 0)),
        out_shape=jax.ShapeDtypeStruct((m, n), F32),
        compiler_params=_params("parallel"),
        name="mem_kv_proj",
    )(x, g.reshape(1, d), w)


def _cross_body(x_ref, g_ref, wq_ref, wo_ref, mem_ref, o_ref, *, scale):
    x = x_ref[0]
    hb = _rms(x, g_ref[...]).astype(BF16)
    qh = (jnp.dot(hb, wq_ref[...], preferred_element_type=F32) * scale).astype(BF16)
    width = MEM_HEADS * HEAD_DIM
    outs = []
    for h in range(MEM_HEADS):
        k = mem_ref[0, :, h * HEAD_DIM:(h + 1) * HEAD_DIM].astype(BF16)
        v = mem_ref[0, :, width + h * HEAD_DIM:width + (h + 1) * HEAD_DIM].astype(BF16)
        lg = lax.dot_general(qh[:, h * HEAD_DIM:(h + 1) * HEAD_DIM], k, _NT, preferred_element_type=F32)
        e = jnp.exp(lg - jnp.max(lg, axis=-1, keepdims=True))
        p = e / jnp.sum(e, axis=-1, keepdims=True)
        outs.append(jnp.dot(p.astype(BF16), v, preferred_element_type=F32).astype(BF16))
    o = jnp.concatenate(outs, axis=1)
    o_ref[0] = x + jnp.dot(o, wo_ref[...], preferred_element_type=F32)


def _cross(x3, g, wq, wo, mem, tm):
    nb, t, d = x3.shape
    n_mem, mw = mem.shape[1:]
    body = functools.partial(_cross_body, scale=1.0 / math.sqrt(HEAD_DIM))
    return pl.pallas_call(
        body, grid=(nb, t // tm),
        in_specs=[pl.BlockSpec((1, tm, d), lambda b, i: (b, i, 0)),
                  pl.BlockSpec((1, d), lambda b, i: (0, 0)),
                  _resident(wq.shape), _resident(wo.shape),
                  pl.BlockSpec((1, n_mem, mw), lambda b, i: (b, 0, 0))],
        out_specs=pl.BlockSpec((1, tm, d), lambda b, i: (b, i, 0)),
        out_shape=jax.ShapeDtypeStruct((nb, t, d), F32),
        compiler_params=_params("parallel", "parallel"),
        name="cross_attn",
    )(x3, g.reshape(1, d), wq, wo, mem)


def _mlp_body(x_ref, g_ref, wu_ref, wd_ref, gf_ref, o_ref, h_ref, acc_ref):
    f = pl.program_id(1)

    @pl.when(f == 0)
    def _():
        h_ref[...] = _rms(x_ref[...], g_ref[...]).astype(BF16)
        acc_ref[...] = jnp.zeros_like(acc_ref)

    u = jnp.maximum(jnp.dot(h_ref[...], wu_ref[...], preferred_element_type=F32), 0.0)
    acc_ref[...] += jnp.dot((u * u).astype(BF16), wd_ref[...], preferred_element_type=F32)

    @pl.when(f == pl.num_programs(1) - 1)
    def _():
        o_ref[...] = _rms(x_ref[...] + acc_ref[...], gf_ref[...])


def _mlp_final(x, g, wu, wd, gf, tm, tf):
    m, d = x.shape
    dff = wu.shape[1]
    return pl.pallas_call(
        _mlp_body, grid=(m // tm, dff // tf),
        in_specs=[pl.BlockSpec((tm, d), lambda i, f: (i, 0)),
                  pl.BlockSpec((1, d), lambda i, f: (0, 0)),
                  pl.BlockSpec((d, tf), lambda i, f: (0, f)),
                  pl.BlockSpec((tf, d), lambda i, f: (f, 0)),
                  pl.BlockSpec((1, d), lambda i, f: (0, 0))],
        out_specs=pl.BlockSpec((tm, d), lambda i, f: (i, 0)),
        out_shape=jax.ShapeDtypeStruct((m, d), F32),
        scratch_shapes=[pltpu.VMEM((tm, d), BF16), pltpu.VMEM((tm, d), F32)],
        compiler_params=_params("parallel", "arbitrary"),
        name="mlp_final",
    )(x, g.reshape(1, d), wu, wd, gf.reshape(1, d))


def _w1cat(w1):
    half = (CMP_LEN // 2) * HEAD_DIM
    hid = w1.shape[1]
    a = w1[:half].reshape(CMP_STRIDE // 2, 2 * HEAD_DIM, hid)
    b = w1[half:].reshape(CMP_STRIDE // 2, 2 * HEAD_DIM, hid)
    return jnp.concatenate([a, b], axis=-1).astype(BF16)


def kernel(x_prompt, x_sample, cache_sb_kv, cache_nsa_kv, cache_win_kv, cache_mem_kv, page_table, mem_prompt,
           norm_mix, w_in, norm_sb_out, norm_nsa_out, w_out, rel_bias_table,
           cmp_pe_k, cmp_pe_v, w_cmp_k1, w_cmp_k2, w_cmp_v1, w_cmp_v2,
           norm_cross, norm_mem, w_cross_q, w_mem_kv, w_cross_o,
           norm_ffn, w_up, w_down, norm_final):
    batch, seq, d = x_prompt.shape
    db, dec_seq, _ = x_sample.shape
    assert dec_seq == 1
    depth = w_in.shape[0]
    assert depth == 1, "the final norm is fused into the MLP of the only layer"
    n_pool = cache_sb_kv.shape[1]
    n_pages = page_table.shape[1]
    past = n_pages * PAGE_SIZE
    sb_heads = cache_sb_kv.shape[4]
    kv_heads = cache_nsa_kv.shape[4]
    nsa_heads = kv_heads * NSA_GROUP
    sbw = sb_heads * HEAD_DIM
    nsw = nsa_heads * HEAD_DIM
    kvw = kv_heads * HEAD_DIM
    n_gates = 3 * nsa_heads
    w_buf = cache_win_kv.shape[2]
    win_keep = min(WINDOW, seq)
    n_mem = mem_prompt.shape[1]
    mem_w = MEM_HEADS * HEAD_DIM
    m = batch * seq
    l = 0

    w_in_p = jnp.pad(w_in[l].astype(BF16), ((0, 0), (0, LANES - n_gates)))
    w1cat = jnp.stack([_w1cat(w_cmp_k1[l]), _w1cat(w_cmp_v1[l])])
    w2 = jnp.stack([w_cmp_k2[l], w_cmp_v2[l]]).astype(BF16)
    pe = jnp.stack([cmp_pe_k[l].reshape(1, -1), cmp_pe_v[l].reshape(1, -1)])
    pe_term = _pe_term(jnp.broadcast_to(pe, (2, SUBLANES, pe.shape[-1])), jnp.stack([w_cmp_k1[l], w_cmp_v1[l]]))
    w_out_b = w_out[l].astype(BF16)

    xp = x_prompt.reshape(m, d)
    (qsb, sbkv, sbkv16, qns, nsakv, nsakv16, winkv, winkv16, gates) = _project(xp, norm_mix[l], w_in_p, 256)
    o_sb = _sb_prompt(qsb, sbkv16, norm_sb_out[l], batch, seq)

    pages_seq = seq // PAGE_SIZE
    prompt_pages = jnp.arange(batch * pages_seq, dtype=jnp.int32).reshape(batch, pages_seq)
    kcvc = _compress(nsakv.reshape(batch * pages_seq, PAGE_SIZE, 4 * kvw), prompt_pages, w1cat, w2, pe_term,
                     kv_heads, pages_seq)
    nq = seq // LANES
    tsel = _bias_tiles(rel_bias_table, nq, LANES, 1, 0)
    tcmp = _bias_tiles(rel_bias_table, nq, LANES, CMP_STRIDE, CMP_LEN - CMP_STRIDE - 1)
    gates_k = jnp.pad(gates[:, :n_gates].reshape(m, kv_heads, 3 * NSA_GROUP).transpose(1, 0, 2),
                      ((0, 0), (0, 0), (0, LANES - 3 * NSA_GROUP)))
    o_ns = _nsa_prompt(qns, kcvc, nsakv16, winkv16, tsel, tcmp, gates_k, norm_nsa_out[l], batch, seq)
    xp = _out_proj(xp, o_sb, o_ns, w_out_b, 512)

    sb_kv_prompt = sbkv.reshape(1, batch, seq, 2, sb_heads, HEAD_DIM)
    nsa_kv_prompt = nsakv.reshape(1, batch, seq, 4, kv_heads, HEAD_DIM)
    win_kv_prompt = winkv.reshape(1, batch, seq, 2, kv_heads, HEAD_DIM)[:, :, seq - win_keep:]

    xs = x_sample.reshape(db, d)
    (qsb_s, sbkv_s, _, qns_s, nsakv_s, _, winkv_s, _, gates_s) = _project(xs, norm_mix[l], w_in_p, db)
    eye = jnp.eye(sb_heads, dtype=BF16)
    q_bd = (qsb_s.reshape(db, sb_heads, 1, HEAD_DIM) * eye[None, :, :, None]).reshape(db, sb_heads, sbw)
    o_sb_s = _sb_decode(q_bd, cache_sb_kv[l].reshape(n_pool, PAGE_SIZE, 2 * sbw), page_table,
                        norm_sb_out[l], 8)

    nsa_cache = cache_nsa_kv[l].reshape(n_pool, PAGE_SIZE, 4 * kvw)
    kcvc_s = _compress(nsa_cache, page_table, w1cat, w2, pe_term, kv_heads, 16)
    n_chunks_s = past // CMP_STRIDE
    tk = past + 1
    n_cmp_s = (tk - CMP_LEN) // CMP_STRIDE + 1
    n_blk_s = -(-tk // SEL_BLOCK)
    n_cache_blk = past // SEL_BLOCK
    assert n_blk_s == n_cache_blk + 1 and n_cache_blk >= N_SEL
    nj = -(-n_blk_s // LANES) * LANES
    ind_s, _ = _sel_constants(n_chunks_s, n_blk_s, n_cmp_s, 0)
    ind_s_pad = np.zeros((nj, n_chunks_s), np.float32)
    ind_s_pad[:n_blk_s] = ind_s
    ind3_s = jnp.asarray(np.concatenate([ind_s_pad.T] * 3, axis=0), BF16)
    q_dec = qns_s.reshape(db, kv_heads, NSA_GROUP, HEAD_DIM)
    ocmp_s, scores = _cmp_decode(rel_bias_table, q_dec, kcvc_s, ind3_s, past, kv_heads)
    n_pick_cache = N_SEL - 1
    picks = _topk(scores[:, :kv_heads].reshape(db * kv_heads, nj), n_cache_blk, n_pick_cache)
    picks = picks[:, :n_pick_cache].reshape(-1)
    gates_d = gates_s[:, :n_gates].reshape(db, kv_heads, NSA_GROUP, 3).transpose(0, 1, 3, 2)[..., None]
    o_ns_s = _sel_decode(page_table, picks,
                         cache_nsa_kv[l].reshape(n_pool, PAGE_SIZE // SEL_BLOCK, SEL_BLOCK, 4 * kvw),
                         nsakv_s.reshape(db, 1, 4 * kvw), cache_win_kv[l].reshape(db, w_buf, 2 * kvw),
                         winkv_s.reshape(db, 1, 2 * kvw), q_dec, gates_d, ocmp_s.reshape(db, kv_heads, NSA_GROUP, HEAD_DIM), norm_nsa_out[l],
                         rel_bias_table, n_pick_cache, past)
    xs = _out_proj(xs, o_sb_s.reshape(db, sbw).astype(BF16), o_ns_s.reshape(db, nsw).astype(BF16), w_out_b, db)

    sb_kv_sample = sbkv_s.reshape(1, db, 1, 2, sb_heads, HEAD_DIM)
    nsa_kv_sample = nsakv_s.reshape(1, db, 1, 4, kv_heads, HEAD_DIM)
    win_new = winkv_s.reshape(db, 1, 2, kv_heads, HEAD_DIM)
    win_kv_sample = jnp.concatenate([cache_win_kv[l], win_new], axis=1)[None, :, 1:]

    mem_kv = _norm_matmul(mem_prompt.reshape(batch * n_mem, d), norm_mem[l], w_mem_kv[l].astype(BF16), n_mem)
    wq = w_cross_q[l].astype(BF16)
    wo = w_cross_o[l].astype(BF16)
    xp = _cross(xp.reshape(batch, seq, d), norm_cross[l], wq, wo, mem_kv.reshape(batch, n_mem, 2 * mem_w), 512)
    xs = _cross(xs.reshape(db, 1, d), norm_cross[l], wq, wo, cache_mem_kv[l].reshape(db, n_mem, 2 * mem_w), 1)
    mem_kv_prompt = mem_kv.reshape(1, batch, n_mem, 2, MEM_HEADS, HEAD_DIM)

    wu = w_up[l].astype(BF16)
    wd = w_down[l].astype(BF16)
    y_prompt = _mlp_final(xp.reshape(m, d), norm_ffn[l], wu, wd, norm_final, 512, 512).reshape(batch, seq, d)
    y_sample = _mlp_final(xs.reshape(db, d), norm_ffn[l], wu, wd, norm_final, db, 512).reshape(db, 1, d)

    return (y_prompt, y_sample, sb_kv_prompt, sb_kv_sample, nsa_kv_prompt, nsa_kv_sample,
            win_kv_prompt, win_kv_sample, mem_kv_prompt)
```

```python
import functools
import math

import numpy as np
import jax
import jax.numpy as jnp
from jax import lax
from jax.experimental import pallas as pl
from jax.experimental.pallas import tpu as pltpu

HEAD_DIM = 128
NSA_GROUP = 4
MEM_HEADS = 4
PAGE_SIZE = 128
CMP_LEN = 32
CMP_STRIDE = 16
SEL_BLOCK = 64
N_SEL = 16
WINDOW = 512
N_BUCKETS = 32
MAX_DISTANCE = 1024
RMS_EPS = 1e-6
NEG_INF = -1e30
BIG = 1e30
LANES = 128
SUBLANES = 8
VMEM_LIMIT = 56 * 1024 * 1024

F32 = jnp.float32
BF16 = jnp.bfloat16
_NT = (((1,), (1,)), ((), ()))


def _bucket_thresholds():
    max_exact = N_BUCKETS // 2
    n_log = N_BUCKETS - max_exact
    ratio = MAX_DISTANCE // max_exact
    th = list(range(max_exact)) + [max_exact]
    for m in range(1, n_log):
        n = th[-1]
        while n ** n_log < (max_exact ** n_log) * (ratio ** m):
            n += 1
        th.append(n)
    return tuple(th)


THRESHOLDS = _bucket_thresholds()


def _bias_chain(rel, tab_ref, h):
    v = jnp.full(rel.shape, tab_ref[0, h], F32)
    for b in range(1, N_BUCKETS):
        v = jnp.where(rel >= THRESHOLDS[b], tab_ref[b, h], v)
    return v


def _rms(x, g):
    return x * lax.rsqrt(jnp.mean(x * x, axis=-1, keepdims=True) + RMS_EPS) * g


def _softplus(z):
    return jnp.maximum(z, 0.0) + jnp.log(1.0 + jnp.exp(-jnp.abs(z)))


def _split2(x):
    hi = x.astype(BF16)
    lo = (x - hi.astype(F32)).astype(BF16)
    return hi, lo


def _split3(x):
    hi = x.astype(BF16)
    r = x - hi.astype(F32)
    mid = r.astype(BF16)
    lo = (r - mid.astype(F32)).astype(BF16)
    return hi, mid, lo


def _masked_softmax(lg, valid):
    lgm = jnp.where(valid, lg, NEG_INF)
    m = jnp.max(lgm, axis=-1, keepdims=True)
    e = jnp.where(valid, jnp.exp(lgm - m), 0.0)
    s = jnp.sum(e, axis=-1, keepdims=True)
    return e / jnp.where(s > 0.0, s, 1.0)


def _params(*sem):
    return pltpu.CompilerParams(dimension_semantics=sem, vmem_limit_bytes=VMEM_LIMIT)


def _resident(shape):
    nd = len(shape)
    return pl.BlockSpec(shape, lambda *_: (0,) * nd, pipeline_mode=pl.Buffered(1))


def _proj_body(x_ref, g_ref, w_ref, qsb_ref, sbkv_ref, sbkv16_ref, qns_ref, nsakv_ref,
               nsakv16_ref, winkv_ref, winkv16_ref, gates_ref, *, sbw, nsw, kvw, scale):
    hb = _rms(x_ref[...], g_ref[...]).astype(BF16)

    def mm(c0, n):
        return jnp.dot(hb, w_ref[:, c0:c0 + n], preferred_element_type=F32)

    c = 0
    qsb_ref[...] = (mm(c, sbw) * scale).astype(BF16)
    c += sbw
    for half in range(2):
        kv = mm(c, sbw)
        sbkv_ref[:, half * sbw:(half + 1) * sbw] = kv
        sbkv16_ref[:, half * sbw:(half + 1) * sbw] = kv.astype(BF16)
        c += sbw
    qns_ref[...] = (mm(c, nsw) * scale).astype(BF16)
    c += nsw
    kv = mm(c, 4 * kvw)
    nsakv_ref[...] = kv
    nsakv16_ref[...] = kv.astype(BF16)
    c += 4 * kvw
    kv = mm(c, 2 * kvw)
    winkv_ref[...] = kv
    winkv16_ref[...] = kv.astype(BF16)
    c += 2 * kvw
    gates_ref[...] = jax.nn.sigmoid(mm(c, LANES))


def _project(x, g, w_pad, tm):
    m, d = x.shape
    sbw = nsw = d // 2
    kvw = nsw // NSA_GROUP
    widths = (sbw, 2 * sbw, 2 * sbw, nsw, 4 * kvw, 4 * kvw, 2 * kvw, 2 * kvw, LANES)
    dtypes = (BF16, F32, BF16, BF16, F32, BF16, F32, BF16, F32)
    body = functools.partial(_proj_body, sbw=sbw, nsw=nsw, kvw=kvw, scale=1.0 / math.sqrt(HEAD_DIM))
    return pl.pallas_call(
        body,
        grid=(m // tm,),
        in_specs=[pl.BlockSpec((tm, d), lambda i: (i, 0)),
                  pl.BlockSpec((1, d), lambda i: (0, 0)),
                  _resident(w_pad.shape)],
        out_specs=[pl.BlockSpec((tm, w), lambda i: (i, 0)) for w in widths],
        out_shape=[jax.ShapeDtypeStruct((m, w), dt) for w, dt in zip(widths, dtypes)],
        compiler_params=_params("parallel"),
        name="in_proj",
    )(x, g.reshape(1, d), w_pad)


def _bias_tile_body(tab_ref, o_ref, *, q_stride, k_stride, k_off):
    m = pl.program_id(0)
    h = pl.program_id(1)
    i = lax.broadcasted_iota(jnp.int32, (LANES, LANES), 0)
    j = lax.broadcasted_iota(jnp.int32, (LANES, LANES), 1)
    rel = m * q_stride + i - (k_stride * j + k_off)
    o_ref[0, 0] = _bias_chain(rel, tab_ref, h)


def _bias_tiles(table, n_tiles, q_stride, k_stride, k_off):
    n_heads = table.shape[1]
    body = functools.partial(_bias_tile_body, q_stride=q_stride, k_stride=k_stride, k_off=k_off)
    out = pl.pallas_call(
        body,
        grid=(n_tiles, n_heads),
        in_specs=[pl.BlockSpec(memory_space=pltpu.SMEM)],
        out_specs=pl.BlockSpec((1, 1, LANES, LANES), lambda m, h: (m, h, 0, 0)),
        out_shape=jax.ShapeDtypeStruct((n_tiles, n_heads, LANES, LANES), F32),
        compiler_params=_params("parallel", "parallel"),
        name="bias_tiles",
    )(table)
    return out.reshape(n_tiles, n_heads // NSA_GROUP, NSA_GROUP * LANES, LANES)


def _cumsum_rhs(t):
    u = np.tril(np.ones((t, t), np.float32))
    half = np.concatenate([u, np.ones((t, t), np.float32)], axis=1)
    return jnp.asarray(np.concatenate([half, half], axis=0), BF16)


def _sb_tile(q, k, v, uo, carry, acc, valid):
    t = q.shape[0]
    sub = uo.shape[1] // 2
    z = lax.dot_general(q, k, _NT, preferred_element_type=F32)
    sp = _softplus(z)
    if valid is not None:
        sp = jnp.where(valid, sp, 0.0)
    parts = []
    for s in reversed(range(t // sub)):
        hi, lo = _split2(sp[:, s * sub:(s + 1) * sub])
        ct = jnp.dot(jnp.concatenate([hi, lo], axis=1), uo, preferred_element_type=F32)
        parts.insert(0, jnp.exp(z[:, s * sub:(s + 1) * sub] - (ct[:, :sub] + carry)))
        carry = carry + ct[:, sub:]
    a = jnp.concatenate(parts, axis=1)
    if valid is not None:
        a = jnp.where(valid, a, 0.0)
    acc = acc + jnp.dot(a.astype(BF16), v, preferred_element_type=F32)
    return carry, acc


def _sb_prompt_body(q_ref, k_ref, v_ref, g_ref, uo_ref, o_ref, *, tq, hps):
    qi = pl.program_id(2)
    uo = uo_ref[...]
    row = lax.broadcasted_iota(jnp.int32, (tq, tq), 0)
    col = lax.broadcasted_iota(jnp.int32, (tq, tq), 1)
    lanes = [slice(h * HEAD_DIM, (h + 1) * HEAD_DIM) for h in range(hps)]
    qs = [q_ref[:, sl] for sl in lanes]

    def tiles(start, state, valid):
        out = []
        for h, sl in enumerate(lanes):
            out.extend(_sb_tile(qs[h], k_ref[pl.ds(start, tq), sl], v_ref[pl.ds(start, tq), sl], uo,
                                state[2 * h], state[2 * h + 1], valid))
        return tuple(out)

    state = (jnp.zeros((tq, uo.shape[1] // 2), F32), jnp.zeros((tq, HEAD_DIM), F32)) * hps
    state = tiles(pl.multiple_of(qi * tq, tq), state, col < row)
    state = lax.fori_loop(0, qi, lambda t, st: tiles(pl.multiple_of((qi - 1 - t) * tq, tq), st, None), state)
    for h, sl in enumerate(lanes):
        o_ref[:, sl] = _rms(state[2 * h + 1], g_ref[0, :, sl]).astype(BF16)


def _sb_prompt(qsb, sbkv16, gain, batch, seq):
    m, sbw = qsb.shape
    heads = sbw // HEAD_DIM
    tq = 2 * LANES
    hps = 2
    nq = seq // tq
    wide = hps * HEAD_DIM
    groups = heads // hps
    return pl.pallas_call(
        functools.partial(_sb_prompt_body, tq=tq, hps=hps),
        grid=(batch, groups, nq),
        in_specs=[pl.BlockSpec((tq, wide), lambda b, h, i: (b * nq + i, h)),
                  pl.BlockSpec((seq, wide), lambda b, h, i: (b, h)),
                  pl.BlockSpec((seq, wide), lambda b, h, i: (b, groups + h)),
                  pl.BlockSpec((1, 1, wide), lambda b, h, i: (h, 0, 0)),
                  pl.BlockSpec((2 * LANES, 2 * LANES), lambda b, h, i: (0, 0))],
        out_specs=pl.BlockSpec((tq, wide), lambda b, h, i: (b * nq + i, h)),
        out_shape=jax.ShapeDtypeStruct((m, sbw), BF16),
        compiler_params=_params("parallel", "parallel", "arbitrary"),
        name="sb_prompt",
    )(qsb, sbkv16, sbkv16, gain.reshape(groups, 1, wide), _cumsum_rhs(LANES))


def _sb_decode_body(pt_ref, *refs, n_pages_step, n_steps, heads):
    pages = refs[:n_pages_step]
    q_ref, uo_ref, g_ref, o_ref, acc_ref, carry_ref = refs[n_pages_step:]
    s = pl.program_id(1)
    rows_tok = 2 * heads

    @pl.when(s == 0)
    def _():
        acc_ref[...] = jnp.zeros_like(acc_ref)
        carry_ref[...] = jnp.zeros_like(carry_ref)

    q = q_ref[0]
    uo = uo_ref[...]
    rowi = lax.broadcasted_iota(jnp.int32, (heads, HEAD_DIM), 0)
    zs = []
    for p in range(n_pages_step):
        z = jnp.zeros((heads, PAGE_SIZE), F32)
        for h in range(heads):
            k = pages[p][pl.ds(h, PAGE_SIZE, stride=rows_tok), :].astype(BF16)
            z = jnp.where(rowi == h, lax.dot_general(q, k, _NT, preferred_element_type=F32), z)
        zs.append(z)
    cts = []
    for z in zs:
        hi, lo = _split2(_softplus(z))
        cts.append(jnp.dot(jnp.concatenate([hi, lo], axis=1), uo, preferred_element_type=F32))
    carry = carry_ref[...]
    acc = acc_ref[...]
    for p in range(n_pages_step):
        a = jnp.exp(zs[p] - (cts[p][:, :PAGE_SIZE] + carry)).astype(BF16)
        carry = carry + cts[p][:, PAGE_SIZE:]
        for h in range(heads):
            v = pages[p][pl.ds(heads + h, PAGE_SIZE, stride=rows_tok), :].astype(BF16)
            acc = acc + jnp.where(rowi == h, jnp.dot(a, v, preferred_element_type=F32), 0.0)
    acc_ref[...] = acc
    carry_ref[...] = carry

    @pl.when(s == n_steps - 1)
    def _():
        o_ref[0] = _rms(acc_ref[...], g_ref[...])


def _sb_decode(q, cache_rows, page_table, gain, pages_per_step):
    db, heads, _ = q.shape
    n_pages = page_table.shape[1]
    n_steps = n_pages // pages_per_step
    page_rows = PAGE_SIZE * 2 * heads

    def page_spec(p):
        def idx(b, s, pt):
            return (pt[b * n_pages + n_pages - 1 - (s * pages_per_step + p)], 0)
        return pl.BlockSpec((page_rows, HEAD_DIM), idx)

    grid_spec = pltpu.PrefetchScalarGridSpec(
        num_scalar_prefetch=1,
        grid=(db, n_steps),
        in_specs=[page_spec(p) for p in range(pages_per_step)] + [
            pl.BlockSpec((1, heads, HEAD_DIM), lambda b, s, pt: (b, 0, 0)),
            pl.BlockSpec((2 * PAGE_SIZE, 2 * PAGE_SIZE), lambda b, s, pt: (0, 0)),
            pl.BlockSpec((heads, HEAD_DIM), lambda b, s, pt: (0, 0))],
        out_specs=pl.BlockSpec((1, heads, HEAD_DIM), lambda b, s, pt: (b, 0, 0)),
        scratch_shapes=[pltpu.VMEM((heads, HEAD_DIM), F32), pltpu.VMEM((heads, PAGE_SIZE), F32)])
    body = functools.partial(_sb_decode_body, n_pages_step=pages_per_step, n_steps=n_steps, heads=heads)
    return pl.pallas_call(
        body, grid_spec=grid_spec,
        out_shape=jax.ShapeDtypeStruct((db, heads, HEAD_DIM), F32),
        compiler_params=_params("parallel", "arbitrary"),
        name="sb_decode",
    )(page_table.reshape(-1), *([cache_rows] * pages_per_step), q, _cumsum_rhs(PAGE_SIZE),
      gain.reshape(heads, HEAD_DIM))


def _pe_term_body(pe_ref, w_ref, o_ref):
    o_ref[0] = jnp.dot(pe_ref[0].astype(BF16), w_ref[0].astype(BF16), preferred_element_type=F32)


def _pe_term(pe, w1):
    n, _, kdim = pe.shape
    hid = w1.shape[-1]
    return pl.pallas_call(
        _pe_term_body, grid=(n,),
        in_specs=[pl.BlockSpec((1, SUBLANES, kdim), lambda t: (t, 0, 0)),
                  pl.BlockSpec((1, kdim, hid), lambda t: (t, 0, 0))],
        out_specs=pl.BlockSpec((1, SUBLANES, hid), lambda t: (t, 0, 0)),
        out_shape=jax.ShapeDtypeStruct((n, SUBLANES, hid), F32),
        compiler_params=_params("parallel"),
        name="cmp_pe_term",
    )(pe, w1)


def _compress_body(pt_ref, *refs, n_pages_step, kv_heads, row_layout):
    pages = refs[:n_pages_step]
    w1_ref, w2_ref, pe_ref, o_ref, carry_ref, tok_ref = refs[n_pages_step:]
    chunks_page = PAGE_SIZE // CMP_STRIDE
    rows = n_pages_step * chunks_page
    step_tokens = n_pages_step * PAGE_SIZE
    rows_tok = 4 * kv_heads

    for idx in range(2 * kv_heads):
        for p in range(n_pages_step):
            if row_layout:
                plane = pages[p][pl.ds(idx, PAGE_SIZE, stride=rows_tok), :]
            else:
                plane = pages[p][0, :, idx * HEAD_DIM:(idx + 1) * HEAD_DIM]
            tok_ref[pl.ds(idx * step_tokens + p * PAGE_SIZE, PAGE_SIZE), :] = plane

    @pl.when(pl.program_id(1) == 0)
    def _():
        carry_ref[...] = jnp.zeros_like(carry_ref)

    rowi = lax.broadcasted_iota(jnp.int32, (rows, HEAD_DIM), 0)
    for t in range(2):
        for k in range(kv_heads):
            idx = t * kv_heads + k
            c0 = idx * HEAD_DIM
            acc = None
            for lp in range(CMP_STRIDE // 2):
                halves = [tok_ref[pl.ds(idx * step_tokens + 2 * lp + dl, rows, stride=CMP_STRIDE), :]
                          for dl in range(2)]
                x = jnp.concatenate(halves, axis=1).astype(BF16)
                d = jnp.dot(x, w1_ref[t, lp], preferred_element_type=F32)
                acc = d if acc is None else acc + d
            first = acc[:, :HEAD_DIM]
            second = acc[:, HEAD_DIM:]
            prev_first = jnp.where(rowi == 0, carry_ref[idx], pltpu.roll(first, 1, axis=0))
            carry_ref[idx] = first[rows - 1:rows, :]
            hid = jax.nn.gelu(prev_first + second + pe_ref[t, 0:1, :])
            o_ref[0, :, c0:c0 + HEAD_DIM] = jnp.dot(hid.astype(BF16), w2_ref[t], preferred_element_type=F32)


def _compress(cache, page_table, w1cat, w2, pe_term, kv_heads, pages_per_step):
    nb, n_pages = page_table.shape
    n_steps = n_pages // pages_per_step
    chunks = n_pages * (PAGE_SIZE // CMP_STRIDE)
    rows_step = pages_per_step * (PAGE_SIZE // CMP_STRIDE)
    width = 2 * kv_heads * HEAD_DIM
    row_layout = cache.ndim == 2

    def page_spec(p):
        if row_layout:
            return pl.BlockSpec((PAGE_SIZE * 4 * kv_heads, HEAD_DIM),
                                lambda b, s, pt: (pt[b * n_pages + s * pages_per_step + p], 0))
        return pl.BlockSpec((1, PAGE_SIZE, width),
                            lambda b, s, pt: (pt[b * n_pages + s * pages_per_step + p], 0, 0))

    grid_spec = pltpu.PrefetchScalarGridSpec(
        num_scalar_prefetch=1,
        grid=(nb, n_steps),
        in_specs=[page_spec(p) for p in range(pages_per_step)] + [
            pl.BlockSpec(w1cat.shape, lambda b, s, pt: (0, 0, 0, 0)),
            pl.BlockSpec(w2.shape, lambda b, s, pt: (0, 0, 0)),
            pl.BlockSpec(pe_term.shape, lambda b, s, pt: (0, 0, 0))],
        out_specs=pl.BlockSpec((1, rows_step, width), lambda b, s, pt: (b, s, 0)),
        scratch_shapes=[pltpu.VMEM((2 * kv_heads, 1, HEAD_DIM), F32),
                        pltpu.VMEM((2 * kv_heads * pages_per_step * PAGE_SIZE, HEAD_DIM), F32)])
    body = functools.partial(_compress_body, n_pages_step=pages_per_step, kv_heads=kv_heads,
                             row_layout=row_layout)
    return pl.pallas_call(
        body, grid_spec=grid_spec,
        out_shape=jax.ShapeDtypeStruct((nb, chunks, width), F32),
        compiler_params=_params("parallel", "arbitrary"),
        name="nsa_compress",
    )(page_table.reshape(-1), *([cache] * pages_per_step), w1cat, w2, pe_term)


def _flash_tile(q, k, v, bias, valid, m, l, acc):
    lg = lax.dot_general(q, k, _NT, preferred_element_type=F32) + bias
    lgm = jnp.where(valid, lg, NEG_INF)
    m_new = jnp.maximum(m, jnp.max(lgm, axis=-1, keepdims=True))
    p = jnp.where(valid, jnp.exp(lgm - m_new), 0.0)
    alpha = jnp.exp(m - m_new)
    l = alpha * l + jnp.sum(p, axis=-1, keepdims=True)
    acc = alpha * acc + jnp.dot(p.astype(BF16), v, preferred_element_type=F32)
    return m_new, l, acc


def _nsa_prompt_body(q_ref, kc_ref, vc_ref, ks_ref, vs_ref, kw_ref, vw_ref, tsel_ref, tcmp_ref,
                     ind_ref, exp_ref, gates_ref, gn_ref, o_ref, mask_ref, *, n_blk, n_pick, n_kt):
    i = pl.program_id(2)
    t = LANES
    rows = NSA_GROUP * t
    q = jnp.concatenate([q_ref[:, g * HEAD_DIM:(g + 1) * HEAD_DIM] for g in range(NSA_GROUP)], axis=0)
    qrow = lax.broadcasted_iota(jnp.int32, (rows, t), 0) & (t - 1)
    col = lax.broadcasted_iota(jnp.int32, (rows, t), 1)
    qpos = i * t + qrow

    lc = lax.dot_general(q, kc_ref[0].astype(BF16), _NT, preferred_element_type=F32) + tcmp_ref[0, 0]
    valid_c = (col >= 1) & (qpos >= CMP_STRIDE * col + (CMP_LEN - CMP_STRIDE - 1))
    pc = _masked_softmax(lc, valid_c)
    o_cmp = jnp.dot(pc.astype(BF16), vc_ref[0].astype(BF16), preferred_element_type=F32)

    pcs = pc[0:t]
    for g in range(1, NSA_GROUP):
        pcs = pcs + pc[g * t:(g + 1) * t]
    p3 = jnp.concatenate(_split3(pcs), axis=1)
    score = lax.dot_general(ind_ref[...], p3, _NT, preferred_element_type=F32)[:n_blk]
    jj = lax.broadcasted_iota(jnp.int32, (n_blk, t), 0)
    qq = i * t + lax.broadcasted_iota(jnp.int32, (n_blk, t), 1)
    qblk = qq // SEL_BLOCK
    forced = (jj == 0) | (jj == qblk) | (jj == qblk - 1)
    score = jnp.where(forced, BIG, jnp.where(jj * SEL_BLOCK <= qq, score, NEG_INF))
    rank = jnp.zeros((n_blk, t), jnp.int32)
    for jp in range(n_blk):
        r = score[jp:jp + 1, :]
        beats = (r > score) | ((r == score) & (jj > jp))
        rank = rank + beats.astype(jnp.int32)
    sel_t = jnp.concatenate([(rank < n_pick).astype(F32), jnp.zeros((t - n_blk, t), F32)], axis=0)
    maskf = jnp.dot(sel_t.T.astype(BF16), exp_ref[...], preferred_element_type=F32)
    for kt in range(n_kt):
        mask_ref[kt] = maskf[:, kt * t:(kt + 1) * t]

    m0 = jnp.full((rows, 1), NEG_INF, F32)
    l0 = jnp.zeros((rows, 1), F32)
    a0 = jnp.zeros((rows, HEAD_DIM), F32)
    causal = col <= qrow

    def sel_tile(jt, carry, diag):
        s = pl.multiple_of(jt * t, t)
        valid = jnp.concatenate([mask_ref[jt]] * NSA_GROUP, axis=0) > 0.5
        if diag:
            valid = valid & causal
        return _flash_tile(q, ks_ref[pl.ds(s, t), :], vs_ref[pl.ds(s, t), :], tsel_ref[i - jt, 0],
                           valid, *carry)

    carry = sel_tile(i, (m0, l0, a0), True)
    carry = lax.fori_loop(0, i, lambda jt, c: sel_tile(jt, c, False), carry)
    o_sel = carry[2] / carry[1]

    carry = (m0, l0, a0)
    n_back = WINDOW // t
    for w in range(n_back + 1):
        jt = jnp.maximum(i - w, 0)
        s = pl.multiple_of(jt * t, t)
        inside = col < jnp.where(i >= w, t, 0)
        if w == 0:
            valid = causal
        elif w == n_back:
            valid = inside & (col > qrow)
        else:
            valid = inside
        carry = _flash_tile(q, kw_ref[pl.ds(s, t), :], vw_ref[pl.ds(s, t), :], tsel_ref[w, 0],
                            valid, *carry)
    o_win = carry[2] / carry[1]

    gates = gates_ref[0]
    gn = gn_ref[0]
    for g in range(NSA_GROUP):
        sl = slice(g * t, (g + 1) * t)
        o = (gates[:, 3 * g:3 * g + 1] * o_cmp[sl] + gates[:, 3 * g + 1:3 * g + 2] * o_sel[sl]
             + gates[:, 3 * g + 2:3 * g + 3] * o_win[sl])
        o_ref[:, g * HEAD_DIM:(g + 1) * HEAD_DIM] = _rms(o, gn[:, g * HEAD_DIM:(g + 1) * HEAD_DIM]).astype(BF16)


def _sel_constants(n_chunks, n_blk, n_cmp, key_len):
    blk = np.arange(n_blk)
    lo = np.clip((blk * SEL_BLOCK - CMP_LEN) // CMP_STRIDE + 1, 0, n_cmp)
    hi = np.clip(-(-((blk + 1) * SEL_BLOCK) // CMP_STRIDE), 0, n_cmp)
    c = np.arange(n_chunks) - 1
    ind = ((c[None, :] >= lo[:, None]) & (c[None, :] < hi[:, None]) & (c[None, :] >= 0)).astype(np.float32)
    expand = (np.arange(key_len)[None, :] // SEL_BLOCK == blk[:, None]).astype(np.float32)
    return ind, expand


def _nsa_prompt(qns, kcvc, nsakv16, winkv16, tsel, tcmp, gates_k, gain, batch, seq):
    m, nsw = qns.shape
    kv_heads = nsw // (NSA_GROUP * HEAD_DIM)
    t = LANES
    nq = seq // t
    n_chunks = seq // CMP_STRIDE
    assert n_chunks == t, "the compressed branch is tiled as a single 128-column tile"
    n_cmp = (seq - CMP_LEN) // CMP_STRIDE + 1
    n_blk = -(-seq // SEL_BLOCK)
    n_pick = min(N_SEL, n_blk)
    ind, expand = _sel_constants(n_chunks, n_blk, n_cmp, seq)
    ind_pad = np.zeros((t, n_chunks), np.float32)
    ind_pad[:n_blk] = ind
    exp_pad = np.zeros((t, seq), np.float32)
    exp_pad[:n_blk] = expand
    ind3 = jnp.asarray(np.concatenate([ind_pad] * 3, axis=1), BF16)
    gw = NSA_GROUP * HEAD_DIM
    body = functools.partial(_nsa_prompt_body, n_blk=n_blk, n_pick=n_pick, n_kt=nq)
    return pl.pallas_call(
        body,
        grid=(kv_heads, batch, nq),
        in_specs=[pl.BlockSpec((t, gw), lambda k, b, i: (b * nq + i, k)),
                  pl.BlockSpec((1, n_chunks, HEAD_DIM), lambda k, b, i: (b, 0, k)),
                  pl.BlockSpec((1, n_chunks, HEAD_DIM), lambda k, b, i: (b, 0, kv_heads + k)),
                  pl.BlockSpec((seq, HEAD_DIM), lambda k, b, i: (b, 2 * kv_heads + k)),
                  pl.BlockSpec((seq, HEAD_DIM), lambda k, b, i: (b, 3 * kv_heads + k)),
                  pl.BlockSpec((seq, HEAD_DIM), lambda k, b, i: (b, k)),
                  pl.BlockSpec((seq, HEAD_DIM), lambda k, b, i: (b, kv_heads + k)),
                  pl.BlockSpec((nq, 1, gw, t), lambda k, b, i: (0, k, 0, 0)),
                  pl.BlockSpec((1, 1, gw, t), lambda k, b, i: (i, k, 0, 0)),
                  pl.BlockSpec((t, 3 * n_chunks), lambda k, b, i: (0, 0)),
                  pl.BlockSpec((t, seq), lambda k, b, i: (0, 0)),
                  pl.BlockSpec((1, t, LANES), lambda k, b, i: (k, b * nq + i, 0)),
                  pl.BlockSpec((1, 1, gw), lambda k, b, i: (k, 0, 0))],
        out_specs=pl.BlockSpec((t, gw), lambda k, b, i: (b * nq + i, k)),
        out_shape=jax.ShapeDtypeStruct((m, nsw), BF16),
        scratch_shapes=[pltpu.VMEM((nq, t, t), F32)],
        compiler_params=_params("parallel", "parallel", "arbitrary"),
        name="nsa_prompt",
    )(qns, kcvc, kcvc, nsakv16, nsakv16, winkv16, winkv16, tsel, tcmp, ind3,
      jnp.asarray(exp_pad, BF16), gates_k, gain.reshape(kv_heads, 1, gw))


def _cmp_decode_body(tab_ref, q_ref, kcvc_ref, ind_ref, ocmp_ref, score_ref, *, past, kv_heads):
    n_chunks = kcvc_ref.shape[1]
    r = lax.broadcasted_iota(jnp.int32, (1, n_chunks), 1)
    rel = past - (CMP_STRIDE * r + (CMP_LEN - CMP_STRIDE - 1))
    valid = (r >= 1) & (rel >= 0)
    score_ref[...] = jnp.zeros_like(score_ref)
    for k in range(kv_heads):
        kc = kcvc_ref[0, :, k * HEAD_DIM:(k + 1) * HEAD_DIM].astype(BF16)
        vc = kcvc_ref[0, :, (kv_heads + k) * HEAD_DIM:(kv_heads + k + 1) * HEAD_DIM].astype(BF16)
        bias = jnp.concatenate([_bias_chain(rel, tab_ref, k * NSA_GROUP + g) for g in range(NSA_GROUP)], axis=0)
        lg = lax.dot_general(q_ref[0, k], kc, _NT, preferred_element_type=F32) + bias
        p = _masked_softmax(lg, valid)
        ocmp_ref[0, k * NSA_GROUP:(k + 1) * NSA_GROUP, :] = jnp.dot(p.astype(BF16), vc, preferred_element_type=F32)
        p3 = jnp.concatenate(_split3(jnp.sum(p, axis=0, keepdims=True)), axis=1)
        score_ref[0, k:k + 1, :] = jnp.dot(p3, ind_ref[...], preferred_element_type=F32)


def _cmp_decode(table, q, kcvc, ind3, past, kv_heads):
    db = q.shape[0]
    heads = kv_heads * NSA_GROUP
    n_chunks = kcvc.shape[1]
    nj = ind3.shape[1]
    body = functools.partial(_cmp_decode_body, past=past, kv_heads=kv_heads)
    return pl.pallas_call(
        body, grid=(db,),
        in_specs=[pl.BlockSpec(memory_space=pltpu.SMEM),
                  pl.BlockSpec((1, kv_heads, NSA_GROUP, HEAD_DIM), lambda b: (b, 0, 0, 0)),
                  pl.BlockSpec((1, n_chunks, kcvc.shape[2]), lambda b: (b, 0, 0)),
                  pl.BlockSpec(ind3.shape, lambda b: (0, 0))],
        out_specs=[pl.BlockSpec((1, heads, HEAD_DIM), lambda b: (b, 0, 0)),
                   pl.BlockSpec((1, SUBLANES, nj), lambda b: (b, 0, 0))],
        out_shape=[jax.ShapeDtypeStruct((db, heads, HEAD_DIM), F32),
                   jax.ShapeDtypeStruct((db, SUBLANES, nj), F32)],
        compiler_params=_params("parallel"),
        name="nsa_cmp_decode",
    )(table, q, kcvc, ind3)


def _topk_body(s_ref, o_ref, *, n_cache_blk, n_pick):
    s = s_ref[...]
    j = lax.broadcasted_iota(jnp.int32, s.shape, 1)
    lane = lax.broadcasted_iota(jnp.int32, o_ref.shape, 1)
    s = jnp.where((j == 0) | (j == n_cache_blk - 1), BIG, s)
    s = jnp.where(j < n_cache_blk, s, -jnp.inf)
    picks = jnp.zeros(o_ref.shape, jnp.int32)
    for t in range(n_pick):
        m = jnp.max(s, axis=-1, keepdims=True)
        idx = jnp.min(jnp.where(s == m, j, 2 ** 30), axis=-1, keepdims=True)
        picks = jnp.where(lane == t, idx, picks)
        s = jnp.where(j == idx, -jnp.inf, s)
    o_ref[...] = picks


def _topk(scores, n_cache_blk, n_pick):
    rows = scores.shape[0]
    return pl.pallas_call(
        functools.partial(_topk_body, n_cache_blk=n_cache_blk, n_pick=n_pick),
        out_shape=jax.ShapeDtypeStruct((rows, LANES), jnp.int32),
        name="nsa_topk",
    )(scores)


def _sel_decode_body(pt_ref, pk_ref, *refs, n_pick, past, w_buf, kv_static):
    blocks = refs[:n_pick]
    (knew_ref, vnew_ref, win_ref, wknew_ref, wvnew_ref, q_ref, gates_ref, ocmp_ref, gn_ref,
     tab_ref, o_ref, ksc, vsc, kwc, vwc) = refs[n_pick:]
    b = pl.program_id(0)
    k = pl.program_id(1)
    kv_heads = kv_static
    q = q_ref[0, 0]
    n_slot = n_pick + 1
    sel_len = n_slot * SEL_BLOCK
    rows_tok = 4 * kv_static

    first64 = lax.broadcasted_iota(jnp.int32, (SEL_BLOCK, HEAD_DIM), 0) == 0
    for n in range(n_pick):
        ksc[n * SEL_BLOCK:(n + 1) * SEL_BLOCK, :] = (
            blocks[n][pl.ds(2 * kv_static + k, SEL_BLOCK, stride=rows_tok), :].astype(BF16))
        vsc[n * SEL_BLOCK:(n + 1) * SEL_BLOCK, :] = (
            blocks[n][pl.ds(3 * kv_static + k, SEL_BLOCK, stride=rows_tok), :].astype(BF16))
    ksc[n_pick * SEL_BLOCK:sel_len, :] = jnp.where(first64, knew_ref[0], 0.0).astype(BF16)
    vsc[n_pick * SEL_BLOCK:sel_len, :] = jnp.where(first64, vnew_ref[0], 0.0).astype(BF16)

    lane = lax.broadcasted_iota(jnp.int32, (1, sel_len), 1)
    slot = lane // SEL_BLOCK
    base = jnp.full((1, sel_len), past, jnp.int32)
    for n in range(n_pick):
        base = jnp.where(slot == n, pk_ref[(b * kv_heads + k) * n_pick + n] * SEL_BLOCK, base)
    rel = past - (base + (lane & (SEL_BLOCK - 1)))
    bias = jnp.concatenate([_bias_chain(rel, tab_ref, k * NSA_GROUP + g) for g in range(NSA_GROUP)], axis=0)
    lg = lax.dot_general(q, ksc[...], _NT, preferred_element_type=F32) + bias
    p = _masked_softmax(lg, rel >= 0)
    o_sel = jnp.dot(p.astype(BF16), vsc[...], preferred_element_type=F32)

    win_len = kwc.shape[0]
    firstw = lax.broadcasted_iota(jnp.int32, (win_len - w_buf, HEAD_DIM), 0) == 0
    kwc[0:w_buf, :] = win_ref[pl.ds(k, w_buf, stride=2 * kv_static), :].astype(BF16)
    vwc[0:w_buf, :] = win_ref[pl.ds(kv_static + k, w_buf, stride=2 * kv_static), :].astype(BF16)
    kwc[w_buf:win_len, :] = jnp.where(firstw, wknew_ref[0], 0.0).astype(BF16)
    vwc[w_buf:win_len, :] = jnp.where(firstw, wvnew_ref[0], 0.0).astype(BF16)
    relw = w_buf - lax.broadcasted_iota(jnp.int32, (1, win_len), 1)
    bias = jnp.concatenate([_bias_chain(relw, tab_ref, k * NSA_GROUP + g) for g in range(NSA_GROUP)], axis=0)
    lg = lax.dot_general(q, kwc[...], _NT, preferred_element_type=F32) + bias
    p = _masked_softmax(lg, (relw >= 0) & (relw < WINDOW))
    o_win = jnp.dot(p.astype(BF16), vwc[...], preferred_element_type=F32)

    o = gates_ref[0, 0, 0] * ocmp_ref[0, 0] + gates_ref[0, 0, 1] * o_sel + gates_ref[0, 0, 2] * o_win
    o_ref[0, 0] = _rms(o, gn_ref[0])


def _sel_decode(page_table, picks, cache_rows, nsa_new, win_rows, win_new, q, gates, ocmp, gain, table,
                n_pick, past, w_buf):
    db, kv_heads = q.shape[:2]
    n_pages = page_table.shape[1]
    halves = PAGE_SIZE // SEL_BLOCK
    rows_tok = 4 * kv_heads

    def pick_spec(n):
        def idx(b, k, pt, pk):
            j = pk[(b * kv_heads + k) * n_pick + n]
            return (pt[b * n_pages + j // halves] * halves + j % halves, 0)
        return pl.BlockSpec((SEL_BLOCK * rows_tok, HEAD_DIM), idx)

    def row_spec(col0):
        return pl.BlockSpec((1, 1, HEAD_DIM), lambda b, k, pt, pk: (b, 0, col0 + k))

    def head_spec(shape):
        nd = len(shape)
        return pl.BlockSpec((1, 1) + shape, lambda b, k, pt, pk: (b, k) + (0,) * nd)

    win_len = w_buf + LANES
    grid_spec = pltpu.PrefetchScalarGridSpec(
        num_scalar_prefetch=2,
        grid=(db, kv_heads),
        in_specs=[pick_spec(n) for n in range(n_pick)]
        + [row_spec(2 * kv_heads), row_spec(3 * kv_heads),
           pl.BlockSpec((w_buf * 2 * kv_heads, HEAD_DIM), lambda b, k, pt, pk: (b, 0)),
           row_spec(0), row_spec(kv_heads),
           head_spec((NSA_GROUP, HEAD_DIM)), head_spec((3, NSA_GROUP, 1)), head_spec((NSA_GROUP, HEAD_DIM)),
           pl.BlockSpec((1, NSA_GROUP, HEAD_DIM), lambda b, k, pt, pk: (k, 0, 0)),
           pl.BlockSpec(memory_space=pltpu.SMEM)],
        out_specs=head_spec((NSA_GROUP, HEAD_DIM)),
        scratch_shapes=[pltpu.VMEM(((n_pick + 1) * SEL_BLOCK, HEAD_DIM), BF16),
                        pltpu.VMEM(((n_pick + 1) * SEL_BLOCK, HEAD_DIM), BF16),
                        pltpu.VMEM((win_len, HEAD_DIM), BF16),
                        pltpu.VMEM((win_len, HEAD_DIM), BF16)])
    body = functools.partial(_sel_decode_body, n_pick=n_pick, past=past, w_buf=w_buf, kv_static=kv_heads)
    return pl.pallas_call(
        body, grid_spec=grid_spec,
        out_shape=jax.ShapeDtypeStruct((db, kv_heads, NSA_GROUP, HEAD_DIM), F32),
        compiler_params=_params("parallel", "parallel"),
        name="nsa_sel_decode",
    )(page_table.reshape(-1), picks, *([cache_rows] * n_pick), nsa_new, nsa_new, win_rows,
      win_new, win_new, q, gates, ocmp, gain.reshape(kv_heads, NSA_GROUP, HEAD_DIM), table)


def _out_proj_body(x_ref, a_ref, b_ref, w_ref, o_ref):
    half = a_ref.shape[1]
    o_ref[...] = (x_ref[...] + jnp.dot(a_ref[...], w_ref[0:half, :], preferred_element_type=F32)
                  + jnp.dot(b_ref[...], w_ref[half:2 * half, :], preferred_element_type=F32))


def _out_proj(x, a, b, w, tm):
    m, d = x.shape
    half = a.shape[1]
    return pl.pallas_call(
        _out_proj_body, grid=(m // tm,),
        in_specs=[pl.BlockSpec((tm, d), lambda i: (i, 0)),
                  pl.BlockSpec((tm, half), lambda i: (i, 0)),
                  pl.BlockSpec((tm, half), lambda i: (i, 0)),
                  _resident(w.shape)],
        out_specs=pl.BlockSpec((tm, d), lambda i: (i, 0)),
        out_shape=jax.ShapeDtypeStruct((m, d), F32),
        compiler_params=_params("parallel"),
        name="out_proj",
    )(x, a, b, w)


def _norm_matmul_body(x_ref, g_ref, w_ref, o_ref):
    o_ref[...] = jnp.dot(_rms(x_ref[...], g_ref[...]).astype(BF16), w_ref[...], preferred_element_type=F32)


def _norm_matmul(x, g, w, tm):
    m, d = x.shape
    n = w.shape[1]
    return pl.pallas_call(
        _norm_matmul_body, grid=(m // tm,),
        in_specs=[pl.BlockSpec((tm, d), lambda i: (i, 0)),
                  pl.BlockSpec((1, d), lambda i: (0, 0)),
                  _resident(w.shape)],
        out_specs=pl.BlockSpec((tm, n), lambda i: (i, 0)),
        out_shape=jax.ShapeDtypeStruct((m, n), F32),
        compiler_params=_params("parallel"),
        name="mem_kv_proj",
    )(x, g.reshape(1, d), w)


def _cross_body(x_ref, g_ref, wq_ref, wo_ref, mem_ref, o_ref, *, scale, n_mem, row_layout):
    x = x_ref[0]
    hb = _rms(x, g_ref[...]).astype(BF16)
    qh = (jnp.dot(hb, wq_ref[...], preferred_element_type=F32) * scale).astype(BF16)
    width = MEM_HEADS * HEAD_DIM
    outs = []
    for h in range(MEM_HEADS):
        if row_layout:
            k = mem_ref[pl.ds(h, n_mem, stride=2 * MEM_HEADS), :].astype(BF16)
            v = mem_ref[pl.ds(MEM_HEADS + h, n_mem, stride=2 * MEM_HEADS), :].astype(BF16)
        else:
            k = mem_ref[0, :, h * HEAD_DIM:(h + 1) * HEAD_DIM].astype(BF16)
            v = mem_ref[0, :, width + h * HEAD_DIM:width + (h + 1) * HEAD_DIM].astype(BF16)
        lg = lax.dot_general(qh[:, h * HEAD_DIM:(h + 1) * HEAD_DIM], k, _NT, preferred_element_type=F32)
        e = jnp.exp(lg - jnp.max(lg, axis=-1, keepdims=True))
        p = e / jnp.sum(e, axis=-1, keepdims=True)
        outs.append(jnp.dot(p.astype(BF16), v, preferred_element_type=F32).astype(BF16))
    o = jnp.concatenate(outs, axis=1)
    o_ref[0] = x + jnp.dot(o, wo_ref[...], preferred_element_type=F32)


def _cross(x3, g, wq, wo, mem, n_mem, tm):
    nb, t, d = x3.shape
    row_layout = mem.ndim == 2
    if row_layout:
        mem_spec = pl.BlockSpec((n_mem * 2 * MEM_HEADS, HEAD_DIM), lambda b, i: (b, 0))
    else:
        mem_spec = pl.BlockSpec((1, n_mem, mem.shape[2]), lambda b, i: (b, 0, 0))
    body = functools.partial(_cross_body, scale=1.0 / math.sqrt(HEAD_DIM), n_mem=n_mem, row_layout=row_layout)
    return pl.pallas_call(
        body, grid=(nb, t // tm),
        in_specs=[pl.BlockSpec((1, tm, d), lambda b, i: (b, i, 0)),
                  pl.BlockSpec((1, d), lambda b, i: (0, 0)),
                  _resident(wq.shape), _resident(wo.shape),
                  mem_spec],
        out_specs=pl.BlockSpec((1, tm, d), lambda b, i: (b, i, 0)),
        out_shape=jax.ShapeDtypeStruct((nb, t, d), F32),
        compiler_params=_params("parallel", "parallel"),
        name="cross_attn",
    )(x3, g.reshape(1, d), wq, wo, mem)


def _mlp_body(x_ref, g_ref, wu_ref, wd_ref, gf_ref, o_ref, h_ref, acc_ref):
    f = pl.program_id(1)

    @pl.when(f == 0)
    def _():
        h_ref[...] = _rms(x_ref[...], g_ref[...]).astype(BF16)
        acc_ref[...] = jnp.zeros_like(acc_ref)

    u = jnp.maximum(jnp.dot(h_ref[...], wu_ref[...], preferred_element_type=F32), 0.0)
    acc_ref[...] += jnp.dot((u * u).astype(BF16), wd_ref[...], preferred_element_type=F32)

    @pl.when(f == pl.num_programs(1) - 1)
    def _():
        o_ref[...] = _rms(x_ref[...] + acc_ref[...], gf_ref[...])


def _mlp_final(x, g, wu, wd, gf, tm, tf):
    m, d = x.shape
    dff = wu.shape[1]
    return pl.pallas_call(
        _mlp_body, grid=(m // tm, dff // tf),
        in_specs=[pl.BlockSpec((tm, d), lambda i, f: (i, 0)),
                  pl.BlockSpec((1, d), lambda i, f: (0, 0)),
                  pl.BlockSpec((d, tf), lambda i, f: (0, f)),
                  pl.BlockSpec((tf, d), lambda i, f: (f, 0)),
                  pl.BlockSpec((1, d), lambda i, f: (0, 0))],
        out_specs=pl.BlockSpec((tm, d), lambda i, f: (i, 0)),
        out_shape=jax.ShapeDtypeStruct((m, d), F32),
        scratch_shapes=[pltpu.VMEM((tm, d), BF16), pltpu.VMEM((tm, d), F32)],
        compiler_params=_params("parallel", "arbitrary"),
        name="mlp_final",
    )(x, g.reshape(1, d), wu, wd, gf.reshape(1, d))


def _w1cat(w1):
    half = (CMP_LEN // 2) * HEAD_DIM
    hid = w1.shape[1]
    a = w1[:half].reshape(CMP_STRIDE // 2, 2 * HEAD_DIM, hid)
    b = w1[half:].reshape(CMP_STRIDE // 2, 2 * HEAD_DIM, hid)
    return jnp.concatenate([a, b], axis=-1).astype(BF16)


def kernel(x_prompt, x_sample, cache_sb_kv, cache_nsa_kv, cache_win_kv, cache_mem_kv, page_table, mem_prompt,
           norm_mix, w_in, norm_sb_out, norm_nsa_out, w_out, rel_bias_table,
           cmp_pe_k, cmp_pe_v, w_cmp_k1, w_cmp_k2, w_cmp_v1, w_cmp_v2,
           norm_cross, norm_mem, w_cross_q, w_mem_kv, w_cross_o,
           norm_ffn, w_up, w_down, norm_final):
    batch, seq, d = x_prompt.shape
    db, dec_seq, _ = x_sample.shape
    assert dec_seq == 1
    depth = w_in.shape[0]
    assert depth == 1, "the final norm is fused into the MLP of the only layer"
    n_pool = cache_sb_kv.shape[1]
    n_pages = page_table.shape[1]
    past = n_pages * PAGE_SIZE
    sb_heads = cache_sb_kv.shape[4]
    kv_heads = cache_nsa_kv.shape[4]
    nsa_heads = kv_heads * NSA_GROUP
    sbw = sb_heads * HEAD_DIM
    nsw = nsa_heads * HEAD_DIM
    kvw = kv_heads * HEAD_DIM
    n_gates = 3 * nsa_heads
    w_buf = cache_win_kv.shape[2]
    win_keep = min(WINDOW, seq)
    n_mem = mem_prompt.shape[1]
    mem_w = MEM_HEADS * HEAD_DIM
    m = batch * seq
    l = 0

    w_in_p = jnp.pad(w_in[l].astype(BF16), ((0, 0), (0, LANES - n_gates)))
    w1cat = jnp.stack([_w1cat(w_cmp_k1[l]), _w1cat(w_cmp_v1[l])])
    w2 = jnp.stack([w_cmp_k2[l], w_cmp_v2[l]]).astype(BF16)
    pe = jnp.stack([cmp_pe_k[l].reshape(1, -1), cmp_pe_v[l].reshape(1, -1)])
    pe_term = _pe_term(jnp.broadcast_to(pe, (2, SUBLANES, pe.shape[-1])), jnp.stack([w_cmp_k1[l], w_cmp_v1[l]]))
    w_out_b = w_out[l].astype(BF16)

    xp = x_prompt.reshape(m, d)
    (qsb, sbkv, sbkv16, qns, nsakv, nsakv16, winkv, winkv16, gates) = _project(xp, norm_mix[l], w_in_p, 256)
    o_sb = _sb_prompt(qsb, sbkv16, norm_sb_out[l], batch, seq)

    pages_seq = seq // PAGE_SIZE
    prompt_pages = jnp.arange(batch * pages_seq, dtype=jnp.int32).reshape(batch, pages_seq)
    kcvc = _compress(nsakv.reshape(batch * pages_seq, PAGE_SIZE, 4 * kvw), prompt_pages, w1cat, w2, pe_term,
                     kv_heads, pages_seq)
    nq = seq // LANES
    tsel = _bias_tiles(rel_bias_table, nq, LANES, 1, 0)
    tcmp = _bias_tiles(rel_bias_table, nq, LANES, CMP_STRIDE, CMP_LEN - CMP_STRIDE - 1)
    gates_k = jnp.pad(gates[:, :n_gates].reshape(m, kv_heads, 3 * NSA_GROUP).transpose(1, 0, 2),
                      ((0, 0), (0, 0), (0, LANES - 3 * NSA_GROUP)))
    o_ns = _nsa_prompt(qns, kcvc, nsakv16, winkv16, tsel, tcmp, gates_k, norm_nsa_out[l], batch, seq)
    xp = _out_proj(xp, o_sb, o_ns, w_out_b, 512)

    sb_kv_prompt = sbkv.reshape(1, batch, seq, 2, sb_heads, HEAD_DIM)
    nsa_kv_prompt = nsakv.reshape(1, batch, seq, 4, kv_heads, HEAD_DIM)
    win_kv_prompt = winkv.reshape(1, batch, seq, 2, kv_heads, HEAD_DIM)[:, :, seq - win_keep:]

    xs = x_sample.reshape(db, d)
    (qsb_s, sbkv_s, _, qns_s, nsakv_s, _, winkv_s, _, gates_s) = _project(xs, norm_mix[l], w_in_p, db)
    o_sb_s = _sb_decode(qsb_s.reshape(db, sb_heads, HEAD_DIM), cache_sb_kv[l].reshape(-1, HEAD_DIM), page_table,
                        norm_sb_out[l], 8)

    nsa_rows = cache_nsa_kv[l].reshape(-1, HEAD_DIM)
    kcvc_s = _compress(nsa_rows, page_table, w1cat, w2, pe_term, kv_heads, 16)
    n_chunks_s = past // CMP_STRIDE
    tk = past + 1
    n_cmp_s = (tk - CMP_LEN) // CMP_STRIDE + 1
    n_blk_s = -(-tk // SEL_BLOCK)
    n_cache_blk = past // SEL_BLOCK
    assert n_blk_s == n_cache_blk + 1 and n_cache_blk >= N_SEL
    nj = -(-n_blk_s // LANES) * LANES
    ind_s, _ = _sel_constants(n_chunks_s, n_blk_s, n_cmp_s, 0)
    ind_s_pad = np.zeros((nj, n_chunks_s), np.float32)
    ind_s_pad[:n_blk_s] = ind_s
    ind3_s = jnp.asarray(np.concatenate([ind_s_pad.T] * 3, axis=0), BF16)
    q_dec = qns_s.reshape(db, kv_heads, NSA_GROUP, HEAD_DIM)
    ocmp_s, scores = _cmp_decode(rel_bias_table, q_dec, kcvc_s, ind3_s, past, kv_heads)
    n_pick_cache = N_SEL - 1
    picks = _topk(scores[:, :kv_heads].reshape(db * kv_heads, nj), n_cache_blk, n_pick_cache)
    picks = picks[:, :n_pick_cache].reshape(-1)
    gates_d = gates_s[:, :n_gates].reshape(db, kv_heads, NSA_GROUP, 3).transpose(0, 1, 3, 2)[..., None]
    o_ns_s = _sel_decode(page_table, picks, nsa_rows, nsakv_s.reshape(db, 1, 4 * kvw),
                         cache_win_kv[l].reshape(-1, HEAD_DIM), winkv_s.reshape(db, 1, 2 * kvw), q_dec, gates_d,
                         ocmp_s.reshape(db, kv_heads, NSA_GROUP, HEAD_DIM), norm_nsa_out[l],
                         rel_bias_table, n_pick_cache, past, w_buf)
    xs = _out_proj(xs, o_sb_s.reshape(db, sbw).astype(BF16), o_ns_s.reshape(db, nsw).astype(BF16), w_out_b, db)

    sb_kv_sample = sbkv_s.reshape(1, db, 1, 2, sb_heads, HEAD_DIM)
    nsa_kv_sample = nsakv_s.reshape(1, db, 1, 4, kv_heads, HEAD_DIM)
    win_new = winkv_s.reshape(db, 1, 2, kv_heads, HEAD_DIM)
    win_kv_sample = jnp.concatenate([cache_win_kv[l], win_new], axis=1)[None, :, 1:]

    mem_kv = _norm_matmul(mem_prompt.reshape(batch * n_mem, d), norm_mem[l], w_mem_kv[l].astype(BF16), n_mem)
    wq = w_cross_q[l].astype(BF16)
    wo = w_cross_o[l].astype(BF16)
    xp = _cross(xp.reshape(batch, seq, d), norm_cross[l], wq, wo, mem_kv.reshape(batch, n_mem, 2 * mem_w),
                n_mem, 512)
    xs = _cross(xs.reshape(db, 1, d), norm_cross[l], wq, wo, cache_mem_kv[l].reshape(-1, HEAD_DIM), n_mem, 1)
    mem_kv_prompt = mem_kv.reshape(1, batch, n_mem, 2, MEM_HEADS, HEAD_DIM)

    wu = w_up[l].astype(BF16)
    wd = w_down[l].astype(BF16)
    y_prompt = _mlp_final(xp.reshape(m, d), norm_ffn[l], wu, wd, norm_final, 512, 512).reshape(batch, seq, d)
    y_sample = _mlp_final(xs.reshape(db, d), norm_ffn[l], wu, wd, norm_final, db, 512).reshape(db, 1, d)

    return (y_prompt, y_sample, sb_kv_prompt, sb_kv_sample, nsa_kv_prompt, nsa_kv_sample,
            win_kv_prompt, win_kv_sample, mem_kv_prompt)
```

```python
import functools
import math

import numpy as np
import jax
import jax.numpy as jnp
from jax import lax
from jax.experimental import pallas as pl
from jax.experimental.pallas import tpu as pltpu

HEAD_DIM = 128
NSA_GROUP = 4
MEM_HEADS = 4
PAGE_SIZE = 128
CMP_LEN = 32
CMP_STRIDE = 16
SEL_BLOCK = 64
N_SEL = 16
WINDOW = 512
N_BUCKETS = 32
MAX_DISTANCE = 1024
RMS_EPS = 1e-6
NEG_INF = -1e30
BIG = 1e30
LANES = 128
SUBLANES = 8
VMEM_LIMIT = 56 * 1024 * 1024

F32 = jnp.float32
BF16 = jnp.bfloat16
_NT = (((1,), (1,)), ((), ()))


def _bucket_thresholds():
    max_exact = N_BUCKETS // 2
    n_log = N_BUCKETS - max_exact
    ratio = MAX_DISTANCE // max_exact
    th = list(range(max_exact)) + [max_exact]
    for m in range(1, n_log):
        n = th[-1]
        while n ** n_log < (max_exact ** n_log) * (ratio ** m):
            n += 1
        th.append(n)
    return tuple(th)


THRESHOLDS = _bucket_thresholds()


def _bias_chain(rel, tab_ref, h):
    v = jnp.full(rel.shape, tab_ref[0, h], F32)
    for b in range(1, N_BUCKETS):
        v = jnp.where(rel >= THRESHOLDS[b], tab_ref[b, h], v)
    return v


def _rms(x, g):
    return x * lax.rsqrt(jnp.mean(x * x, axis=-1, keepdims=True) + RMS_EPS) * g


def _softplus(z):
    return jnp.maximum(z, 0.0) + jnp.log(1.0 + jnp.exp(-jnp.abs(z)))


def _split2(x):
    hi = x.astype(BF16)
    lo = (x - hi.astype(F32)).astype(BF16)
    return hi, lo


def _split3(x):
    hi = x.astype(BF16)
    r = x - hi.astype(F32)
    mid = r.astype(BF16)
    lo = (r - mid.astype(F32)).astype(BF16)
    return hi, mid, lo


def _masked_softmax(lg, valid):
    lgm = jnp.where(valid, lg, NEG_INF)
    m = jnp.max(lgm, axis=-1, keepdims=True)
    e = jnp.where(valid, jnp.exp(lgm - m), 0.0)
    s = jnp.sum(e, axis=-1, keepdims=True)
    return e / jnp.where(s > 0.0, s, 1.0)


def _params(*sem):
    return pltpu.CompilerParams(dimension_semantics=sem, vmem_limit_bytes=VMEM_LIMIT)


def _resident(shape):
    nd = len(shape)
    return pl.BlockSpec(shape, lambda *_: (0,) * nd, pipeline_mode=pl.Buffered(1))


def _proj_body(x_ref, g_ref, w_ref, qsb_ref, sbkv_ref, sbkv16_ref, qns_ref, nsakv_ref,
               nsakv16_ref, winkv_ref, winkv16_ref, gates_ref, *, sbw, nsw, kvw, scale):
    hb = _rms(x_ref[...], g_ref[...]).astype(BF16)

    def mm(c0, n):
        return jnp.dot(hb, w_ref[:, c0:c0 + n], preferred_element_type=F32)

    c = 0
    qsb_ref[...] = (mm(c, sbw) * scale).astype(BF16)
    c += sbw
    for half in range(2):
        kv = mm(c, sbw)
        sbkv_ref[:, half * sbw:(half + 1) * sbw] = kv
        sbkv16_ref[:, half * sbw:(half + 1) * sbw] = kv.astype(BF16)
        c += sbw
    qns_ref[...] = (mm(c, nsw) * scale).astype(BF16)
    c += nsw
    kv = mm(c, 4 * kvw)
    nsakv_ref[...] = kv
    nsakv16_ref[...] = kv.astype(BF16)
    c += 4 * kvw
    kv = mm(c, 2 * kvw)
    winkv_ref[...] = kv
    winkv16_ref[...] = kv.astype(BF16)
    c += 2 * kvw
    gates_ref[...] = jax.nn.sigmoid(mm(c, LANES))


def _project(x, g, w_pad, tm):
    m, d = x.shape
    sbw = nsw = d // 2
    kvw = nsw // NSA_GROUP
    widths = (sbw, 2 * sbw, 2 * sbw, nsw, 4 * kvw, 4 * kvw, 2 * kvw, 2 * kvw, LANES)
    dtypes = (BF16, F32, BF16, BF16, F32, BF16, F32, BF16, F32)
    body = functools.partial(_proj_body, sbw=sbw, nsw=nsw, kvw=kvw, scale=1.0 / math.sqrt(HEAD_DIM))
    return pl.pallas_call(
        body,
        grid=(m // tm,),
        in_specs=[pl.BlockSpec((tm, d), lambda i: (i, 0)),
                  pl.BlockSpec((1, d), lambda i: (0, 0)),
                  _resident(w_pad.shape)],
        out_specs=[pl.BlockSpec((tm, w), lambda i: (i, 0)) for w in widths],
        out_shape=[jax.ShapeDtypeStruct((m, w), dt) for w, dt in zip(widths, dtypes)],
        compiler_params=_params("parallel"),
        name="in_proj",
    )(x, g.reshape(1, d), w_pad)


def _bias_tile_body(tab_ref, o_ref, *, q_stride, k_stride, k_off, window, n_heads):
    m = pl.program_id(0)
    i = lax.broadcasted_iota(jnp.int32, (LANES, LANES), 0)
    j = lax.broadcasted_iota(jnp.int32, (LANES, LANES), 1)
    rel = m * q_stride + i - (k_stride * j + k_off)
    for h in range(n_heads):
        v = _bias_chain(rel, tab_ref, h)
        if window:
            v = jnp.where((rel >= 0) & (rel < WINDOW), v, NEG_INF)
        o_ref[0, h] = v


def _bias_tiles(table, n_tiles, q_stride, k_stride, k_off, window=False):
    n_heads = table.shape[1]
    body = functools.partial(_bias_tile_body, q_stride=q_stride, k_stride=k_stride, k_off=k_off, window=window,
                             n_heads=n_heads)
    out = pl.pallas_call(
        body,
        grid=(n_tiles,),
        in_specs=[pl.BlockSpec(memory_space=pltpu.SMEM)],
        out_specs=pl.BlockSpec((1, n_heads, LANES, LANES), lambda m: (m, 0, 0, 0)),
        out_shape=jax.ShapeDtypeStruct((n_tiles, n_heads, LANES, LANES), F32),
        compiler_params=_params("parallel"),
        name="bias_tiles",
    )(table)
    return out.reshape(n_tiles, n_heads // NSA_GROUP, NSA_GROUP * LANES, LANES)


def _cumsum_rhs(t):
    u = np.tril(np.ones((t, t), np.float32))
    half = np.concatenate([u, np.ones((t, t), np.float32)], axis=1)
    return jnp.asarray(np.concatenate([half, half], axis=0), BF16)


def _sb_tile(q, k, v, uo, carry, acc, valid):
    t = q.shape[0]
    sub = uo.shape[1] // 2
    z = lax.dot_general(q, k, _NT, preferred_element_type=F32)
    sp = _softplus(z)
    if valid is not None:
        sp = jnp.where(valid, sp, 0.0)
    parts = []
    for s in reversed(range(t // sub)):
        hi, lo = _split2(sp[:, s * sub:(s + 1) * sub])
        ct = jnp.dot(jnp.concatenate([hi, lo], axis=1), uo, preferred_element_type=F32)
        parts.insert(0, jnp.exp(z[:, s * sub:(s + 1) * sub] - (ct[:, :sub] + carry)))
        carry = carry + ct[:, sub:]
    a = jnp.concatenate(parts, axis=1)
    if valid is not None:
        a = jnp.where(valid, a, 0.0)
    acc = acc + jnp.dot(a.astype(BF16), v, preferred_element_type=F32)
    return carry, acc


def _sb_prompt_body(q_ref, k_ref, v_ref, g_ref, uo_ref, o_ref, *, tq, hps):
    qi = pl.program_id(2)
    uo = uo_ref[...]
    row = lax.broadcasted_iota(jnp.int32, (tq, tq), 0)
    col = lax.broadcasted_iota(jnp.int32, (tq, tq), 1)
    lanes = [slice(h * HEAD_DIM, (h + 1) * HEAD_DIM) for h in range(hps)]
    qs = [q_ref[:, sl] for sl in lanes]

    def tiles(start, state, valid):
        out = []
        for h, sl in enumerate(lanes):
            out.extend(_sb_tile(qs[h], k_ref[pl.ds(start, tq), sl], v_ref[pl.ds(start, tq), sl], uo,
                                state[2 * h], state[2 * h + 1], valid))
        return tuple(out)

    state = (jnp.zeros((tq, uo.shape[1] // 2), F32), jnp.zeros((tq, HEAD_DIM), F32)) * hps
    state = tiles(pl.multiple_of(qi * tq, tq), state, col < row)
    state = lax.fori_loop(0, qi, lambda t, st: tiles(pl.multiple_of((qi - 1 - t) * tq, tq), st, None), state)
    for h, sl in enumerate(lanes):
        o_ref[:, sl] = _rms(state[2 * h + 1], g_ref[0, :, sl]).astype(BF16)


def _sb_prompt(qsb, sbkv16, gain, batch, seq):
    m, sbw = qsb.shape
    heads = sbw // HEAD_DIM
    tq = 2 * LANES
    hps = 4
    nq = seq // tq
    wide = hps * HEAD_DIM
    groups = heads // hps
    return pl.pallas_call(
        functools.partial(_sb_prompt_body, tq=tq, hps=hps),
        grid=(batch, groups, nq),
        in_specs=[pl.BlockSpec((tq, wide), lambda b, h, i: (b * nq + i, h)),
                  pl.BlockSpec((seq, wide), lambda b, h, i: (b, h)),
                  pl.BlockSpec((seq, wide), lambda b, h, i: (b, groups + h)),
                  pl.BlockSpec((1, 1, wide), lambda b, h, i: (h, 0, 0)),
                  pl.BlockSpec((2 * LANES, 2 * LANES), lambda b, h, i: (0, 0))],
        out_specs=pl.BlockSpec((tq, wide), lambda b, h, i: (b * nq + i, h)),
        out_shape=jax.ShapeDtypeStruct((m, sbw), BF16),
        compiler_params=_params("parallel", "parallel", "arbitrary"),
        name="sb_prompt",
    )(qsb, sbkv16, sbkv16, gain.reshape(groups, 1, wide), _cumsum_rhs(LANES))


def _sb_decode_body(pt_ref, *refs, n_pages_step, n_steps, heads):
    pages = refs[:n_pages_step]
    q_ref, uo_ref, g_ref, o_ref, acc_ref, carry_ref = refs[n_pages_step:]
    s = pl.program_id(1)
    rows_tok = 2 * heads

    @pl.when(s == 0)
    def _():
        acc_ref[...] = jnp.zeros_like(acc_ref)
        carry_ref[...] = jnp.zeros_like(carry_ref)

    q = q_ref[0]
    uo = uo_ref[...]
    rowi = lax.broadcasted_iota(jnp.int32, (heads, HEAD_DIM), 0)
    zs = []
    for p in range(n_pages_step):
        z = jnp.zeros((heads, PAGE_SIZE), F32)
        for h in range(heads):
            k = pages[p][pl.ds(h, PAGE_SIZE, stride=rows_tok), :].astype(BF16)
            z = jnp.where(rowi == h, lax.dot_general(q, k, _NT, preferred_element_type=F32), z)
        zs.append(z)
    cts = []
    for z in zs:
        hi, lo = _split2(_softplus(z))
        cts.append(jnp.dot(jnp.concatenate([hi, lo], axis=1), uo, preferred_element_type=F32))
    carry = carry_ref[...]
    acc = acc_ref[...]
    for p in range(n_pages_step):
        a = jnp.exp(zs[p] - (cts[p][:, :PAGE_SIZE] + carry)).astype(BF16)
        carry = carry + cts[p][:, PAGE_SIZE:]
        for h in range(heads):
            v = pages[p][pl.ds(heads + h, PAGE_SIZE, stride=rows_tok), :].astype(BF16)
            acc = acc + jnp.where(rowi == h, jnp.dot(a, v, preferred_element_type=F32), 0.0)
    acc_ref[...] = acc
    carry_ref[...] = carry

    @pl.when(s == n_steps - 1)
    def _():
        o_ref[0] = _rms(acc_ref[...], g_ref[...])


def _sb_decode(q, cache_rows, page_table, gain, pages_per_step):
    db, heads, _ = q.shape
    n_pages = page_table.shape[1]
    n_steps = n_pages // pages_per_step
    page_rows = PAGE_SIZE * 2 * heads

    def page_spec(p):
        def idx(b, s, pt):
            return (pt[b * n_pages + n_pages - 1 - (s * pages_per_step + p)], 0)
        return pl.BlockSpec((page_rows, HEAD_DIM), idx)

    grid_spec = pltpu.PrefetchScalarGridSpec(
        num_scalar_prefetch=1,
        grid=(db, n_steps),
        in_specs=[page_spec(p) for p in range(pages_per_step)] + [
            pl.BlockSpec((1, heads, HEAD_DIM), lambda b, s, pt: (b, 0, 0)),
            pl.BlockSpec((2 * PAGE_SIZE, 2 * PAGE_SIZE), lambda b, s, pt: (0, 0)),
            pl.BlockSpec((heads, HEAD_DIM), lambda b, s, pt: (0, 0))],
        out_specs=pl.BlockSpec((1, heads, HEAD_DIM), lambda b, s, pt: (b, 0, 0)),
        scratch_shapes=[pltpu.VMEM((heads, HEAD_DIM), F32), pltpu.VMEM((heads, PAGE_SIZE), F32)])
    body = functools.partial(_sb_decode_body, n_pages_step=pages_per_step, n_steps=n_steps, heads=heads)
    return pl.pallas_call(
        body, grid_spec=grid_spec,
        out_shape=jax.ShapeDtypeStruct((db, heads, HEAD_DIM), F32),
        compiler_params=_params("parallel", "arbitrary"),
        name="sb_decode",
    )(page_table.reshape(-1), *([cache_rows] * pages_per_step), q, _cumsum_rhs(PAGE_SIZE),
      gain.reshape(heads, HEAD_DIM))


def _pe_term_body(pe_ref, w_ref, o_ref):
    o_ref[0] = jnp.dot(pe_ref[0].astype(BF16), w_ref[0].astype(BF16), preferred_element_type=F32)


def _pe_term(pe, w1):
    n, _, kdim = pe.shape
    hid = w1.shape[-1]
    return pl.pallas_call(
        _pe_term_body, grid=(n,),
        in_specs=[pl.BlockSpec((1, SUBLANES, kdim), lambda t: (t, 0, 0)),
                  pl.BlockSpec((1, kdim, hid), lambda t: (t, 0, 0))],
        out_specs=pl.BlockSpec((1, SUBLANES, hid), lambda t: (t, 0, 0)),
        out_shape=jax.ShapeDtypeStruct((n, SUBLANES, hid), F32),
        compiler_params=_params("parallel"),
        name="cmp_pe_term",
    )(pe, w1)


def _compress_body(pt_ref, *refs, n_pages_step, kv_heads, row_layout):
    pages = refs[:n_pages_step]
    w1_ref, w2_ref, pe_ref, o_ref, carry_ref, tok_ref = refs[n_pages_step:]
    chunks_page = PAGE_SIZE // CMP_STRIDE
    rows = n_pages_step * chunks_page
    step_tokens = n_pages_step * PAGE_SIZE
    rows_tok = 4 * kv_heads

    for idx in range(2 * kv_heads):
        for p in range(n_pages_step):
            if row_layout:
                plane = pages[p][pl.ds(idx, PAGE_SIZE, stride=rows_tok), :]
            else:
                plane = pages[p][0, :, idx * HEAD_DIM:(idx + 1) * HEAD_DIM]
            tok_ref[pl.ds(idx * step_tokens + p * PAGE_SIZE, PAGE_SIZE), :] = plane

    @pl.when(pl.program_id(1) == 0)
    def _():
        carry_ref[...] = jnp.zeros_like(carry_ref)

    rowi = lax.broadcasted_iota(jnp.int32, (rows, HEAD_DIM), 0)
    for t in range(2):
        for k in range(kv_heads):
            idx = t * kv_heads + k
            c0 = idx * HEAD_DIM
            acc = None
            for lp in range(CMP_STRIDE // 2):
                halves = [tok_ref[pl.ds(idx * step_tokens + 2 * lp + dl, rows, stride=CMP_STRIDE), :]
                          for dl in range(2)]
                x = jnp.concatenate(halves, axis=1).astype(BF16)
                d = jnp.dot(x, w1_ref[t, lp], preferred_element_type=F32)
                acc = d if acc is None else acc + d
            first = acc[:, :HEAD_DIM]
            second = acc[:, HEAD_DIM:]
            prev_first = jnp.where(rowi == 0, carry_ref[idx], pltpu.roll(first, 1, axis=0))
            carry_ref[idx] = first[rows - 1:rows, :]
            hid = jax.nn.gelu(prev_first + second + pe_ref[t, 0:1, :])
            o_ref[0, :, c0:c0 + HEAD_DIM] = jnp.dot(hid.astype(BF16), w2_ref[t], preferred_element_type=F32)


def _compress(cache, page_table, w1cat, w2, pe_term, kv_heads, pages_per_step):
    nb, n_pages = page_table.shape
    n_steps = n_pages // pages_per_step
    chunks = n_pages * (PAGE_SIZE // CMP_STRIDE)
    rows_step = pages_per_step * (PAGE_SIZE // CMP_STRIDE)
    width = 2 * kv_heads * HEAD_DIM
    row_layout = cache.ndim == 2

    def page_spec(p):
        if row_layout:
            return pl.BlockSpec((PAGE_SIZE * 4 * kv_heads, HEAD_DIM),
                                lambda b, s, pt: (pt[b * n_pages + s * pages_per_step + p], 0))
        return pl.BlockSpec((1, PAGE_SIZE, width),
                            lambda b, s, pt: (pt[b * n_pages + s * pages_per_step + p], 0, 0))

    grid_spec = pltpu.PrefetchScalarGridSpec(
        num_scalar_prefetch=1,
        grid=(nb, n_steps),
        in_specs=[page_spec(p) for p in range(pages_per_step)] + [
            pl.BlockSpec(w1cat.shape, lambda b, s, pt: (0, 0, 0, 0)),
            pl.BlockSpec(w2.shape, lambda b, s, pt: (0, 0, 0)),
            pl.BlockSpec(pe_term.shape, lambda b, s, pt: (0, 0, 0))],
        out_specs=pl.BlockSpec((1, rows_step, width), lambda b, s, pt: (b, s, 0)),
        scratch_shapes=[pltpu.VMEM((2 * kv_heads, 1, HEAD_DIM), F32),
                        pltpu.VMEM((2 * kv_heads * pages_per_step * PAGE_SIZE, HEAD_DIM), F32)])
    body = functools.partial(_compress_body, n_pages_step=pages_per_step, kv_heads=kv_heads,
                             row_layout=row_layout)
    return pl.pallas_call(
        body, grid_spec=grid_spec,
        out_shape=jax.ShapeDtypeStruct((nb, chunks, width), F32),
        compiler_params=_params("parallel", "arbitrary"),
        name="nsa_compress",
    )(page_table.reshape(-1), *([cache] * pages_per_step), w1cat, w2, pe_term)


def _nsa_prompt_body(q_ref, kc_ref, vc_ref, ks_ref, vs_ref, kw_ref, vw_ref, tsel_ref, twin_ref, tcmp_ref,
                     ind_ref, exp_ref, gates_ref, gn_ref, o_ref, mask_ref, *, n_blk, n_pick, n_kt):
    i = pl.program_id(2)
    t = LANES
    rows = NSA_GROUP * t
    q = jnp.concatenate([q_ref[:, g * HEAD_DIM:(g + 1) * HEAD_DIM] for g in range(NSA_GROUP)], axis=0)
    qrow = lax.broadcasted_iota(jnp.int32, (rows, t), 0) & (t - 1)
    col = lax.broadcasted_iota(jnp.int32, (rows, t), 1)
    qpos = i * t + qrow

    lc = lax.dot_general(q, kc_ref[0].astype(BF16), _NT, preferred_element_type=F32) + tcmp_ref[0, 0]
    valid_c = (col >= 1) & (qpos >= CMP_STRIDE * col + (CMP_LEN - CMP_STRIDE - 1))
    pc = _masked_softmax(lc, valid_c)
    o_cmp = jnp.dot(pc.astype(BF16), vc_ref[0].astype(BF16), preferred_element_type=F32)

    pcs = pc[0:t]
    for g in range(1, NSA_GROUP):
        pcs = pcs + pc[g * t:(g + 1) * t]
    p3 = jnp.concatenate(_split3(pcs), axis=1)
    score = lax.dot_general(ind_ref[...], p3, _NT, preferred_element_type=F32)[:n_blk]
    jj = lax.broadcasted_iota(jnp.int32, (n_blk, t), 0)
    qq = i * t + lax.broadcasted_iota(jnp.int32, (n_blk, t), 1)
    qblk = qq // SEL_BLOCK
    forced = (jj == 0) | (jj == qblk) | (jj == qblk - 1)
    score = jnp.where(forced, BIG, jnp.where(jj * SEL_BLOCK <= qq, score, NEG_INF))
    rank = jnp.zeros((n_blk, t), jnp.int32)
    for jp in range(n_blk):
        r = score[jp:jp + 1, :]
        beats = (r > score) | ((r == score) & (jj > jp))
        rank = rank + beats.astype(jnp.int32)
    sel_t = jnp.concatenate([(rank < n_pick).astype(F32), jnp.zeros((t - n_blk, t), F32)], axis=0)
    maskf = jnp.dot(sel_t.T.astype(BF16), exp_ref[...], preferred_element_type=F32)
    for kt in range(n_kt):
        mask_ref[kt] = (maskf[:, kt * t:(kt + 1) * t] - 1.0) * BIG


    two = 2 * t
    qrow2 = lax.broadcasted_iota(jnp.int32, (rows, two), 0) & (t - 1)
    col2 = lax.broadcasted_iota(jnp.int32, (rows, two), 1)

    def sel_logits(jg, diag):
        s = pl.multiple_of(jg * two, two)
        d0 = i - 2 * jg
        bias = jnp.concatenate([tsel_ref[d0, 0], tsel_ref[jnp.maximum(d0 - 1, 0), 0]], axis=1)
        picked = jnp.concatenate([mask_ref[2 * jg], mask_ref[2 * jg + 1]], axis=1)
        lg = (lax.dot_general(q, ks_ref[pl.ds(s, two), :], _NT, preferred_element_type=F32)
              + (bias + jnp.concatenate([picked] * NSA_GROUP, axis=0)))
        if diag:
            lg = jnp.where(jg * two + col2 <= i * t + qrow2, lg, NEG_INF)
        return lg

    def fold(x):
        return jnp.maximum(x[:, :t], x[:, t:])

    jd = i // 2
    m_vec = fold(sel_logits(jd, True))
    m_vec = lax.fori_loop(0, jd, lambda jg, mv: jnp.maximum(mv, fold(sel_logits(jg, False))), m_vec)
    m_b = jnp.broadcast_to(jnp.max(m_vec, axis=-1, keepdims=True), (rows, t))
    m_b2 = jnp.concatenate([m_b, m_b], axis=1)

    def sel_weights(jg, carry, diag):
        p = jnp.exp(sel_logits(jg, diag) - m_b2)
        s = pl.multiple_of(jg * two, two)
        acc = carry[1] + jnp.dot(p.astype(BF16), vs_ref[pl.ds(s, two), :], preferred_element_type=F32)
        return carry[0] + (p[:, :t] + p[:, t:]), acc

    carry = sel_weights(jd, (jnp.zeros((rows, t), F32), jnp.zeros((rows, HEAD_DIM), F32)), True)
    carry = lax.fori_loop(0, jd, lambda jg, c: sel_weights(jg, c, False), carry)
    o_sel = carry[1] / jnp.sum(carry[0], axis=-1, keepdims=True)

    n_back = WINDOW // t
    lgs = []
    starts = []
    for w in range(n_back + 1):
        s = pl.multiple_of(jnp.maximum(i - w, 0) * t, t)
        lg = lax.dot_general(q, kw_ref[pl.ds(s, t), :], _NT, preferred_element_type=F32) + twin_ref[w, 0]
        if w > 0:
            lg = lg + jnp.where(i >= w, 0.0, NEG_INF)
        lgs.append(lg)
        starts.append(s)
    m_vec = lgs[0]
    for lg in lgs[1:]:
        m_vec = jnp.maximum(m_vec, lg)
    m_b = jnp.broadcast_to(jnp.max(m_vec, axis=-1, keepdims=True), (rows, t))
    l_vec = jnp.zeros((rows, t), F32)
    acc = jnp.zeros((rows, HEAD_DIM), F32)
    for lg, s in zip(lgs, starts):
        p = jnp.exp(lg - m_b)
        l_vec = l_vec + p
        acc = acc + jnp.dot(p.astype(BF16), vw_ref[pl.ds(s, t), :], preferred_element_type=F32)
    o_win = acc / jnp.sum(l_vec, axis=-1, keepdims=True)

    gates = gates_ref[0]
    gn = gn_ref[0]
    for g in range(NSA_GROUP):
        sl = slice(g * t, (g + 1) * t)
        o = (gates[:, 3 * g:3 * g + 1] * o_cmp[sl] + gates[:, 3 * g + 1:3 * g + 2] * o_sel[sl]
             + gates[:, 3 * g + 2:3 * g + 3] * o_win[sl])
        o_ref[:, g * HEAD_DIM:(g + 1) * HEAD_DIM] = _rms(o, gn[:, g * HEAD_DIM:(g + 1) * HEAD_DIM]).astype(BF16)


def _sel_constants(n_chunks, n_blk, n_cmp, key_len):
    blk = np.arange(n_blk)
    lo = np.clip((blk * SEL_BLOCK - CMP_LEN) // CMP_STRIDE + 1, 0, n_cmp)
    hi = np.clip(-(-((blk + 1) * SEL_BLOCK) // CMP_STRIDE), 0, n_cmp)
    c = np.arange(n_chunks) - 1
    ind = ((c[None, :] >= lo[:, None]) & (c[None, :] < hi[:, None]) & (c[None, :] >= 0)).astype(np.float32)
    expand = (np.arange(key_len)[None, :] // SEL_BLOCK == blk[:, None]).astype(np.float32)
    return ind, expand


def _nsa_prompt(qns, kcvc, nsakv16, winkv16, tsel, twin, tcmp, gates_k, gain, batch, seq):
    m, nsw = qns.shape
    kv_heads = nsw // (NSA_GROUP * HEAD_DIM)
    t = LANES
    nq = seq // t
    n_chunks = seq // CMP_STRIDE
    assert n_chunks == t, "the compressed branch is tiled as a single 128-column tile"
    n_cmp = (seq - CMP_LEN) // CMP_STRIDE + 1
    n_blk = -(-seq // SEL_BLOCK)
    n_pick = min(N_SEL, n_blk)
    ind, expand = _sel_constants(n_chunks, n_blk, n_cmp, seq)
    ind_pad = np.zeros((t, n_chunks), np.float32)
    ind_pad[:n_blk] = ind
    exp_pad = np.zeros((t, seq), np.float32)
    exp_pad[:n_blk] = expand
    ind3 = jnp.asarray(np.concatenate([ind_pad] * 3, axis=1), BF16)
    gw = NSA_GROUP * HEAD_DIM
    body = functools.partial(_nsa_prompt_body, n_blk=n_blk, n_pick=n_pick, n_kt=nq)
    return pl.pallas_call(
        body,
        grid=(kv_heads, batch, nq),
        in_specs=[pl.BlockSpec((t, gw), lambda k, b, i: (b * nq + i, k)),
                  pl.BlockSpec((1, n_chunks, HEAD_DIM), lambda k, b, i: (b, 0, k)),
                  pl.BlockSpec((1, n_chunks, HEAD_DIM), lambda k, b, i: (b, 0, kv_heads + k)),
                  pl.BlockSpec((seq, HEAD_DIM), lambda k, b, i: (b, 2 * kv_heads + k)),
                  pl.BlockSpec((seq, HEAD_DIM), lambda k, b, i: (b, 3 * kv_heads + k)),
                  pl.BlockSpec((seq, HEAD_DIM), lambda k, b, i: (b, k)),
                  pl.BlockSpec((seq, HEAD_DIM), lambda k, b, i: (b, kv_heads + k)),
                  pl.BlockSpec((nq, 1, gw, t), lambda k, b, i: (0, k, 0, 0)),
                  pl.BlockSpec((twin.shape[0], 1, gw, t), lambda k, b, i: (0, k, 0, 0)),
                  pl.BlockSpec((1, 1, gw, t), lambda k, b, i: (i, k, 0, 0)),
                  pl.BlockSpec((t, 3 * n_chunks), lambda k, b, i: (0, 0)),
                  pl.BlockSpec((t, seq), lambda k, b, i: (0, 0)),
                  pl.BlockSpec((1, t, LANES), lambda k, b, i: (k, b * nq + i, 0)),
                  pl.BlockSpec((1, 1, gw), lambda k, b, i: (k, 0, 0))],
        out_specs=pl.BlockSpec((t, gw), lambda k, b, i: (b * nq + i, k)),
        out_shape=jax.ShapeDtypeStruct((m, nsw), BF16),
        scratch_shapes=[pltpu.VMEM((nq, t, t), F32)],
        compiler_params=_params("parallel", "parallel", "arbitrary"),
        name="nsa_prompt",
    )(qns, kcvc, kcvc, nsakv16, nsakv16, winkv16, winkv16, tsel, twin, tcmp, ind3,
      jnp.asarray(exp_pad, BF16), gates_k, gain.reshape(kv_heads, 1, gw))


def _cmp_decode_body(tab_ref, q_ref, kcvc_ref, ind_ref, ocmp_ref, score_ref, *, past, kv_heads):
    n_chunks = kcvc_ref.shape[1]
    r = lax.broadcasted_iota(jnp.int32, (1, n_chunks), 1)
    rel = past - (CMP_STRIDE * r + (CMP_LEN - CMP_STRIDE - 1))
    valid = (r >= 1) & (rel >= 0)
    score_ref[...] = jnp.zeros_like(score_ref)
    for k in range(kv_heads):
        kc = kcvc_ref[0, :, k * HEAD_DIM:(k + 1) * HEAD_DIM].astype(BF16)
        vc = kcvc_ref[0, :, (kv_heads + k) * HEAD_DIM:(kv_heads + k + 1) * HEAD_DIM].astype(BF16)
        bias = jnp.concatenate([_bias_chain(rel, tab_ref, k * NSA_GROUP + g) for g in range(NSA_GROUP)], axis=0)
        lg = lax.dot_general(q_ref[0, k], kc, _NT, preferred_element_type=F32) + bias
        p = _masked_softmax(lg, valid)
        ocmp_ref[0, k * NSA_GROUP:(k + 1) * NSA_GROUP, :] = jnp.dot(p.astype(BF16), vc, preferred_element_type=F32)
        p3 = jnp.concatenate(_split3(jnp.sum(p, axis=0, keepdims=True)), axis=1)
        score_ref[0, k:k + 1, :] = jnp.dot(p3, ind_ref[...], preferred_element_type=F32)


def _cmp_decode(table, q, kcvc, ind3, past, kv_heads):
    db = q.shape[0]
    heads = kv_heads * NSA_GROUP
    n_chunks = kcvc.shape[1]
    nj = ind3.shape[1]
    body = functools.partial(_cmp_decode_body, past=past, kv_heads=kv_heads)
    return pl.pallas_call(
        body, grid=(db,),
        in_specs=[pl.BlockSpec(memory_space=pltpu.SMEM),
                  pl.BlockSpec((1, kv_heads, NSA_GROUP, HEAD_DIM), lambda b: (b, 0, 0, 0)),
                  pl.BlockSpec((1, n_chunks, kcvc.shape[2]), lambda b: (b, 0, 0)),
                  pl.BlockSpec(ind3.shape, lambda b: (0, 0))],
        out_specs=[pl.BlockSpec((1, heads, HEAD_DIM), lambda b: (b, 0, 0)),
                   pl.BlockSpec((1, SUBLANES, nj), lambda b: (b, 0, 0))],
        out_shape=[jax.ShapeDtypeStruct((db, heads, HEAD_DIM), F32),
                   jax.ShapeDtypeStruct((db, SUBLANES, nj), F32)],
        compiler_params=_params("parallel"),
        name="nsa_cmp_decode",
    )(table, q, kcvc, ind3)


def _topk_body(s_ref, o_ref, *, n_cache_blk, n_pick):
    s = s_ref[...]
    j = lax.broadcasted_iota(jnp.int32, s.shape, 1)
    lane = lax.broadcasted_iota(jnp.int32, o_ref.shape, 1)
    s = jnp.where((j == 0) | (j == n_cache_blk - 1), BIG, s)
    s = jnp.where(j < n_cache_blk, s, -jnp.inf)
    picks = jnp.zeros(o_ref.shape, jnp.int32)
    for t in range(n_pick):
        m = jnp.max(s, axis=-1, keepdims=True)
        idx = jnp.min(jnp.where(s == m, j, 2 ** 30), axis=-1, keepdims=True)
        picks = jnp.where(lane == t, idx, picks)
        s = jnp.where(j == idx, -jnp.inf, s)
    o_ref[...] = picks


def _topk(scores, n_cache_blk, n_pick):
    rows = scores.shape[0]
    return pl.pallas_call(
        functools.partial(_topk_body, n_cache_blk=n_cache_blk, n_pick=n_pick),
        out_shape=jax.ShapeDtypeStruct((rows, LANES), jnp.int32),
        name="nsa_topk",
    )(scores)


def _sel_decode_body(pt_ref, pk_ref, *refs, n_pick, past, w_buf, kv_static):
    blocks = refs[:n_pick]
    (knew_ref, vnew_ref, win_ref, wknew_ref, wvnew_ref, q_ref, gates_ref, ocmp_ref, gn_ref,
     tab_ref, o_ref, ksc, vsc, kwc, vwc) = refs[n_pick:]
    b = pl.program_id(0)
    k = pl.program_id(1)
    kv_heads = kv_static
    q = q_ref[0, 0]
    n_slot = n_pick + 1
    sel_len = n_slot * SEL_BLOCK
    rows_tok = 4 * kv_static

    first64 = lax.broadcasted_iota(jnp.int32, (SEL_BLOCK, HEAD_DIM), 0) == 0
    for n in range(n_pick):
        ksc[n * SEL_BLOCK:(n + 1) * SEL_BLOCK, :] = (
            blocks[n][pl.ds(2 * kv_static + k, SEL_BLOCK, stride=rows_tok), :].astype(BF16))
        vsc[n * SEL_BLOCK:(n + 1) * SEL_BLOCK, :] = (
            blocks[n][pl.ds(3 * kv_static + k, SEL_BLOCK, stride=rows_tok), :].astype(BF16))
    ksc[n_pick * SEL_BLOCK:sel_len, :] = jnp.where(first64, knew_ref[0], 0.0).astype(BF16)
    vsc[n_pick * SEL_BLOCK:sel_len, :] = jnp.where(first64, vnew_ref[0], 0.0).astype(BF16)

    lane = lax.broadcasted_iota(jnp.int32, (1, sel_len), 1)
    slot = lane // SEL_BLOCK
    base = jnp.full((1, sel_len), past, jnp.int32)
    for n in range(n_pick):
        base = jnp.where(slot == n, pk_ref[(b * kv_heads + k) * n_pick + n] * SEL_BLOCK, base)
    rel = past - (base + (lane & (SEL_BLOCK - 1)))
    bias = jnp.concatenate([_bias_chain(rel, tab_ref, k * NSA_GROUP + g) for g in range(NSA_GROUP)], axis=0)
    lg = lax.dot_general(q, ksc[...], _NT, preferred_element_type=F32) + bias
    p = _masked_softmax(lg, rel >= 0)
    o_sel = jnp.dot(p.astype(BF16), vsc[...], preferred_element_type=F32)

    win_len = kwc.shape[0]
    firstw = lax.broadcasted_iota(jnp.int32, (win_len - w_buf, HEAD_DIM), 0) == 0
    kwc[0:w_buf, :] = win_ref[pl.ds(k, w_buf, stride=2 * kv_static), :].astype(BF16)
    vwc[0:w_buf, :] = win_ref[pl.ds(kv_static + k, w_buf, stride=2 * kv_static), :].astype(BF16)
    kwc[w_buf:win_len, :] = jnp.where(firstw, wknew_ref[0], 0.0).astype(BF16)
    vwc[w_buf:win_len, :] = jnp.where(firstw, wvnew_ref[0], 0.0).astype(BF16)
    relw = w_buf - lax.broadcasted_iota(jnp.int32, (1, win_len), 1)
    bias = jnp.concatenate([_bias_chain(relw, tab_ref, k * NSA_GROUP + g) for g in range(NSA_GROUP)], axis=0)
    lg = lax.dot_general(q, kwc[...], _NT, preferred_element_type=F32) + bias
    p = _masked_softmax(lg, (relw >= 0) & (relw < WINDOW))
    o_win = jnp.dot(p.astype(BF16), vwc[...], preferred_element_type=F32)

    o = gates_ref[0, 0, 0] * ocmp_ref[0, 0] + gates_ref[0, 0, 1] * o_sel + gates_ref[0, 0, 2] * o_win
    o_ref[0, 0] = _rms(o, gn_ref[0])


def _sel_decode(page_table, picks, cache_rows, nsa_new, win_rows, win_new, q, gates, ocmp, gain, table,
                n_pick, past, w_buf):
    db, kv_heads = q.shape[:2]
    n_pages = page_table.shape[1]
    halves = PAGE_SIZE // SEL_BLOCK
    rows_tok = 4 * kv_heads

    def pick_spec(n):
        def idx(b, k, pt, pk):
            j = pk[(b * kv_heads + k) * n_pick + n]
            return (pt[b * n_pages + j // halves] * halves + j % halves, 0)
        return pl.BlockSpec((SEL_BLOCK * rows_tok, HEAD_DIM), idx)

    def row_spec(col0):
        return pl.BlockSpec((1, 1, HEAD_DIM), lambda b, k, pt, pk: (b, 0, col0 + k))

    def head_spec(shape):
        nd = len(shape)
        return pl.BlockSpec((1, 1) + shape, lambda b, k, pt, pk: (b, k) + (0,) * nd)

    win_len = w_buf + LANES
    grid_spec = pltpu.PrefetchScalarGridSpec(
        num_scalar_prefetch=2,
        grid=(db, kv_heads),
        in_specs=[pick_spec(n) for n in range(n_pick)]
        + [row_spec(2 * kv_heads), row_spec(3 * kv_heads),
           pl.BlockSpec((w_buf * 2 * kv_heads, HEAD_DIM), lambda b, k, pt, pk: (b, 0)),
           row_spec(0), row_spec(kv_heads),
           head_spec((NSA_GROUP, HEAD_DIM)), head_spec((3, NSA_GROUP, 1)), head_spec((NSA_GROUP, HEAD_DIM)),
           pl.BlockSpec((1, NSA_GROUP, HEAD_DIM), lambda b, k, pt, pk: (k, 0, 0)),
           pl.BlockSpec(memory_space=pltpu.SMEM)],
        out_specs=head_spec((NSA_GROUP, HEAD_DIM)),
        scratch_shapes=[pltpu.VMEM(((n_pick + 1) * SEL_BLOCK, HEAD_DIM), BF16),
                        pltpu.VMEM(((n_pick + 1) * SEL_BLOCK, HEAD_DIM), BF16),
                        pltpu.VMEM((win_len, HEAD_DIM), BF16),
                        pltpu.VMEM((win_len, HEAD_DIM), BF16)])
    body = functools.partial(_sel_decode_body, n_pick=n_pick, past=past, w_buf=w_buf, kv_static=kv_heads)
    return pl.pallas_call(
        body, grid_spec=grid_spec,
        out_shape=jax.ShapeDtypeStruct((db, kv_heads, NSA_GROUP, HEAD_DIM), F32),
        compiler_params=_params("parallel", "parallel"),
        name="nsa_sel_decode",
    )(page_table.reshape(-1), picks, *([cache_rows] * n_pick), nsa_new, nsa_new, win_rows,
      win_new, win_new, q, gates, ocmp, gain.reshape(kv_heads, NSA_GROUP, HEAD_DIM), table)


def _out_proj_body(x_ref, a_ref, b_ref, w_ref, o_ref):
    half = a_ref.shape[1]
    o_ref[...] = (x_ref[...] + jnp.dot(a_ref[...], w_ref[0:half, :], preferred_element_type=F32)
                  + jnp.dot(b_ref[...], w_ref[half:2 * half, :], preferred_element_type=F32))


def _out_proj(x, a, b, w, tm):
    m, d = x.shape
    half = a.shape[1]
    return pl.pallas_call(
        _out_proj_body, grid=(m // tm,),
        in_specs=[pl.BlockSpec((tm, d), lambda i: (i, 0)),
                  pl.BlockSpec((tm, half), lambda i: (i, 0)),
                  pl.BlockSpec((tm, half), lambda i: (i, 0)),
                  _resident(w.shape)],
        out_specs=pl.BlockSpec((tm, d), lambda i: (i, 0)),
        out_shape=jax.ShapeDtypeStruct((m, d), F32),
        compiler_params=_params("parallel"),
        name="out_proj",
    )(x, a, b, w)


def _norm_matmul_body(x_ref, g_ref, w_ref, o_ref):
    o_ref[...] = jnp.dot(_rms(x_ref[...], g_ref[...]).astype(BF16), w_ref[...], preferred_element_type=F32)


def _norm_matmul(x, g, w, tm):
    m, d = x.shape
    n = w.shape[1]
    return pl.pallas_call(
        _norm_matmul_body, grid=(m // tm,),
        in_specs=[pl.BlockSpec((tm, d), lambda i: (i, 0)),
                  pl.BlockSpec((1, d), lambda i: (0, 0)),
                  _resident(w.shape)],
        out_specs=pl.BlockSpec((tm, n), lambda i: (i, 0)),
        out_shape=jax.ShapeDtypeStruct((m, n), F32),
        compiler_params=_params("parallel"),
        name="mem_kv_proj",
    )(x, g.reshape(1, d), w)


def _cross_body(x_ref, g_ref, wq_ref, wo_ref, mem_ref, o_ref, *, scale, n_mem, row_layout):
    x = x_ref[0]
    hb = _rms(x, g_ref[...]).astype(BF16)
    qh = (jnp.dot(hb, wq_ref[...], preferred_element_type=F32) * scale).astype(BF16)
    width = MEM_HEADS * HEAD_DIM
    outs = []
    for h in range(MEM_HEADS):
        if row_layout:
            k = mem_ref[pl.ds(h, n_mem, stride=2 * MEM_HEADS), :].astype(BF16)
            v = mem_ref[pl.ds(MEM_HEADS + h, n_mem, stride=2 * MEM_HEADS), :].astype(BF16)
        else:
            k = mem_ref[0, :, h * HEAD_DIM:(h + 1) * HEAD_DIM].astype(BF16)
            v = mem_ref[0, :, width + h * HEAD_DIM:width + (h + 1) * HEAD_DIM].astype(BF16)
        lg = lax.dot_general(qh[:, h * HEAD_DIM:(h + 1) * HEAD_DIM], k, _NT, preferred_element_type=F32)
        e = jnp.exp(lg - jnp.max(lg, axis=-1, keepdims=True))
        p = e / jnp.sum(e, axis=-1, keepdims=True)
        outs.append(jnp.dot(p.astype(BF16), v, preferred_element_type=F32).astype(BF16))
    o = jnp.concatenate(outs, axis=1)
    o_ref[0] = x + jnp.dot(o, wo_ref[...], preferred_element_type=F32)


def _cross(x3, g, wq, wo, mem, n_mem, tm):
    nb, t, d = x3.shape
    row_layout = mem.ndim == 2
    if row_layout:
        mem_spec = pl.BlockSpec((n_mem * 2 * MEM_HEADS, HEAD_DIM), lambda b, i: (b, 0))
    else:
        mem_spec = pl.BlockSpec((1, n_mem, mem.shape[2]), lambda b, i: (b, 0, 0))
    body = functools.partial(_cross_body, scale=1.0 / math.sqrt(HEAD_DIM), n_mem=n_mem, row_layout=row_layout)
    return pl.pallas_call(
        body, grid=(nb, t // tm),
        in_specs=[pl.BlockSpec((1, tm, d), lambda b, i: (b, i, 0)),
                  pl.BlockSpec((1, d), lambda b, i: (0, 0)),
                  _resident(wq.shape), _resident(wo.shape),
                  mem_spec],
        out_specs=pl.BlockSpec((1, tm, d), lambda b, i: (b, i, 0)),
        out_shape=jax.ShapeDtypeStruct((nb, t, d), F32),
        compiler_params=_params("parallel", "parallel"),
        name="cross_attn",
    )(x3, g.reshape(1, d), wq, wo, mem)


def _mlp_body(x_ref, g_ref, wu_ref, wd_ref, gf_ref, o_ref, h_ref, acc_ref):
    f = pl.program_id(1)

    @pl.when(f == 0)
    def _():
        h_ref[...] = _rms(x_ref[...], g_ref[...]).astype(BF16)
        acc_ref[...] = jnp.zeros_like(acc_ref)

    u = jnp.maximum(jnp.dot(h_ref[...], wu_ref[...], preferred_element_type=F32), 0.0)
    acc_ref[...] += jnp.dot((u * u).astype(BF16), wd_ref[...], preferred_element_type=F32)

    @pl.when(f == pl.num_programs(1) - 1)
    def _():
        o_ref[...] = _rms(x_ref[...] + acc_ref[...], gf_ref[...])


def _mlp_final(x, g, wu, wd, gf, tm, tf):
    m, d = x.shape
    dff = wu.shape[1]
    return pl.pallas_call(
        _mlp_body, grid=(m // tm, dff // tf),
        in_specs=[pl.BlockSpec((tm, d), lambda i, f: (i, 0)),
                  pl.BlockSpec((1, d), lambda i, f: (0, 0)),
                  pl.BlockSpec((d, tf), lambda i, f: (0, f)),
                  pl.BlockSpec((tf, d), lambda i, f: (f, 0)),
                  pl.BlockSpec((1, d), lambda i, f: (0, 0))],
        out_specs=pl.BlockSpec((tm, d), lambda i, f: (i, 0)),
        out_shape=jax.ShapeDtypeStruct((m, d), F32),
        scratch_shapes=[pltpu.VMEM((tm, d), BF16), pltpu.VMEM((tm, d), F32)],
        compiler_params=_params("parallel", "arbitrary"),
        name="mlp_final",
    )(x, g.reshape(1, d), wu, wd, gf.reshape(1, d))


def _w1cat(w1):
    half = (CMP_LEN // 2) * HEAD_DIM
    hid = w1.shape[1]
    a = w1[:half].reshape(CMP_STRIDE // 2, 2 * HEAD_DIM, hid)
    b = w1[half:].reshape(CMP_STRIDE // 2, 2 * HEAD_DIM, hid)
    return jnp.concatenate([a, b], axis=-1).astype(BF16)


def kernel(x_prompt, x_sample, cache_sb_kv, cache_nsa_kv, cache_win_kv, cache_mem_kv, page_table, mem_prompt,
           norm_mix, w_in, norm_sb_out, norm_nsa_out, w_out, rel_bias_table,
           cmp_pe_k, cmp_pe_v, w_cmp_k1, w_cmp_k2, w_cmp_v1, w_cmp_v2,
           norm_cross, norm_mem, w_cross_q, w_mem_kv, w_cross_o,
           norm_ffn, w_up, w_down, norm_final):
    batch, seq, d = x_prompt.shape
    db, dec_seq, _ = x_sample.shape
    assert dec_seq == 1
    depth = w_in.shape[0]
    assert depth == 1, "the final norm is fused into the MLP of the only layer"
    n_pool = cache_sb_kv.shape[1]
    n_pages = page_table.shape[1]
    past = n_pages * PAGE_SIZE
    sb_heads = cache_sb_kv.shape[4]
    kv_heads = cache_nsa_kv.shape[4]
    nsa_heads = kv_heads * NSA_GROUP
    sbw = sb_heads * HEAD_DIM
    nsw = nsa_heads * HEAD_DIM
    kvw = kv_heads * HEAD_DIM
    n_gates = 3 * nsa_heads
    w_buf = cache_win_kv.shape[2]
    win_keep = min(WINDOW, seq)
    n_mem = mem_prompt.shape[1]
    mem_w = MEM_HEADS * HEAD_DIM
    m = batch * seq
    l = 0

    w_in_p = jnp.pad(w_in[l].astype(BF16), ((0, 0), (0, LANES - n_gates)))
    w1cat = jnp.stack([_w1cat(w_cmp_k1[l]), _w1cat(w_cmp_v1[l])])
    w2 = jnp.stack([w_cmp_k2[l], w_cmp_v2[l]]).astype(BF16)
    pe = jnp.stack([cmp_pe_k[l].reshape(1, -1), cmp_pe_v[l].reshape(1, -1)])
    pe_term = _pe_term(jnp.broadcast_to(pe, (2, SUBLANES, pe.shape[-1])), jnp.stack([w_cmp_k1[l], w_cmp_v1[l]]))
    w_out_b = w_out[l].astype(BF16)

    xp = x_prompt.reshape(m, d)
    (qsb, sbkv, sbkv16, qns, nsakv, nsakv16, winkv, winkv16, gates) = _project(xp, norm_mix[l], w_in_p, 256)
    o_sb = _sb_prompt(qsb, sbkv16, norm_sb_out[l], batch, seq)

    pages_seq = seq // PAGE_SIZE
    prompt_pages = jnp.arange(batch * pages_seq, dtype=jnp.int32).reshape(batch, pages_seq)
    kcvc = _compress(nsakv.reshape(batch * pages_seq, PAGE_SIZE, 4 * kvw), prompt_pages, w1cat, w2, pe_term,
                     kv_heads, pages_seq)
    nq = seq // LANES
    tsel = _bias_tiles(rel_bias_table, nq, LANES, 1, 0)
    tcmp = _bias_tiles(rel_bias_table, nq, LANES, CMP_STRIDE, CMP_LEN - CMP_STRIDE - 1)
    gates_k = jnp.pad(gates[:, :n_gates].reshape(m, kv_heads, 3 * NSA_GROUP).transpose(1, 0, 2),
                      ((0, 0), (0, 0), (0, LANES - 3 * NSA_GROUP)))
    twin = _bias_tiles(rel_bias_table, WINDOW // LANES + 1, LANES, 1, 0, window=True)
    o_ns = _nsa_prompt(qns, kcvc, nsakv16, winkv16, tsel, twin, tcmp, gates_k, norm_nsa_out[l], batch, seq)
    xp = _out_proj(xp, o_sb, o_ns, w_out_b, 512)

    sb_kv_prompt = sbkv.reshape(1, batch, seq, 2, sb_heads, HEAD_DIM)
    nsa_kv_prompt = nsakv.reshape(1, batch, seq, 4, kv_heads, HEAD_DIM)
    win_kv_prompt = winkv.reshape(1, batch, seq, 2, kv_heads, HEAD_DIM)[:, :, seq - win_keep:]

    xs = x_sample.reshape(db, d)
    (qsb_s, sbkv_s, _, qns_s, nsakv_s, _, winkv_s, _, gates_s) = _project(xs, norm_mix[l], w_in_p, db)
    o_sb_s = _sb_decode(qsb_s.reshape(db, sb_heads, HEAD_DIM), cache_sb_kv[l].reshape(-1, HEAD_DIM), page_table,
                        norm_sb_out[l], 8)

    nsa_rows = cache_nsa_kv[l].reshape(-1, HEAD_DIM)
    kcvc_s = _compress(nsa_rows, page_table, w1cat, w2, pe_term, kv_heads, 16)
    n_chunks_s = past // CMP_STRIDE
    tk = past + 1
    n_cmp_s = (tk - CMP_LEN) // CMP_STRIDE + 1
    n_blk_s = -(-tk // SEL_BLOCK)
    n_cache_blk = past // SEL_BLOCK
    assert n_blk_s == n_cache_blk + 1 and n_cache_blk >= N_SEL
    nj = -(-n_blk_s // LANES) * LANES
    ind_s, _ = _sel_constants(n_chunks_s, n_blk_s, n_cmp_s, 0)
    ind_s_pad = np.zeros((nj, n_chunks_s), np.float32)
    ind_s_pad[:n_blk_s] = ind_s
    ind3_s = jnp.asarray(np.concatenate([ind_s_pad.T] * 3, axis=0), BF16)
    q_dec = qns_s.reshape(db, kv_heads, NSA_GROUP, HEAD_DIM)
    ocmp_s, scores = _cmp_decode(rel_bias_table, q_dec, kcvc_s, ind3_s, past, kv_heads)
    n_pick_cache = N_SEL - 1
    picks = _topk(scores[:, :kv_heads].reshape(db * kv_heads, nj), n_cache_blk, n_pick_cache)
    picks = picks[:, :n_pick_cache].reshape(-1)
    gates_d = gates_s[:, :n_gates].reshape(db, kv_heads, NSA_GROUP, 3).transpose(0, 1, 3, 2)[..., None]
    o_ns_s = _sel_decode(page_table, picks, nsa_rows, nsakv_s.reshape(db, 1, 4 * kvw),
                         cache_win_kv[l].reshape(-1, HEAD_DIM), winkv_s.reshape(db, 1, 2 * kvw), q_dec, gates_d,
                         ocmp_s.reshape(db, kv_heads, NSA_GROUP, HEAD_DIM), norm_nsa_out[l],
                         rel_bias_table, n_pick_cache, past, w_buf)
    xs = _out_proj(xs, o_sb_s.reshape(db, sbw).astype(BF16), o_ns_s.reshape(db, nsw).astype(BF16), w_out_b, db)

    sb_kv_sample = sbkv_s.reshape(1, db, 1, 2, sb_heads, HEAD_DIM)
    nsa_kv_sample = nsakv_s.reshape(1, db, 1, 4, kv_heads, HEAD_DIM)
    win_new = winkv_s.reshape(db, 1, 2, kv_heads, HEAD_DIM)
    win_kv_sample = jnp.concatenate([cache_win_kv[l], win_new], axis=1)[None, :, 1:]

    mem_kv = _norm_matmul(mem_prompt.reshape(batch * n_mem, d), norm_mem[l], w_mem_kv[l].astype(BF16), n_mem)
    wq = w_cross_q[l].astype(BF16)
    wo = w_cross_o[l].astype(BF16)
    xp = _cross(xp.reshape(batch, seq, d), norm_cross[l], wq, wo, mem_kv.reshape(batch, n_mem, 2 * mem_w),
                n_mem, 512)
    xs = _cross(xs.reshape(db, 1, d), norm_cross[l], wq, wo, cache_mem_kv[l].reshape(-1, HEAD_DIM), n_mem, 1)
    mem_kv_prompt = mem_kv.reshape(1, batch, n_mem, 2, MEM_HEADS, HEAD_DIM)

    wu = w_up[l].astype(BF16)
    wd = w_down[l].astype(BF16)
    y_prompt = _mlp_final(xp.reshape(m, d), norm_ffn[l], wu, wd, norm_final, 512, 512).reshape(batch, seq, d)
    y_sample = _mlp_final(xs.reshape(db, d), norm_ffn[l], wu, wd, norm_final, db, 512).reshape(db, 1, d)

    return (y_prompt, y_sample, sb_kv_prompt, sb_kv_sample, nsa_kv_prompt, nsa_kv_sample,
            win_kv_prompt, win_kv_sample, mem_kv_prompt)
```

```python
import functools
import math

import numpy as np
import jax
import jax.numpy as jnp
from jax import lax
from jax.experimental import pallas as pl
from jax.experimental.pallas import tpu as pltpu

HEAD_DIM = 128
NSA_GROUP = 4
MEM_HEADS = 4
PAGE_SIZE = 128
CMP_LEN = 32
CMP_STRIDE = 16
SEL_BLOCK = 64
N_SEL = 16
WINDOW = 512
N_BUCKETS = 32
MAX_DISTANCE = 1024
RMS_EPS = 1e-6
NEG_INF = -1e30
BIG = 1e30
LANES = 128
SUBLANES = 8
VMEM_LIMIT = 56 * 1024 * 1024
_PLANE_PITCH = PAGE_SIZE + SUBLANES

F32 = jnp.float32
BF16 = jnp.bfloat16
_NT = (((1,), (1,)), ((), ()))


def _bucket_thresholds():
    max_exact = N_BUCKETS // 2
    n_log = N_BUCKETS - max_exact
    ratio = MAX_DISTANCE // max_exact
    th = list(range(max_exact)) + [max_exact]
    for m in range(1, n_log):
        n = th[-1]
        while n ** n_log < (max_exact ** n_log) * (ratio ** m):
            n += 1
        th.append(n)
    return tuple(th)


THRESHOLDS = _bucket_thresholds()


def _bias_chain(rel, tab_ref, h):
    v = jnp.full(rel.shape, tab_ref[0, h], F32)
    for b in range(1, N_BUCKETS):
        v = jnp.where(rel >= THRESHOLDS[b], tab_ref[b, h], v)
    return v


def _rms(x, g):
    return x * lax.rsqrt(jnp.mean(x * x, axis=-1, keepdims=True) + RMS_EPS) * g


def _softplus(z):
    return jnp.maximum(z, 0.0) + jnp.log(1.0 + jnp.exp(-jnp.abs(z)))


def _split2(x):
    hi = x.astype(BF16)
    lo = (x - hi.astype(F32)).astype(BF16)
    return hi, lo


def _split3(x):
    hi = x.astype(BF16)
    r = x - hi.astype(F32)
    mid = r.astype(BF16)
    lo = (r - mid.astype(F32)).astype(BF16)
    return hi, mid, lo


def _masked_softmax(lg, valid):
    lgm = jnp.where(valid, lg, NEG_INF)
    m = jnp.max(lgm, axis=-1, keepdims=True)
    e = jnp.where(valid, jnp.exp(lgm - m), 0.0)
    s = jnp.sum(e, axis=-1, keepdims=True)
    return e / jnp.where(s > 0.0, s, 1.0)


def _params(*sem):
    return pltpu.CompilerParams(dimension_semantics=sem, vmem_limit_bytes=VMEM_LIMIT)


def _resident(shape):
    nd = len(shape)
    return pl.BlockSpec(shape, lambda *_: (0,) * nd, pipeline_mode=pl.Buffered(1))


def _proj_body(x_ref, g_ref, w_ref, qsb_ref, sbkv_ref, sbkv16_ref, qns_ref, nsakv_ref,
               nsakv16_ref, winkv_ref, winkv16_ref, gates_ref, *, sbw, nsw, kvw, scale):
    hb = _rms(x_ref[...], g_ref[...]).astype(BF16)

    def mm(c0, n):
        return jnp.dot(hb, w_ref[:, c0:c0 + n], preferred_element_type=F32)

    c = 0
    qsb_ref[...] = (mm(c, sbw) * scale).astype(BF16)
    c += sbw
    for half in range(2):
        kv = mm(c, sbw)
        sbkv_ref[:, half * sbw:(half + 1) * sbw] = kv
        sbkv16_ref[:, half * sbw:(half + 1) * sbw] = kv.astype(BF16)
        c += sbw
    qns_ref[...] = (mm(c, nsw) * scale).astype(BF16)
    c += nsw
    kv = mm(c, 4 * kvw)
    nsakv_ref[...] = kv
    nsakv16_ref[...] = kv.astype(BF16)
    c += 4 * kvw
    kv = mm(c, 2 * kvw)
    winkv_ref[...] = kv
    winkv16_ref[...] = kv.astype(BF16)
    c += 2 * kvw
    gates_ref[...] = jax.nn.sigmoid(mm(c, LANES))


def _project(x, g, w_pad, tm):
    m, d = x.shape
    sbw = nsw = d // 2
    kvw = nsw // NSA_GROUP
    widths = (sbw, 2 * sbw, 2 * sbw, nsw, 4 * kvw, 4 * kvw, 2 * kvw, 2 * kvw, LANES)
    dtypes = (BF16, F32, BF16, BF16, F32, BF16, F32, BF16, F32)
    body = functools.partial(_proj_body, sbw=sbw, nsw=nsw, kvw=kvw, scale=1.0 / math.sqrt(HEAD_DIM))
    return pl.pallas_call(
        body,
        grid=(m // tm,),
        in_specs=[pl.BlockSpec((tm, d), lambda i: (i, 0)),
                  pl.BlockSpec((1, d), lambda i: (0, 0)),
                  _resident(w_pad.shape)],
        out_specs=[pl.BlockSpec((tm, w), lambda i: (i, 0)) for w in widths],
        out_shape=[jax.ShapeDtypeStruct((m, w), dt) for w, dt in zip(widths, dtypes)],
        compiler_params=_params("parallel"),
        name="in_proj",
    )(x, g.reshape(1, d), w_pad)


def _bias_tile_body(tab_ref, o_ref, *, q_stride, k_stride, k_off, window, n_heads):
    m = pl.program_id(0)
    i = lax.broadcasted_iota(jnp.int32, (LANES, LANES), 0)
    j = lax.broadcasted_iota(jnp.int32, (LANES, LANES), 1)
    rel = m * q_stride + i - (k_stride * j + k_off)
    for h in range(n_heads):
        v = _bias_chain(rel, tab_ref, h)
        if window:
            v = jnp.where((rel >= 0) & (rel < WINDOW), v, NEG_INF)
        o_ref[0, h] = v


def _bias_tiles(table, n_tiles, q_stride, k_stride, k_off, window=False):
    n_heads = table.shape[1]
    body = functools.partial(_bias_tile_body, q_stride=q_stride, k_stride=k_stride, k_off=k_off, window=window,
                             n_heads=n_heads)
    out = pl.pallas_call(
        body,
        grid=(n_tiles,),
        in_specs=[pl.BlockSpec(memory_space=pltpu.SMEM)],
        out_specs=pl.BlockSpec((1, n_heads, LANES, LANES), lambda m: (m, 0, 0, 0)),
        out_shape=jax.ShapeDtypeStruct((n_tiles, n_heads, LANES, LANES), F32),
        compiler_params=_params("parallel"),
        name="bias_tiles",
    )(table)
    return out.reshape(n_tiles, n_heads // NSA_GROUP, NSA_GROUP * LANES, LANES)


def _cumsum_rhs(t):
    u = np.tril(np.ones((t, t), np.float32))
    half = np.concatenate([u, np.ones((t, t), np.float32)], axis=1)
    return jnp.asarray(np.concatenate([half, half], axis=0), BF16)


def _sb_tile(q, k, v, uo, carry, acc, valid):
    t = q.shape[0]
    sub = uo.shape[1] // 2
    z = lax.dot_general(q, k, _NT, preferred_element_type=F32)
    sp = _softplus(z)
    if valid is not None:
        sp = jnp.where(valid, sp, 0.0)
    parts = []
    for s in reversed(range(t // sub)):
        hi, lo = _split2(sp[:, s * sub:(s + 1) * sub])
        ct = jnp.dot(jnp.concatenate([hi, lo], axis=1), uo, preferred_element_type=F32)
        parts.insert(0, jnp.exp(z[:, s * sub:(s + 1) * sub] - (ct[:, :sub] + carry)))
        carry = carry + ct[:, sub:]
    a = jnp.concatenate(parts, axis=1)
    if valid is not None:
        a = jnp.where(valid, a, 0.0)
    acc = acc + jnp.dot(a.astype(BF16), v, preferred_element_type=F32)
    return carry, acc


def _sb_prompt_body(q_ref, k_ref, v_ref, g_ref, uo_ref, o_ref, *, tq, hps):
    qi = pl.program_id(2)
    uo = uo_ref[...]
    row = lax.broadcasted_iota(jnp.int32, (tq, tq), 0)
    col = lax.broadcasted_iota(jnp.int32, (tq, tq), 1)
    lanes = [slice(h * HEAD_DIM, (h + 1) * HEAD_DIM) for h in range(hps)]
    qs = [q_ref[:, sl] for sl in lanes]

    def tiles(start, state, valid):
        out = []
        for h, sl in enumerate(lanes):
            out.extend(_sb_tile(qs[h], k_ref[pl.ds(start, tq), sl], v_ref[pl.ds(start, tq), sl], uo,
                                state[2 * h], state[2 * h + 1], valid))
        return tuple(out)

    state = (jnp.zeros((tq, uo.shape[1] // 2), F32), jnp.zeros((tq, HEAD_DIM), F32)) * hps
    state = tiles(pl.multiple_of(qi * tq, tq), state, col < row)
    state = lax.fori_loop(0, qi, lambda t, st: tiles(pl.multiple_of((qi - 1 - t) * tq, tq), st, None), state)
    for h, sl in enumerate(lanes):
        o_ref[:, sl] = _rms(state[2 * h + 1], g_ref[0, :, sl]).astype(BF16)


def _sb_prompt(qsb, sbkv16, gain, batch, seq):
    m, sbw = qsb.shape
    heads = sbw // HEAD_DIM
    tq = 2 * LANES
    hps = 8
    nq = seq // tq
    wide = hps * HEAD_DIM
    groups = heads // hps
    return pl.pallas_call(
        functools.partial(_sb_prompt_body, tq=tq, hps=hps),
        grid=(batch, groups, nq),
        in_specs=[pl.BlockSpec((tq, wide), lambda b, h, i: (b * nq + i, h)),
                  pl.BlockSpec((seq, wide), lambda b, h, i: (b, h)),
                  pl.BlockSpec((seq, wide), lambda b, h, i: (b, groups + h)),
                  pl.BlockSpec((1, 1, wide), lambda b, h, i: (h, 0, 0)),
                  pl.BlockSpec((2 * LANES, 2 * LANES), lambda b, h, i: (0, 0))],
        out_specs=pl.BlockSpec((tq, wide), lambda b, h, i: (b * nq + i, h)),
        out_shape=jax.ShapeDtypeStruct((m, sbw), BF16),
        compiler_params=_params("parallel", "parallel", "arbitrary"),
        name="sb_prompt",
    )(qsb, sbkv16, sbkv16, gain.reshape(groups, 1, wide), _cumsum_rhs(LANES))


def _sb_decode_body(pt_ref, *refs, n_pages_step, n_steps, heads):
    pages = refs[:n_pages_step]
    q_ref, uo_ref, g_ref, o_ref, acc_ref, carry_ref, plane_ref = refs[n_pages_step:]
    s = pl.program_id(1)
    rows_tok = 2 * heads
    pitch = _PLANE_PITCH

    @pl.when(s == 0)
    def _():
        acc_ref[...] = jnp.zeros_like(acc_ref)
        carry_ref[...] = jnp.zeros_like(carry_ref)

    for p in range(n_pages_step):
        for tok in range(PAGE_SIZE):
            for kv in range(2):
                plane_ref[pl.ds((p * rows_tok + kv * heads) * pitch + tok, heads, stride=pitch), :] = (
                    pages[p][pl.ds(tok * rows_tok + kv * heads, heads), :])

    def plane(p, r):
        return plane_ref[pl.ds((p * rows_tok + r) * pitch, PAGE_SIZE), :].astype(BF16)

    q = q_ref[0]
    uo = uo_ref[...]
    rowi = lax.broadcasted_iota(jnp.int32, (heads, HEAD_DIM), 0)
    zs = []
    for p in range(n_pages_step):
        z = jnp.zeros((heads, PAGE_SIZE), F32)
        for h in range(heads):
            z = jnp.where(rowi == h, lax.dot_general(q, plane(p, h), _NT, preferred_element_type=F32), z)
        zs.append(z)
    cts = []
    for z in zs:
        hi, lo = _split2(_softplus(z))
        cts.append(jnp.dot(jnp.concatenate([hi, lo], axis=1), uo, preferred_element_type=F32))
    carry = carry_ref[...]
    acc = acc_ref[...]
    for p in range(n_pages_step):
        a = jnp.exp(zs[p] - (cts[p][:, :PAGE_SIZE] + carry)).astype(BF16)
        carry = carry + cts[p][:, PAGE_SIZE:]
        for h in range(heads):
            acc = acc + jnp.where(rowi == h, jnp.dot(a, plane(p, heads + h), preferred_element_type=F32), 0.0)
    acc_ref[...] = acc
    carry_ref[...] = carry

    @pl.when(s == n_steps - 1)
    def _():
        o_ref[0] = _rms(acc_ref[...], g_ref[...])


def _sb_decode(q, cache_rows, page_table, gain, pages_per_step):
    db, heads, _ = q.shape
    n_pages = page_table.shape[1]
    n_steps = n_pages // pages_per_step
    page_rows = PAGE_SIZE * 2 * heads

    def page_spec(p):
        def idx(b, s, pt):
            return (pt[b * n_pages + n_pages - 1 - (s * pages_per_step + p)], 0)
        return pl.BlockSpec((page_rows, HEAD_DIM), idx)

    grid_spec = pltpu.PrefetchScalarGridSpec(
        num_scalar_prefetch=1,
        grid=(db, n_steps),
        in_specs=[page_spec(p) for p in range(pages_per_step)] + [
            pl.BlockSpec((1, heads, HEAD_DIM), lambda b, s, pt: (b, 0, 0)),
            pl.BlockSpec((2 * PAGE_SIZE, 2 * PAGE_SIZE), lambda b, s, pt: (0, 0)),
            pl.BlockSpec((heads, HEAD_DIM), lambda b, s, pt: (0, 0))],
        out_specs=pl.BlockSpec((1, heads, HEAD_DIM), lambda b, s, pt: (b, 0, 0)),
        scratch_shapes=[pltpu.VMEM((heads, HEAD_DIM), F32), pltpu.VMEM((heads, PAGE_SIZE), F32),
                        pltpu.VMEM((pages_per_step * 2 * heads * _PLANE_PITCH, HEAD_DIM), F32)])
    body = functools.partial(_sb_decode_body, n_pages_step=pages_per_step, n_steps=n_steps, heads=heads)
    return pl.pallas_call(
        body, grid_spec=grid_spec,
        out_shape=jax.ShapeDtypeStruct((db, heads, HEAD_DIM), F32),
        compiler_params=_params("parallel", "arbitrary"),
        name="sb_decode",
    )(page_table.reshape(-1), *([cache_rows] * pages_per_step), q, _cumsum_rhs(PAGE_SIZE),
      gain.reshape(heads, HEAD_DIM))


def _pe_term_body(pe_ref, w_ref, o_ref):
    o_ref[0] = jnp.dot(pe_ref[0].astype(BF16), w_ref[0].astype(BF16), preferred_element_type=F32)


def _pe_term(pe, w1):
    n, _, kdim = pe.shape
    hid = w1.shape[-1]
    return pl.pallas_call(
        _pe_term_body, grid=(n,),
        in_specs=[pl.BlockSpec((1, SUBLANES, kdim), lambda t: (t, 0, 0)),
                  pl.BlockSpec((1, kdim, hid), lambda t: (t, 0, 0))],
        out_specs=pl.BlockSpec((1, SUBLANES, hid), lambda t: (t, 0, 0)),
        out_shape=jax.ShapeDtypeStruct((n, SUBLANES, hid), F32),
        compiler_params=_params("parallel"),
        name="cmp_pe_term",
    )(pe, w1)


def _compress_body(pt_ref, *refs, n_pages_step, kv_heads, row_layout):
    pages = refs[:n_pages_step]
    w1_ref, w2_ref, pe_ref, o_ref, carry_ref, tok_ref = refs[n_pages_step:]
    chunks_page = PAGE_SIZE // CMP_STRIDE
    rows = n_pages_step * chunks_page
    rows_tok = 4 * kv_heads
    pitch = _PLANE_PITCH

    for idx in range(2 * kv_heads):
        for p in range(n_pages_step):
            for v in range(PAGE_SIZE // SUBLANES):
                if row_layout:
                    x = pages[p][pl.ds(v * SUBLANES * rows_tok + idx, SUBLANES, stride=rows_tok), :]
                else:
                    x = pages[p][0, v * SUBLANES:(v + 1) * SUBLANES, idx * HEAD_DIM:(idx + 1) * HEAD_DIM]
                chunk, l0 = divmod(v * SUBLANES, CMP_STRIDE)
                tok_ref[pl.ds((idx * CMP_STRIDE + l0) * pitch + p * chunks_page + chunk, SUBLANES,
                              stride=pitch), :] = x

    @pl.when(pl.program_id(1) == 0)
    def _():
        carry_ref[...] = jnp.zeros_like(carry_ref)

    rowi = lax.broadcasted_iota(jnp.int32, (rows, HEAD_DIM), 0)
    for t in range(2):
        x = jnp.concatenate(
            [jnp.concatenate([tok_ref[pl.ds(((t * kv_heads + k) * CMP_STRIDE + l) * pitch, rows), :].astype(BF16)
                              for l in range(CMP_STRIDE)], axis=1) for k in range(kv_heads)], axis=0)
        acc = jnp.dot(x, w1_ref[t], preferred_element_type=F32)
        hids = []
        for k in range(kv_heads):
            idx = t * kv_heads + k
            first = acc[k * rows:(k + 1) * rows, :HEAD_DIM]
            second = acc[k * rows:(k + 1) * rows, HEAD_DIM:]
            prev_first = jnp.where(rowi == 0, carry_ref[idx], pltpu.roll(first, 1, axis=0))
            carry_ref[idx] = first[rows - 1:rows, :]
            hids.append(jax.nn.gelu(prev_first + second + pe_ref[t, 0:1, :]).astype(BF16))
        out = jnp.dot(jnp.concatenate(hids, axis=0), w2_ref[t], preferred_element_type=F32)
        for k in range(kv_heads):
            c0 = (t * kv_heads + k) * HEAD_DIM
            o_ref[0, :, c0:c0 + HEAD_DIM] = out[k * rows:(k + 1) * rows]


def _compress(cache, page_table, w1cat, w2, pe_term, kv_heads, pages_per_step):
    nb, n_pages = page_table.shape
    n_steps = n_pages // pages_per_step
    chunks = n_pages * (PAGE_SIZE // CMP_STRIDE)
    rows_step = pages_per_step * (PAGE_SIZE // CMP_STRIDE)
    width = 2 * kv_heads * HEAD_DIM
    row_layout = cache.ndim == 2

    def page_spec(p):
        if row_layout:
            return pl.BlockSpec((PAGE_SIZE * 4 * kv_heads, HEAD_DIM),
                                lambda b, s, pt: (pt[b * n_pages + s * pages_per_step + p], 0))
        return pl.BlockSpec((1, PAGE_SIZE, width),
                            lambda b, s, pt: (pt[b * n_pages + s * pages_per_step + p], 0, 0))

    grid_spec = pltpu.PrefetchScalarGridSpec(
        num_scalar_prefetch=1,
        grid=(nb, n_steps),
        in_specs=[page_spec(p) for p in range(pages_per_step)] + [
            pl.BlockSpec(w1cat.shape, lambda b, s, pt: (0, 0, 0)),
            pl.BlockSpec(w2.shape, lambda b, s, pt: (0, 0, 0)),
            pl.BlockSpec(pe_term.shape, lambda b, s, pt: (0, 0, 0))],
        out_specs=pl.BlockSpec((1, rows_step, width), lambda b, s, pt: (b, s, 0)),
        scratch_shapes=[pltpu.VMEM((2 * kv_heads, 1, HEAD_DIM), F32),
                        pltpu.VMEM((2 * kv_heads * CMP_STRIDE * _PLANE_PITCH, HEAD_DIM), F32)])
    assert rows_step <= PAGE_SIZE, "a step's chunks must fit one plane"
    body = functools.partial(_compress_body, n_pages_step=pages_per_step, kv_heads=kv_heads,
                             row_layout=row_layout)
    return pl.pallas_call(
        body, grid_spec=grid_spec,
        out_shape=jax.ShapeDtypeStruct((nb, chunks, width), F32),
        compiler_params=_params("parallel", "arbitrary"),
        name="nsa_compress",
    )(page_table.reshape(-1), *([cache] * pages_per_step), w1cat, w2, pe_term)


def _nsa_prompt_body(q_ref, kc_ref, vc_ref, ks_ref, vs_ref, kw_ref, vw_ref, tsel_ref, twin_ref, tcmp_ref,
                     ind_ref, exp_ref, gates_ref, gn_ref, o_ref, mask_ref, *, n_blk, n_pick, n_kt):
    i = pl.program_id(2)
    t = LANES
    rows = NSA_GROUP * t
    q = jnp.concatenate([q_ref[:, g * HEAD_DIM:(g + 1) * HEAD_DIM] for g in range(NSA_GROUP)], axis=0)
    qrow = lax.broadcasted_iota(jnp.int32, (rows, t), 0) & (t - 1)
    col = lax.broadcasted_iota(jnp.int32, (rows, t), 1)
    qpos = i * t + qrow

    lc = lax.dot_general(q, kc_ref[0].astype(BF16), _NT, preferred_element_type=F32) + tcmp_ref[0, 0]
    valid_c = (col >= 1) & (qpos >= CMP_STRIDE * col + (CMP_LEN - CMP_STRIDE - 1))
    pc = _masked_softmax(lc, valid_c)
    o_cmp = jnp.dot(pc.astype(BF16), vc_ref[0].astype(BF16), preferred_element_type=F32)

    pcs = pc[0:t]
    for g in range(1, NSA_GROUP):
        pcs = pcs + pc[g * t:(g + 1) * t]
    p3 = jnp.concatenate(_split3(pcs), axis=1)
    score = lax.dot_general(ind_ref[...], p3, _NT, preferred_element_type=F32)[:n_blk]
    jj = lax.broadcasted_iota(jnp.int32, (n_blk, t), 0)
    qq = i * t + lax.broadcasted_iota(jnp.int32, (n_blk, t), 1)
    qblk = qq // SEL_BLOCK
    forced = (jj == 0) | (jj == qblk) | (jj == qblk - 1)
    score = jnp.where(forced, BIG, jnp.where(jj * SEL_BLOCK <= qq, score, NEG_INF))
    rank = jnp.zeros((n_blk, t), jnp.int32)
    for jp in range(n_blk):
        r = score[jp:jp + 1, :]
        beats = (r > score) | ((r == score) & (jj > jp))
        rank = rank + beats.astype(jnp.int32)
    sel_t = jnp.concatenate([(rank < n_pick).astype(F32), jnp.zeros((t - n_blk, t), F32)], axis=0)
    maskf = jnp.dot(sel_t.T.astype(BF16), exp_ref[...], preferred_element_type=F32)
    for kt in range(n_kt):
        mask_ref[kt] = (maskf[:, kt * t:(kt + 1) * t] - 1.0) * BIG


    two = 2 * t
    qrow2 = lax.broadcasted_iota(jnp.int32, (rows, two), 0) & (t - 1)
    col2 = lax.broadcasted_iota(jnp.int32, (rows, two), 1)

    def sel_logits(jg, diag):
        s = pl.multiple_of(jg * two, two)
        d0 = i - 2 * jg
        bias = jnp.concatenate([tsel_ref[d0, 0], tsel_ref[jnp.maximum(d0 - 1, 0), 0]], axis=1)
        picked = jnp.concatenate([mask_ref[2 * jg], mask_ref[2 * jg + 1]], axis=1)
        lg = (lax.dot_general(q, ks_ref[pl.ds(s, two), :], _NT, preferred_element_type=F32)
              + (bias + jnp.concatenate([picked] * NSA_GROUP, axis=0)))
        if diag:
            lg = jnp.where(jg * two + col2 <= i * t + qrow2, lg, NEG_INF)
        return lg

    def fold(x):
        return jnp.maximum(x[:, :t], x[:, t:])

    jd = i // 2
    m_vec = fold(sel_logits(jd, True))
    m_vec = lax.fori_loop(0, jd, lambda jg, mv: jnp.maximum(mv, fold(sel_logits(jg, False))), m_vec)
    m_b = jnp.broadcast_to(jnp.max(m_vec, axis=-1, keepdims=True), (rows, t))
    m_b2 = jnp.concatenate([m_b, m_b], axis=1)

    def sel_weights(jg, carry, diag):
        p = jnp.exp(sel_logits(jg, diag) - m_b2)
        s = pl.multiple_of(jg * two, two)
        acc = carry[1] + jnp.dot(p.astype(BF16), vs_ref[pl.ds(s, two), :], preferred_element_type=F32)
        return carry[0] + (p[:, :t] + p[:, t:]), acc

    carry = sel_weights(jd, (jnp.zeros((rows, t), F32), jnp.zeros((rows, HEAD_DIM), F32)), True)
    carry = lax.fori_loop(0, jd, lambda jg, c: sel_weights(jg, c, False), carry)
    o_sel = carry[1] / jnp.sum(carry[0], axis=-1, keepdims=True)

    n_back = WINDOW // t
    lgs = []
    starts = []
    for w in range(n_back + 1):
        s = pl.multiple_of(jnp.maximum(i - w, 0) * t, t)
        lg = lax.dot_general(q, kw_ref[pl.ds(s, t), :], _NT, preferred_element_type=F32) + twin_ref[w, 0]
        if w > 0:
            lg = lg + jnp.where(i >= w, 0.0, NEG_INF)
        lgs.append(lg)
        starts.append(s)
    m_vec = lgs[0]
    for lg in lgs[1:]:
        m_vec = jnp.maximum(m_vec, lg)
    m_b = jnp.broadcast_to(jnp.max(m_vec, axis=-1, keepdims=True), (rows, t))
    l_vec = jnp.zeros((rows, t), F32)
    acc = jnp.zeros((rows, HEAD_DIM), F32)
    for lg, s in zip(lgs, starts):
        p = jnp.exp(lg - m_b)
        l_vec = l_vec + p
        acc = acc + jnp.dot(p.astype(BF16), vw_ref[pl.ds(s, t), :], preferred_element_type=F32)
    o_win = acc / jnp.sum(l_vec, axis=-1, keepdims=True)

    gates = gates_ref[0]
    gn = gn_ref[0]
    for g in range(NSA_GROUP):
        sl = slice(g * t, (g + 1) * t)
        o = (gates[:, 3 * g:3 * g + 1] * o_cmp[sl] + gates[:, 3 * g + 1:3 * g + 2] * o_sel[sl]
             + gates[:, 3 * g + 2:3 * g + 3] * o_win[sl])
        o_ref[:, g * HEAD_DIM:(g + 1) * HEAD_DIM] = _rms(o, gn[:, g * HEAD_DIM:(g + 1) * HEAD_DIM]).astype(BF16)


def _sel_constants(n_chunks, n_blk, n_cmp, key_len):
    blk = np.arange(n_blk)
    lo = np.clip((blk * SEL_BLOCK - CMP_LEN) // CMP_STRIDE + 1, 0, n_cmp)
    hi = np.clip(-(-((blk + 1) * SEL_BLOCK) // CMP_STRIDE), 0, n_cmp)
    c = np.arange(n_chunks) - 1
    ind = ((c[None, :] >= lo[:, None]) & (c[None, :] < hi[:, None]) & (c[None, :] >= 0)).astype(np.float32)
    expand = (np.arange(key_len)[None, :] // SEL_BLOCK == blk[:, None]).astype(np.float32)
    return ind, expand


def _nsa_prompt(qns, kcvc, nsakv16, winkv16, tsel, twin, tcmp, gates_k, gain, batch, seq):
    m, nsw = qns.shape
    kv_heads = nsw // (NSA_GROUP * HEAD_DIM)
    t = LANES
    nq = seq // t
    n_chunks = seq // CMP_STRIDE
    assert n_chunks == t, "the compressed branch is tiled as a single 128-column tile"
    n_cmp = (seq - CMP_LEN) // CMP_STRIDE + 1
    n_blk = -(-seq // SEL_BLOCK)
    n_pick = min(N_SEL, n_blk)
    ind, expand = _sel_constants(n_chunks, n_blk, n_cmp, seq)
    ind_pad = np.zeros((t, n_chunks), np.float32)
    ind_pad[:n_blk] = ind
    exp_pad = np.zeros((t, seq), np.float32)
    exp_pad[:n_blk] = expand
    ind3 = jnp.asarray(np.concatenate([ind_pad] * 3, axis=1), BF16)
    gw = NSA_GROUP * HEAD_DIM
    body = functools.partial(_nsa_prompt_body, n_blk=n_blk, n_pick=n_pick, n_kt=nq)
    return pl.pallas_call(
        body,
        grid=(kv_heads, batch, nq),
        in_specs=[pl.BlockSpec((t, gw), lambda k, b, i: (b * nq + i, k)),
                  pl.BlockSpec((1, n_chunks, HEAD_DIM), lambda k, b, i: (b, 0, k)),
                  pl.BlockSpec((1, n_chunks, HEAD_DIM), lambda k, b, i: (b, 0, kv_heads + k)),
                  pl.BlockSpec((seq, HEAD_DIM), lambda k, b, i: (b, 2 * kv_heads + k)),
                  pl.BlockSpec((seq, HEAD_DIM), lambda k, b, i: (b, 3 * kv_heads + k)),
                  pl.BlockSpec((seq, HEAD_DIM), lambda k, b, i: (b, k)),
                  pl.BlockSpec((seq, HEAD_DIM), lambda k, b, i: (b, kv_heads + k)),
                  pl.BlockSpec((nq, 1, gw, t), lambda k, b, i: (0, k, 0, 0)),
                  pl.BlockSpec((twin.shape[0], 1, gw, t), lambda k, b, i: (0, k, 0, 0)),
                  pl.BlockSpec((1, 1, gw, t), lambda k, b, i: (i, k, 0, 0)),
                  pl.BlockSpec((t, 3 * n_chunks), lambda k, b, i: (0, 0)),
                  pl.BlockSpec((t, seq), lambda k, b, i: (0, 0)),
                  pl.BlockSpec((1, t, LANES), lambda k, b, i: (k, b * nq + i, 0)),
                  pl.BlockSpec((1, 1, gw), lambda k, b, i: (k, 0, 0))],
        out_specs=pl.BlockSpec((t, gw), lambda k, b, i: (b * nq + i, k)),
        out_shape=jax.ShapeDtypeStruct((m, nsw), BF16),
        scratch_shapes=[pltpu.VMEM((nq, t, t), F32)],
        compiler_params=_params("parallel", "parallel", "arbitrary"),
        name="nsa_prompt",
    )(qns, kcvc, kcvc, nsakv16, nsakv16, winkv16, winkv16, tsel, twin, tcmp, ind3,
      jnp.asarray(exp_pad, BF16), gates_k, gain.reshape(kv_heads, 1, gw))


def _cmp_decode_body(tab_ref, q_ref, kcvc_ref, ind_ref, ocmp_ref, score_ref, *, past, kv_heads):
    n_chunks = kcvc_ref.shape[1]
    r = lax.broadcasted_iota(jnp.int32, (1, n_chunks), 1)
    rel = past - (CMP_STRIDE * r + (CMP_LEN - CMP_STRIDE - 1))
    valid = (r >= 1) & (rel >= 0)
    score_ref[...] = jnp.zeros_like(score_ref)
    for k in range(kv_heads):
        kc = kcvc_ref[0, :, k * HEAD_DIM:(k + 1) * HEAD_DIM].astype(BF16)
        vc = kcvc_ref[0, :, (kv_heads + k) * HEAD_DIM:(kv_heads + k + 1) * HEAD_DIM].astype(BF16)
        bias = jnp.concatenate([_bias_chain(rel, tab_ref, k * NSA_GROUP + g) for g in range(NSA_GROUP)], axis=0)
        lg = lax.dot_general(q_ref[0, k], kc, _NT, preferred_element_type=F32) + bias
        p = _masked_softmax(lg, valid)
        ocmp_ref[0, k * NSA_GROUP:(k + 1) * NSA_GROUP, :] = jnp.dot(p.astype(BF16), vc, preferred_element_type=F32)
        p3 = jnp.concatenate(_split3(jnp.sum(p, axis=0, keepdims=True)), axis=1)
        score_ref[0, k:k + 1, :] = jnp.dot(p3, ind_ref[...], preferred_element_type=F32)


def _cmp_decode(table, q, kcvc, ind3, past, kv_heads):
    db = q.shape[0]
    heads = kv_heads * NSA_GROUP
    n_chunks = kcvc.shape[1]
    nj = ind3.shape[1]
    body = functools.partial(_cmp_decode_body, past=past, kv_heads=kv_heads)
    return pl.pallas_call(
        body, grid=(db,),
        in_specs=[pl.BlockSpec(memory_space=pltpu.SMEM),
                  pl.BlockSpec((1, kv_heads, NSA_GROUP, HEAD_DIM), lambda b: (b, 0, 0, 0)),
                  pl.BlockSpec((1, n_chunks, kcvc.shape[2]), lambda b: (b, 0, 0)),
                  pl.BlockSpec(ind3.shape, lambda b: (0, 0))],
        out_specs=[pl.BlockSpec((1, heads, HEAD_DIM), lambda b: (b, 0, 0)),
                   pl.BlockSpec((1, SUBLANES, nj), lambda b: (b, 0, 0))],
        out_shape=[jax.ShapeDtypeStruct((db, heads, HEAD_DIM), F32),
                   jax.ShapeDtypeStruct((db, SUBLANES, nj), F32)],
        compiler_params=_params("parallel"),
        name="nsa_cmp_decode",
    )(table, q, kcvc, ind3)


def _topk_body(s_ref, o_ref, *, n_cache_blk, n_pick):
    s = s_ref[...]
    j = lax.broadcasted_iota(jnp.int32, s.shape, 1)
    lane = lax.broadcasted_iota(jnp.int32, o_ref.shape, 1)
    s = jnp.where((j == 0) | (j == n_cache_blk - 1), BIG, s)
    s = jnp.where(j < n_cache_blk, s, -jnp.inf)
    picks = jnp.zeros(o_ref.shape, jnp.int32)
    for t in range(n_pick):
        m = jnp.max(s, axis=-1, keepdims=True)
        idx = jnp.min(jnp.where(s == m, j, 2 ** 30), axis=-1, keepdims=True)
        picks = jnp.where(lane == t, idx, picks)
        s = jnp.where(j == idx, -jnp.inf, s)
    o_ref[...] = picks


def _topk(scores, n_cache_blk, n_pick):
    rows = scores.shape[0]
    return pl.pallas_call(
        functools.partial(_topk_body, n_cache_blk=n_cache_blk, n_pick=n_pick),
        out_shape=jax.ShapeDtypeStruct((rows, LANES), jnp.int32),
        name="nsa_topk",
    )(scores)


def _sel_decode_body(pt_ref, pk_ref, *refs, n_pick, past, w_buf, kv_static):
    blocks = refs[:n_pick]
    (knew_ref, vnew_ref, win_ref, wknew_ref, wvnew_ref, q_ref, gates_ref, ocmp_ref, gn_ref,
     tab_ref, o_ref, ksc, vsc, kwc, vwc) = refs[n_pick:]
    b = pl.program_id(0)
    k = pl.program_id(1)
    kv_heads = kv_static
    q = q_ref[0, 0]
    n_slot = n_pick + 1
    sel_len = n_slot * SEL_BLOCK
    rows_tok = 4 * kv_static

    first64 = lax.broadcasted_iota(jnp.int32, (SEL_BLOCK, HEAD_DIM), 0) == 0
    for n in range(n_pick):
        ksc[n * SEL_BLOCK:(n + 1) * SEL_BLOCK, :] = (
            blocks[n][pl.ds(2 * kv_static + k, SEL_BLOCK, stride=rows_tok), :].astype(BF16))
        vsc[n * SEL_BLOCK:(n + 1) * SEL_BLOCK, :] = (
            blocks[n][pl.ds(3 * kv_static + k, SEL_BLOCK, stride=rows_tok), :].astype(BF16))
    ksc[n_pick * SEL_BLOCK:sel_len, :] = jnp.where(first64, knew_ref[0], 0.0).astype(BF16)
    vsc[n_pick * SEL_BLOCK:sel_len, :] = jnp.where(first64, vnew_ref[0], 0.0).astype(BF16)

    lane = lax.broadcasted_iota(jnp.int32, (1, sel_len), 1)
    slot = lane // SEL_BLOCK
    base = jnp.full((1, sel_len), past, jnp.int32)
    for n in range(n_pick):
        base = jnp.where(slot == n, pk_ref[(b * kv_heads + k) * n_pick + n] * SEL_BLOCK, base)
    rel = past - (base + (lane & (SEL_BLOCK - 1)))
    bias = jnp.concatenate([_bias_chain(rel, tab_ref, k * NSA_GROUP + g) for g in range(NSA_GROUP)], axis=0)
    lg = lax.dot_general(q, ksc[...], _NT, preferred_element_type=F32) + bias
    p = _masked_softmax(lg, rel >= 0)
    o_sel = jnp.dot(p.astype(BF16), vsc[...], preferred_element_type=F32)

    win_len = kwc.shape[0]
    firstw = lax.broadcasted_iota(jnp.int32, (win_len - w_buf, HEAD_DIM), 0) == 0
    kwc[0:w_buf, :] = win_ref[pl.ds(k, w_buf, stride=2 * kv_static), :].astype(BF16)
    vwc[0:w_buf, :] = win_ref[pl.ds(kv_static + k, w_buf, stride=2 * kv_static), :].astype(BF16)
    kwc[w_buf:win_len, :] = jnp.where(firstw, wknew_ref[0], 0.0).astype(BF16)
    vwc[w_buf:win_len, :] = jnp.where(firstw, wvnew_ref[0], 0.0).astype(BF16)
    relw = w_buf - lax.broadcasted_iota(jnp.int32, (1, win_len), 1)
    bias = jnp.concatenate([_bias_chain(relw, tab_ref, k * NSA_GROUP + g) for g in range(NSA_GROUP)], axis=0)
    lg = lax.dot_general(q, kwc[...], _NT, preferred_element_type=F32) + bias
    p = _masked_softmax(lg, (relw >= 0) & (relw < WINDOW))
    o_win = jnp.dot(p.astype(BF16), vwc[...], preferred_element_type=F32)

    o = gates_ref[0, 0, 0] * ocmp_ref[0, 0] + gates_ref[0, 0, 1] * o_sel + gates_ref[0, 0, 2] * o_win
    o_ref[0, 0] = _rms(o, gn_ref[0])


def _sel_decode(page_table, picks, cache_rows, nsa_new, win_rows, win_new, q, gates, ocmp, gain, table,
                n_pick, past, w_buf):
    db, kv_heads = q.shape[:2]
    n_pages = page_table.shape[1]
    halves = PAGE_SIZE // SEL_BLOCK
    rows_tok = 4 * kv_heads

    def pick_spec(n):
        def idx(b, k, pt, pk):
            j = pk[(b * kv_heads + k) * n_pick + n]
            return (pt[b * n_pages + j // halves] * halves + j % halves, 0)
        return pl.BlockSpec((SEL_BLOCK * rows_tok, HEAD_DIM), idx)

    def row_spec(col0):
        return pl.BlockSpec((1, 1, HEAD_DIM), lambda b, k, pt, pk: (b, 0, col0 + k))

    def head_spec(shape):
        nd = len(shape)
        return pl.BlockSpec((1, 1) + shape, lambda b, k, pt, pk: (b, k) + (0,) * nd)

    win_len = w_buf + LANES
    grid_spec = pltpu.PrefetchScalarGridSpec(
        num_scalar_prefetch=2,
        grid=(db, kv_heads),
        in_specs=[pick_spec(n) for n in range(n_pick)]
        + [row_spec(2 * kv_heads), row_spec(3 * kv_heads),
           pl.BlockSpec((w_buf * 2 * kv_heads, HEAD_DIM), lambda b, k, pt, pk: (b, 0)),
           row_spec(0), row_spec(kv_heads),
           head_spec((NSA_GROUP, HEAD_DIM)), head_spec((3, NSA_GROUP, 1)), head_spec((NSA_GROUP, HEAD_DIM)),
           pl.BlockSpec((1, NSA_GROUP, HEAD_DIM), lambda b, k, pt, pk: (k, 0, 0)),
           pl.BlockSpec(memory_space=pltpu.SMEM)],
        out_specs=head_spec((NSA_GROUP, HEAD_DIM)),
        scratch_shapes=[pltpu.VMEM(((n_pick + 1) * SEL_BLOCK, HEAD_DIM), BF16),
                        pltpu.VMEM(((n_pick + 1) * SEL_BLOCK, HEAD_DIM), BF16),
                        pltpu.VMEM((win_len, HEAD_DIM), BF16),
                        pltpu.VMEM((win_len, HEAD_DIM), BF16)])
    body = functools.partial(_sel_decode_body, n_pick=n_pick, past=past, w_buf=w_buf, kv_static=kv_heads)
    return pl.pallas_call(
        body, grid_spec=grid_spec,
        out_shape=jax.ShapeDtypeStruct((db, kv_heads, NSA_GROUP, HEAD_DIM), F32),
        compiler_params=_params("parallel", "parallel"),
        name="nsa_sel_decode",
    )(page_table.reshape(-1), picks, *([cache_rows] * n_pick), nsa_new, nsa_new, win_rows,
      win_new, win_new, q, gates, ocmp, gain.reshape(kv_heads, NSA_GROUP, HEAD_DIM), table)


def _out_proj_body(x_ref, a_ref, b_ref, w_ref, o_ref):
    half = a_ref.shape[1]
    o_ref[...] = (x_ref[...] + jnp.dot(a_ref[...], w_ref[0:half, :], preferred_element_type=F32)
                  + jnp.dot(b_ref[...], w_ref[half:2 * half, :], preferred_element_type=F32))


def _out_proj(x, a, b, w, tm):
    m, d = x.shape
    half = a.shape[1]
    return pl.pallas_call(
        _out_proj_body, grid=(m // tm,),
        in_specs=[pl.BlockSpec((tm, d), lambda i: (i, 0)),
                  pl.BlockSpec((tm, half), lambda i: (i, 0)),
                  pl.BlockSpec((tm, half), lambda i: (i, 0)),
                  _resident(w.shape)],
        out_specs=pl.BlockSpec((tm, d), lambda i: (i, 0)),
        out_shape=jax.ShapeDtypeStruct((m, d), F32),
        compiler_params=_params("parallel"),
        name="out_proj",
    )(x, a, b, w)


def _norm_matmul_body(x_ref, g_ref, w_ref, o_ref):
    o_ref[...] = jnp.dot(_rms(x_ref[...], g_ref[...]).astype(BF16), w_ref[...], preferred_element_type=F32)


def _norm_matmul(x, g, w, tm):
    m, d = x.shape
    n = w.shape[1]
    return pl.pallas_call(
        _norm_matmul_body, grid=(m // tm,),
        in_specs=[pl.BlockSpec((tm, d), lambda i: (i, 0)),
                  pl.BlockSpec((1, d), lambda i: (0, 0)),
                  _resident(w.shape)],
        out_specs=pl.BlockSpec((tm, n), lambda i: (i, 0)),
        out_shape=jax.ShapeDtypeStruct((m, n), F32),
        compiler_params=_params("parallel"),
        name="mem_kv_proj",
    )(x, g.reshape(1, d), w)


def _cross_body(x_ref, g_ref, wq_ref, wo_ref, mem_ref, o_ref, *, scale, n_mem, row_layout):
    x = x_ref[0]
    hb = _rms(x, g_ref[...]).astype(BF16)
    qh = (jnp.dot(hb, wq_ref[...], preferred_element_type=F32) * scale).astype(BF16)
    width = MEM_HEADS * HEAD_DIM
    outs = []
    for h in range(MEM_HEADS):
        if row_layout:
            k = mem_ref[pl.ds(h, n_mem, stride=2 * MEM_HEADS), :].astype(BF16)
            v = mem_ref[pl.ds(MEM_HEADS + h, n_mem, stride=2 * MEM_HEADS), :].astype(BF16)
        else:
            k = mem_ref[0, :, h * HEAD_DIM:(h + 1) * HEAD_DIM].astype(BF16)
            v = mem_ref[0, :, width + h * HEAD_DIM:width + (h + 1) * HEAD_DIM].astype(BF16)
        lg = lax.dot_general(qh[:, h * HEAD_DIM:(h + 1) * HEAD_DIM], k, _NT, preferred_element_type=F32)
        e = jnp.exp(lg - jnp.max(lg, axis=-1, keepdims=True))
        p = e / jnp.sum(e, axis=-1, keepdims=True)
        outs.append(jnp.dot(p.astype(BF16), v, preferred_element_type=F32).astype(BF16))
    o = jnp.concatenate(outs, axis=1)
    o_ref[0] = x + jnp.dot(o, wo_ref[...], preferred_element_type=F32)


def _cross(x3, g, wq, wo, mem, n_mem, tm):
    nb, t, d = x3.shape
    row_layout = mem.ndim == 2
    if row_layout:
        mem_spec = pl.BlockSpec((n_mem * 2 * MEM_HEADS, HEAD_DIM), lambda b, i: (b, 0))
    else:
        mem_spec = pl.BlockSpec((1, n_mem, mem.shape[2]), lambda b, i: (b, 0, 0))
    body = functools.partial(_cross_body, scale=1.0 / math.sqrt(HEAD_DIM), n_mem=n_mem, row_layout=row_layout)
    return pl.pallas_call(
        body, grid=(nb, t // tm),
        in_specs=[pl.BlockSpec((1, tm, d), lambda b, i: (b, i, 0)),
                  pl.BlockSpec((1, d), lambda b, i: (0, 0)),
                  _resident(wq.shape), _resident(wo.shape),
                  mem_spec],
        out_specs=pl.BlockSpec((1, tm, d), lambda b, i: (b, i, 0)),
        out_shape=jax.ShapeDtypeStruct((nb, t, d), F32),
        compiler_params=_params("parallel", "parallel"),
        name="cross_attn",
    )(x3, g.reshape(1, d), wq, wo, mem)


def _mlp_body(x_ref, g_ref, wu_ref, wd_ref, gf_ref, o_ref, h_ref, acc_ref):
    f = pl.program_id(1)

    @pl.when(f == 0)
    def _():
        h_ref[...] = _rms(x_ref[...], g_ref[...]).astype(BF16)
        acc_ref[...] = jnp.zeros_like(acc_ref)

    u = jnp.maximum(jnp.dot(h_ref[...], wu_ref[...], preferred_element_type=F32), 0.0)
    acc_ref[...] += jnp.dot((u * u).astype(BF16), wd_ref[...], preferred_element_type=F32)

    @pl.when(f == pl.num_programs(1) - 1)
    def _():
        o_ref[...] = _rms(x_ref[...] + acc_ref[...], gf_ref[...])


def _mlp_final(x, g, wu, wd, gf, tm, tf):
    m, d = x.shape
    dff = wu.shape[1]
    return pl.pallas_call(
        _mlp_body, grid=(m // tm, dff // tf),
        in_specs=[pl.BlockSpec((tm, d), lambda i, f: (i, 0)),
                  pl.BlockSpec((1, d), lambda i, f: (0, 0)),
                  pl.BlockSpec((d, tf), lambda i, f: (0, f)),
                  pl.BlockSpec((tf, d), lambda i, f: (f, 0)),
                  pl.BlockSpec((1, d), lambda i, f: (0, 0))],
        out_specs=pl.BlockSpec((tm, d), lambda i, f: (i, 0)),
        out_shape=jax.ShapeDtypeStruct((m, d), F32),
        scratch_shapes=[pltpu.VMEM((tm, d), BF16), pltpu.VMEM((tm, d), F32)],
        compiler_params=_params("parallel", "arbitrary"),
        name="mlp_final",
    )(x, g.reshape(1, d), wu, wd, gf.reshape(1, d))


def _w1cat(w1):
    half = CMP_STRIDE * HEAD_DIM
    return jnp.concatenate([w1[:half], w1[half:]], axis=-1).astype(BF16)


def kernel(x_prompt, x_sample, cache_sb_kv, cache_nsa_kv, cache_win_kv, cache_mem_kv, page_table, mem_prompt,
           norm_mix, w_in, norm_sb_out, norm_nsa_out, w_out, rel_bias_table,
           cmp_pe_k, cmp_pe_v, w_cmp_k1, w_cmp_k2, w_cmp_v1, w_cmp_v2,
           norm_cross, norm_mem, w_cross_q, w_mem_kv, w_cross_o,
           norm_ffn, w_up, w_down, norm_final):
    batch, seq, d = x_prompt.shape
    db, dec_seq, _ = x_sample.shape
    assert dec_seq == 1
    depth = w_in.shape[0]
    assert depth == 1, "the final norm is fused into the MLP of the only layer"
    n_pool = cache_sb_kv.shape[1]
    n_pages = page_table.shape[1]
    past = n_pages * PAGE_SIZE
    sb_heads = cache_sb_kv.shape[4]
    kv_heads = cache_nsa_kv.shape[4]
    nsa_heads = kv_heads * NSA_GROUP
    sbw = sb_heads * HEAD_DIM
    nsw = nsa_heads * HEAD_DIM
    kvw = kv_heads * HEAD_DIM
    n_gates = 3 * nsa_heads
    w_buf = cache_win_kv.shape[2]
    win_keep = min(WINDOW, seq)
    n_mem = mem_prompt.shape[1]
    mem_w = MEM_HEADS * HEAD_DIM
    m = batch * seq
    l = 0

    w_in_p = jnp.pad(w_in[l].astype(BF16), ((0, 0), (0, LANES - n_gates)))
    w1cat = jnp.stack([_w1cat(w_cmp_k1[l]), _w1cat(w_cmp_v1[l])])
    w2 = jnp.stack([w_cmp_k2[l], w_cmp_v2[l]]).astype(BF16)
    pe = jnp.stack([cmp_pe_k[l].reshape(1, -1), cmp_pe_v[l].reshape(1, -1)])
    pe_term = _pe_term(jnp.broadcast_to(pe, (2, SUBLANES, pe.shape[-1])), jnp.stack([w_cmp_k1[l], w_cmp_v1[l]]))
    w_out_b = w_out[l].astype(BF16)

    xp = x_prompt.reshape(m, d)
    (qsb, sbkv, sbkv16, qns, nsakv, nsakv16, winkv, winkv16, gates) = _project(xp, norm_mix[l], w_in_p, 256)
    o_sb = _sb_prompt(qsb, sbkv16, norm_sb_out[l], batch, seq)

    pages_seq = seq // PAGE_SIZE
    prompt_pages = jnp.arange(batch * pages_seq, dtype=jnp.int32).reshape(batch, pages_seq)
    kcvc = _compress(nsakv.reshape(batch * pages_seq, PAGE_SIZE, 4 * kvw), prompt_pages, w1cat, w2, pe_term,
                     kv_heads, pages_seq)
    nq = seq // LANES
    tsel = _bias_tiles(rel_bias_table, nq, LANES, 1, 0)
    tcmp = _bias_tiles(rel_bias_table, nq, LANES, CMP_STRIDE, CMP_LEN - CMP_STRIDE - 1)
    gates_k = jnp.pad(gates[:, :n_gates].reshape(m, kv_heads, 3 * NSA_GROUP).transpose(1, 0, 2),
                      ((0, 0), (0, 0), (0, LANES - 3 * NSA_GROUP)))
    twin = _bias_tiles(rel_bias_table, WINDOW // LANES + 1, LANES, 1, 0, window=True)
    o_ns = _nsa_prompt(qns, kcvc, nsakv16, winkv16, tsel, twin, tcmp, gates_k, norm_nsa_out[l], batch, seq)
    xp = _out_proj(xp, o_sb, o_ns, w_out_b, 512)

    sb_kv_prompt = sbkv.reshape(1, batch, seq, 2, sb_heads, HEAD_DIM)
    nsa_kv_prompt = nsakv.reshape(1, batch, seq, 4, kv_heads, HEAD_DIM)
    win_kv_prompt = winkv.reshape(1, batch, seq, 2, kv_heads, HEAD_DIM)[:, :, seq - win_keep:]

    xs = x_sample.reshape(db, d)
    (qsb_s, sbkv_s, _, qns_s, nsakv_s, _, winkv_s, _, gates_s) = _project(xs, norm_mix[l], w_in_p, db)
    o_sb_s = _sb_decode(qsb_s.reshape(db, sb_heads, HEAD_DIM), cache_sb_kv[l].reshape(-1, HEAD_DIM), page_table,
                        norm_sb_out[l], 16)

    nsa_rows = cache_nsa_kv[l].reshape(-1, HEAD_DIM)
    kcvc_s = _compress(nsa_rows, page_table, w1cat, w2, pe_term, kv_heads, 16)
    n_chunks_s = past // CMP_STRIDE
    tk = past + 1
    n_cmp_s = (tk - CMP_LEN) // CMP_STRIDE + 1
    n_blk_s = -(-tk // SEL_BLOCK)
    n_cache_blk = past // SEL_BLOCK
    assert n_blk_s == n_cache_blk + 1 and n_cache_blk >= N_SEL
    nj = -(-n_blk_s // LANES) * LANES
    ind_s, _ = _sel_constants(n_chunks_s, n_blk_s, n_cmp_s, 0)
    ind_s_pad = np.zeros((nj, n_chunks_s), np.float32)
    ind_s_pad[:n_blk_s] = ind_s
    ind3_s = jnp.asarray(np.concatenate([ind_s_pad.T] * 3, axis=0), BF16)
    q_dec = qns_s.reshape(db, kv_heads, NSA_GROUP, HEAD_DIM)
    ocmp_s, scores = _cmp_decode(rel_bias_table, q_dec, kcvc_s, ind3_s, past, kv_heads)
    n_pick_cache = N_SEL - 1
    picks = _topk(scores[:, :kv_heads].reshape(db * kv_heads, nj), n_cache_blk, n_pick_cache)
    picks = picks[:, :n_pick_cache].reshape(-1)
    gates_d = gates_s[:, :n_gates].reshape(db, kv_heads, NSA_GROUP, 3).transpose(0, 1, 3, 2)[..., None]
    o_ns_s = _sel_decode(page_table, picks, nsa_rows, nsakv_s.reshape(db, 1, 4 * kvw),
                         cache_win_kv[l].reshape(-1, HEAD_DIM), winkv_s.reshape(db, 1, 2 * kvw), q_dec, gates_d,
                         ocmp_s.reshape(db, kv_heads, NSA_GROUP, HEAD_DIM), norm_nsa_out[l],
                         rel_bias_table, n_pick_cache, past, w_buf)
    xs = _out_proj(xs, o_sb_s.reshape(db, sbw).astype(BF16), o_ns_s.reshape(db, nsw).astype(BF16), w_out_b, db)

    sb_kv_sample = sbkv_s.reshape(1, db, 1, 2, sb_heads, HEAD_DIM)
    nsa_kv_sample = nsakv_s.reshape(1, db, 1, 4, kv_heads, HEAD_DIM)
    win_new = winkv_s.reshape(db, 1, 2, kv_heads, HEAD_DIM)
    win_kv_sample = jnp.concatenate([cache_win_kv[l], win_new], axis=1)[None, :, 1:]

    mem_kv = _norm_matmul(mem_prompt.reshape(batch * n_mem, d), norm_mem[l], w_mem_kv[l].astype(BF16), n_mem)
    wq = w_cross_q[l].astype(BF16)
    wo = w_cross_o[l].astype(BF16)
    xp = _cross(xp.reshape(batch, seq, d), norm_cross[l], wq, wo, mem_kv.reshape(batch, n_mem, 2 * mem_w),
                n_mem, 512)
    xs = _cross(xs.reshape(db, 1, d), norm_cross[l], wq, wo, cache_mem_kv[l].reshape(-1, HEAD_DIM), n_mem, 1)
    mem_kv_prompt = mem_kv.reshape(1, batch, n_mem, 2, MEM_HEADS, HEAD_DIM)

    wu = w_up[l].astype(BF16)
    wd = w_down[l].astype(BF16)
    y_prompt = _mlp_final(xp.reshape(m, d), norm_ffn[l], wu, wd, norm_final, 512, 1024).reshape(batch, seq, d)
    y_sample = _mlp_final(xs.reshape(db, d), norm_ffn[l], wu, wd, norm_final, db, 1024).reshape(db, 1, d)

    return (y_prompt, y_sample, sb_kv_prompt, sb_kv_sample, nsa_kv_prompt, nsa_kv_sample,
            win_kv_prompt, win_kv_sample, mem_kv_prompt)
```

```python
import functools
import math

import numpy as np
import jax
import jax.numpy as jnp
from jax import lax
from jax.experimental import pallas as pl
from jax.experimental.pallas import tpu as pltpu

HEAD_DIM = 128
NSA_GROUP = 4
MEM_HEADS = 4
PAGE_SIZE = 128
CMP_LEN = 32
CMP_STRIDE = 16
SEL_BLOCK = 64
N_SEL = 16
WINDOW = 512
N_BUCKETS = 32
MAX_DISTANCE = 1024
RMS_EPS = 1e-6
NEG_INF = -1e30
BIG = 1e30
LANES = 128
SUBLANES = 8
VMEM_LIMIT = 56 * 1024 * 1024
_PLANE_PITCH = PAGE_SIZE + SUBLANES

F32 = jnp.float32
BF16 = jnp.bfloat16
_NT = (((1,), (1,)), ((), ()))


def _bucket_thresholds():
    max_exact = N_BUCKETS // 2
    n_log = N_BUCKETS - max_exact
    ratio = MAX_DISTANCE // max_exact
    th = list(range(max_exact)) + [max_exact]
    for m in range(1, n_log):
        n = th[-1]
        while n ** n_log < (max_exact ** n_log) * (ratio ** m):
            n += 1
        th.append(n)
    return tuple(th)


THRESHOLDS = _bucket_thresholds()


def _bias_chain(rel, tab_ref, h):
    v = jnp.full(rel.shape, tab_ref[0, h], F32)
    for b in range(1, N_BUCKETS):
        v = jnp.where(rel >= THRESHOLDS[b], tab_ref[b, h], v)
    return v


def _rms(x, g):
    return x * lax.rsqrt(jnp.mean(x * x, axis=-1, keepdims=True) + RMS_EPS) * g


def _softplus(z):
    return jnp.maximum(z, 0.0) + jnp.log(1.0 + jnp.exp(-jnp.abs(z)))


def _split2(x):
    hi = x.astype(BF16)
    lo = (x - hi.astype(F32)).astype(BF16)
    return hi, lo


def _split3(x):
    hi = x.astype(BF16)
    r = x - hi.astype(F32)
    mid = r.astype(BF16)
    lo = (r - mid.astype(F32)).astype(BF16)
    return hi, mid, lo


def _masked_softmax(lg, valid):
    lgm = jnp.where(valid, lg, NEG_INF)
    m = jnp.max(lgm, axis=-1, keepdims=True)
    e = jnp.where(valid, jnp.exp(lgm - m), 0.0)
    s = jnp.sum(e, axis=-1, keepdims=True)
    return e / jnp.where(s > 0.0, s, 1.0)


def _params(*sem):
    return pltpu.CompilerParams(dimension_semantics=sem, vmem_limit_bytes=VMEM_LIMIT)


def _resident(shape):
    nd = len(shape)
    return pl.BlockSpec(shape, lambda *_: (0,) * nd, pipeline_mode=pl.Buffered(1))


def _proj_body(x_ref, g_ref, w_ref, qsb_ref, sbkv_ref, sbkv16_ref, qns_ref, nsakv_ref,
               nsakv16_ref, winkv_ref, winkv16_ref, gates_ref, *, sbw, nsw, kvw, scale):
    hb = _rms(x_ref[...], g_ref[...]).astype(BF16)

    def mm(c0, n):
        return jnp.dot(hb, w_ref[:, c0:c0 + n], preferred_element_type=F32)

    c = 0
    qsb_ref[...] = (mm(c, sbw) * scale).astype(BF16)
    c += sbw
    for half in range(2):
        kv = mm(c, sbw)
        sbkv_ref[:, half * sbw:(half + 1) * sbw] = kv
        sbkv16_ref[:, half * sbw:(half + 1) * sbw] = kv.astype(BF16)
        c += sbw
    qns_ref[...] = (mm(c, nsw) * scale).astype(BF16)
    c += nsw
    kv = mm(c, 4 * kvw)
    nsakv_ref[...] = kv
    nsakv16_ref[...] = kv.astype(BF16)
    c += 4 * kvw
    kv = mm(c, 2 * kvw)
    winkv_ref[...] = kv
    winkv16_ref[...] = kv.astype(BF16)
    c += 2 * kvw
    gates_ref[...] = jax.nn.sigmoid(mm(c, LANES))


def _project(x, g, w_pad, tm):
    m, d = x.shape
    sbw = nsw = d // 2
    kvw = nsw // NSA_GROUP
    widths = (sbw, 2 * sbw, 2 * sbw, nsw, 4 * kvw, 4 * kvw, 2 * kvw, 2 * kvw, LANES)
    dtypes = (BF16, F32, BF16, BF16, F32, BF16, F32, BF16, F32)
    body = functools.partial(_proj_body, sbw=sbw, nsw=nsw, kvw=kvw, scale=1.0 / math.sqrt(HEAD_DIM))
    return pl.pallas_call(
        body,
        grid=(m // tm,),
        in_specs=[pl.BlockSpec((tm, d), lambda i: (i, 0)),
                  pl.BlockSpec((1, d), lambda i: (0, 0)),
                  _resident(w_pad.shape)],
        out_specs=[pl.BlockSpec((tm, w), lambda i: (i, 0)) for w in widths],
        out_shape=[jax.ShapeDtypeStruct((m, w), dt) for w, dt in zip(widths, dtypes)],
        compiler_params=_params("parallel"),
        name="in_proj",
    )(x, g.reshape(1, d), w_pad)


def _bias_tile_body(tab_ref, o_ref, *, q_stride, k_stride, k_off, window, n_heads):
    m = pl.program_id(0)
    i = lax.broadcasted_iota(jnp.int32, (LANES, LANES), 0)
    j = lax.broadcasted_iota(jnp.int32, (LANES, LANES), 1)
    rel = m * q_stride + i - (k_stride * j + k_off)
    for h in range(n_heads):
        v = _bias_chain(rel, tab_ref, h)
        if window:
            v = jnp.where((rel >= 0) & (rel < WINDOW), v, NEG_INF)
        o_ref[0, h] = v


def _bias_tiles(table, n_tiles, q_stride, k_stride, k_off, window=False):
    n_heads = table.shape[1]
    body = functools.partial(_bias_tile_body, q_stride=q_stride, k_stride=k_stride, k_off=k_off, window=window,
                             n_heads=n_heads)
    out = pl.pallas_call(
        body,
        grid=(n_tiles,),
        in_specs=[pl.BlockSpec(memory_space=pltpu.SMEM)],
        out_specs=pl.BlockSpec((1, n_heads, LANES, LANES), lambda m: (m, 0, 0, 0)),
        out_shape=jax.ShapeDtypeStruct((n_tiles, n_heads, LANES, LANES), F32),
        compiler_params=_params("parallel"),
        name="bias_tiles",
    )(table)
    return out.reshape(n_tiles, n_heads // NSA_GROUP, NSA_GROUP * LANES, LANES)


def _cumsum_rhs(t):
    u = np.tril(np.ones((t, t), np.float32))
    half = np.concatenate([u, np.ones((t, t), np.float32)], axis=1)
    return jnp.asarray(np.concatenate([half, half], axis=0), BF16)


def _sb_tile(q, k, v, uo, carry, acc, valid):
    t = q.shape[0]
    sub = uo.shape[1] // 2
    z = lax.dot_general(q, k, _NT, preferred_element_type=F32)
    sp = _softplus(z)
    if valid is not None:
        sp = jnp.where(valid, sp, 0.0)
    parts = []
    for s in reversed(range(t // sub)):
        hi, lo = _split2(sp[:, s * sub:(s + 1) * sub])
        ct = jnp.dot(jnp.concatenate([hi, lo], axis=1), uo, preferred_element_type=F32)
        parts.insert(0, jnp.exp(z[:, s * sub:(s + 1) * sub] - (ct[:, :sub] + carry)))
        carry = carry + ct[:, sub:]
    a = jnp.concatenate(parts, axis=1)
    if valid is not None:
        a = jnp.where(valid, a, 0.0)
    acc = acc + jnp.dot(a.astype(BF16), v, preferred_element_type=F32)
    return carry, acc


def _sb_prompt_body(q_ref, k_ref, v_ref, g_ref, uo_ref, o_ref, *, tq, hps):
    qi = pl.program_id(2)
    uo = uo_ref[...]
    row = lax.broadcasted_iota(jnp.int32, (tq, tq), 0)
    col = lax.broadcasted_iota(jnp.int32, (tq, tq), 1)
    lanes = [slice(h * HEAD_DIM, (h + 1) * HEAD_DIM) for h in range(hps)]
    qs = [q_ref[:, sl] for sl in lanes]

    def tiles(start, state, valid):
        out = []
        for h, sl in enumerate(lanes):
            out.extend(_sb_tile(qs[h], k_ref[pl.ds(start, tq), sl], v_ref[pl.ds(start, tq), sl], uo,
                                state[2 * h], state[2 * h + 1], valid))
        return tuple(out)

    state = (jnp.zeros((tq, uo.shape[1] // 2), F32), jnp.zeros((tq, HEAD_DIM), F32)) * hps
    state = tiles(pl.multiple_of(qi * tq, tq), state, col < row)
    state = lax.fori_loop(0, qi, lambda t, st: tiles(pl.multiple_of((qi - 1 - t) * tq, tq), st, None), state)
    for h, sl in enumerate(lanes):
        o_ref[:, sl] = _rms(state[2 * h + 1], g_ref[0, :, sl]).astype(BF16)


def _sb_prompt(qsb, sbkv16, gain, batch, seq):
    m, sbw = qsb.shape
    heads = sbw // HEAD_DIM
    tq = 2 * LANES
    hps = 8
    nq = seq // tq
    wide = hps * HEAD_DIM
    groups = heads // hps
    return pl.pallas_call(
        functools.partial(_sb_prompt_body, tq=tq, hps=hps),
        grid=(batch, groups, nq),
        in_specs=[pl.BlockSpec((tq, wide), lambda b, h, i: (b * nq + i, h)),
                  pl.BlockSpec((seq, wide), lambda b, h, i: (b, h)),
                  pl.BlockSpec((seq, wide), lambda b, h, i: (b, groups + h)),
                  pl.BlockSpec((1, 1, wide), lambda b, h, i: (h, 0, 0)),
                  pl.BlockSpec((2 * LANES, 2 * LANES), lambda b, h, i: (0, 0))],
        out_specs=pl.BlockSpec((tq, wide), lambda b, h, i: (b * nq + i, h)),
        out_shape=jax.ShapeDtypeStruct((m, sbw), BF16),
        compiler_params=_params("parallel", "parallel", "arbitrary"),
        name="sb_prompt",
    )(qsb, sbkv16, sbkv16, gain.reshape(groups, 1, wide), _cumsum_rhs(LANES))


def _sb_decode_body(pt_ref, *refs, n_pages_step, n_steps, heads):
    pages = refs[:n_pages_step]
    q_ref, uo_ref, g_ref, o_ref, acc_ref, carry_ref, plane_ref = refs[n_pages_step:]
    s = pl.program_id(1)
    rows_tok = 2 * heads
    pitch = _PLANE_PITCH

    @pl.when(s == 0)
    def _():
        acc_ref[...] = jnp.zeros_like(acc_ref)
        carry_ref[...] = jnp.zeros_like(carry_ref)

    for p in range(n_pages_step):
        for tok in range(PAGE_SIZE):
            for kv in range(2):
                plane_ref[pl.ds((p * rows_tok + kv * heads) * pitch + tok, heads, stride=pitch), :] = (
                    pages[p][pl.ds(tok * rows_tok + kv * heads, heads), :])

    def plane(p, r):
        return plane_ref[pl.ds((p * rows_tok + r) * pitch, PAGE_SIZE), :].astype(BF16)

    q = q_ref[0]
    uo = uo_ref[...]
    rowi = lax.broadcasted_iota(jnp.int32, (heads, HEAD_DIM), 0)
    zs = []
    for p in range(n_pages_step):
        z = jnp.zeros((heads, PAGE_SIZE), F32)
        for h in range(heads):
            z = jnp.where(rowi == h, lax.dot_general(q, plane(p, h), _NT, preferred_element_type=F32), z)
        zs.append(z)
    cts = []
    for z in zs:
        hi, lo = _split2(_softplus(z))
        cts.append(jnp.dot(jnp.concatenate([hi, lo], axis=1), uo, preferred_element_type=F32))
    carry = carry_ref[...]
    acc = acc_ref[...]
    for p in range(n_pages_step):
        a = jnp.exp(zs[p] - (cts[p][:, :PAGE_SIZE] + carry)).astype(BF16)
        carry = carry + cts[p][:, PAGE_SIZE:]
        for h in range(heads):
            acc = acc + jnp.where(rowi == h, jnp.dot(a, plane(p, heads + h), preferred_element_type=F32), 0.0)
    acc_ref[...] = acc
    carry_ref[...] = carry

    @pl.when(s == n_steps - 1)
    def _():
        o_ref[0] = _rms(acc_ref[...], g_ref[...])


def _sb_decode(q, cache_rows, page_table, gain, pages_per_step):
    db, heads, _ = q.shape
    n_pages = page_table.shape[1]
    n_steps = n_pages // pages_per_step
    page_rows = PAGE_SIZE * 2 * heads

    def page_spec(p):
        def idx(b, s, pt):
            return (pt[b * n_pages + n_pages - 1 - (s * pages_per_step + p)], 0)
        return pl.BlockSpec((page_rows, HEAD_DIM), idx)

    grid_spec = pltpu.PrefetchScalarGridSpec(
        num_scalar_prefetch=1,
        grid=(db, n_steps),
        in_specs=[page_spec(p) for p in range(pages_per_step)] + [
            pl.BlockSpec((1, heads, HEAD_DIM), lambda b, s, pt: (b, 0, 0)),
            pl.BlockSpec((2 * PAGE_SIZE, 2 * PAGE_SIZE), lambda b, s, pt: (0, 0)),
            pl.BlockSpec((heads, HEAD_DIM), lambda b, s, pt: (0, 0))],
        out_specs=pl.BlockSpec((1, heads, HEAD_DIM), lambda b, s, pt: (b, 0, 0)),
        scratch_shapes=[pltpu.VMEM((heads, HEAD_DIM), F32), pltpu.VMEM((heads, PAGE_SIZE), F32),
                        pltpu.VMEM((pages_per_step * 2 * heads * _PLANE_PITCH, HEAD_DIM), F32)])
    body = functools.partial(_sb_decode_body, n_pages_step=pages_per_step, n_steps=n_steps, heads=heads)
    return pl.pallas_call(
        body, grid_spec=grid_spec,
        out_shape=jax.ShapeDtypeStruct((db, heads, HEAD_DIM), F32),
        compiler_params=_params("parallel", "arbitrary"),
        name="sb_decode",
    )(page_table.reshape(-1), *([cache_rows] * pages_per_step), q, _cumsum_rhs(PAGE_SIZE),
      gain.reshape(heads, HEAD_DIM))


def _pe_term_body(pe_ref, w_ref, o_ref):
    o_ref[0] = jnp.dot(pe_ref[0].astype(BF16), w_ref[0].astype(BF16), preferred_element_type=F32)


def _pe_term(pe, w1):
    n, _, kdim = pe.shape
    hid = w1.shape[-1]
    return pl.pallas_call(
        _pe_term_body, grid=(n,),
        in_specs=[pl.BlockSpec((1, SUBLANES, kdim), lambda t: (t, 0, 0)),
                  pl.BlockSpec((1, kdim, hid), lambda t: (t, 0, 0))],
        out_specs=pl.BlockSpec((1, SUBLANES, hid), lambda t: (t, 0, 0)),
        out_shape=jax.ShapeDtypeStruct((n, SUBLANES, hid), F32),
        compiler_params=_params("parallel"),
        name="cmp_pe_term",
    )(pe, w1)


def _compress_body(pt_ref, *refs, n_pages_step, kv_heads, row_layout):
    pages = refs[:n_pages_step]
    w1_ref, w2_ref, pe_ref, o_ref, carry_ref, tok_ref = refs[n_pages_step:]
    chunks_page = PAGE_SIZE // CMP_STRIDE
    rows = n_pages_step * chunks_page
    rows_tok = 4 * kv_heads
    pitch = _PLANE_PITCH

    for idx in range(2 * kv_heads):
        for p in range(n_pages_step):
            for v in range(PAGE_SIZE // SUBLANES):
                if row_layout:
                    x = pages[p][pl.ds(v * SUBLANES * rows_tok + idx, SUBLANES, stride=rows_tok), :]
                else:
                    x = pages[p][0, v * SUBLANES:(v + 1) * SUBLANES, idx * HEAD_DIM:(idx + 1) * HEAD_DIM]
                chunk, l0 = divmod(v * SUBLANES, CMP_STRIDE)
                tok_ref[pl.ds((idx * CMP_STRIDE + l0) * pitch + p * chunks_page + chunk, SUBLANES,
                              stride=pitch), :] = x

    @pl.when(pl.program_id(1) == 0)
    def _():
        carry_ref[...] = jnp.zeros_like(carry_ref)

    rowi = lax.broadcasted_iota(jnp.int32, (rows, HEAD_DIM), 0)
    for t in range(2):
        x = jnp.concatenate(
            [jnp.concatenate([tok_ref[pl.ds(((t * kv_heads + k) * CMP_STRIDE + l) * pitch, rows), :].astype(BF16)
                              for l in range(CMP_STRIDE)], axis=1) for k in range(kv_heads)], axis=0)
        acc = jnp.dot(x, w1_ref[t], preferred_element_type=F32)
        hids = []
        for k in range(kv_heads):
            idx = t * kv_heads + k
            first = acc[k * rows:(k + 1) * rows, :HEAD_DIM]
            second = acc[k * rows:(k + 1) * rows, HEAD_DIM:]
            prev_first = jnp.where(rowi == 0, carry_ref[idx], pltpu.roll(first, 1, axis=0))
            carry_ref[idx] = first[rows - 1:rows, :]
            hids.append(jax.nn.gelu(prev_first + second + pe_ref[t, 0:1, :]).astype(BF16))
        out = jnp.dot(jnp.concatenate(hids, axis=0), w2_ref[t], preferred_element_type=F32)
        for k in range(kv_heads):
            c0 = (t * kv_heads + k) * HEAD_DIM
            o_ref[0, :, c0:c0 + HEAD_DIM] = out[k * rows:(k + 1) * rows]


def _compress(cache, page_table, w1cat, w2, pe_term, kv_heads, pages_per_step):
    nb, n_pages = page_table.shape
    n_steps = n_pages // pages_per_step
    chunks = n_pages * (PAGE_SIZE // CMP_STRIDE)
    rows_step = pages_per_step * (PAGE_SIZE // CMP_STRIDE)
    width = 2 * kv_heads * HEAD_DIM
    row_layout = cache.ndim == 2

    def page_spec(p):
        if row_layout:
            return pl.BlockSpec((PAGE_SIZE * 4 * kv_heads, HEAD_DIM),
                                lambda b, s, pt: (pt[b * n_pages + s * pages_per_step + p], 0))
        return pl.BlockSpec((1, PAGE_SIZE, width),
                            lambda b, s, pt: (pt[b * n_pages + s * pages_per_step + p], 0, 0))

    grid_spec = pltpu.PrefetchScalarGridSpec(
        num_scalar_prefetch=1,
        grid=(nb, n_steps),
        in_specs=[page_spec(p) for p in range(pages_per_step)] + [
            pl.BlockSpec(w1cat.shape, lambda b, s, pt: (0, 0, 0)),
            pl.BlockSpec(w2.shape, lambda b, s, pt: (0, 0, 0)),
            pl.BlockSpec(pe_term.shape, lambda b, s, pt: (0, 0, 0))],
        out_specs=pl.BlockSpec((1, rows_step, width), lambda b, s, pt: (b, s, 0)),
        scratch_shapes=[pltpu.VMEM((2 * kv_heads, 1, HEAD_DIM), F32),
                        pltpu.VMEM((2 * kv_heads * CMP_STRIDE * _PLANE_PITCH, HEAD_DIM), F32)])
    assert rows_step <= PAGE_SIZE, "a step's chunks must fit one plane"
    body = functools.partial(_compress_body, n_pages_step=pages_per_step, kv_heads=kv_heads,
                             row_layout=row_layout)
    return pl.pallas_call(
        body, grid_spec=grid_spec,
        out_shape=jax.ShapeDtypeStruct((nb, chunks, width), F32),
        compiler_params=_params("parallel", "arbitrary"),
        name="nsa_compress",
    )(page_table.reshape(-1), *([cache] * pages_per_step), w1cat, w2, pe_term)


def _nsa_prompt_body(q_ref, kc_ref, vc_ref, ks_ref, vs_ref, kw_ref, vw_ref, tsel_ref, twin_ref, tcmp_ref,
                     ind_ref, exp_ref, gates_ref, gn_ref, o_ref, mask_ref, *, n_blk, n_pick, n_kt, nk):
    i = pl.program_id(1)
    t = LANES
    rows = NSA_GROUP * t
    gw = NSA_GROUP * HEAD_DIM
    heads = range(nk)

    def hd(k):
        return slice(k * HEAD_DIM, (k + 1) * HEAD_DIM)

    qs = [jnp.concatenate([q_ref[:, k * gw + g * HEAD_DIM:k * gw + (g + 1) * HEAD_DIM] for g in range(NSA_GROUP)],
                          axis=0) for k in heads]
    qrow = lax.broadcasted_iota(jnp.int32, (rows, t), 0) & (t - 1)
    col = lax.broadcasted_iota(jnp.int32, (rows, t), 1)
    qpos = i * t + qrow
    valid_c = (col >= 1) & (qpos >= CMP_STRIDE * col + (CMP_LEN - CMP_STRIDE - 1))
    jj = lax.broadcasted_iota(jnp.int32, (n_blk, t), 0)
    qq = i * t + lax.broadcasted_iota(jnp.int32, (n_blk, t), 1)
    qblk = qq // SEL_BLOCK
    forced = (jj == 0) | (jj == qblk) | (jj == qblk - 1)
    eligible = jj * SEL_BLOCK <= qq

    o_cmps = []
    for k in heads:
        lc = (lax.dot_general(qs[k], kc_ref[0, :, hd(k)].astype(BF16), _NT, preferred_element_type=F32)
              + tcmp_ref[0, k])
        pc = _masked_softmax(lc, valid_c)
        o_cmps.append(jnp.dot(pc.astype(BF16), vc_ref[0, :, hd(k)].astype(BF16), preferred_element_type=F32))

        pcs = pc[0:t]
        for g in range(1, NSA_GROUP):
            pcs = pcs + pc[g * t:(g + 1) * t]
        p3 = jnp.concatenate(_split3(pcs), axis=1)
        score = lax.dot_general(ind_ref[...], p3, _NT, preferred_element_type=F32)[:n_blk]
        score = jnp.where(forced, BIG, jnp.where(eligible, score, NEG_INF))
        rank = jnp.zeros((n_blk, t), jnp.int32)
        for jp in range(n_blk):
            r = score[jp:jp + 1, :]
            beats = (r > score) | ((r == score) & (jj > jp))
            rank = rank + beats.astype(jnp.int32)
        sel_t = jnp.concatenate([(rank < n_pick).astype(F32), jnp.zeros((t - n_blk, t), F32)], axis=0)
        maskf = jnp.dot(sel_t.T.astype(BF16), exp_ref[...], preferred_element_type=F32)
        for kt in range(n_kt):
            mask_ref[k, kt] = (maskf[:, kt * t:(kt + 1) * t] - 1.0) * BIG


    two = 2 * t
    qrow2 = lax.broadcasted_iota(jnp.int32, (rows, two), 0) & (t - 1)
    col2 = lax.broadcasted_iota(jnp.int32, (rows, two), 1)

    def sel_logits(k, jg, diag):
        s = pl.multiple_of(jg * two, two)
        d0 = i - 2 * jg
        bias = jnp.concatenate([tsel_ref[d0, k], tsel_ref[jnp.maximum(d0 - 1, 0), k]], axis=1)
        picked = jnp.concatenate([mask_ref[k, 2 * jg], mask_ref[k, 2 * jg + 1]], axis=1)
        lg = (lax.dot_general(qs[k], ks_ref[pl.ds(s, two), hd(k)], _NT, preferred_element_type=F32)
              + (bias + jnp.concatenate([picked] * NSA_GROUP, axis=0)))
        if diag:
            lg = jnp.where(jg * two + col2 <= i * t + qrow2, lg, NEG_INF)
        return lg

    def fold(x):
        return jnp.maximum(x[:, :t], x[:, t:])

    jd = i // 2
    m_vecs = tuple(fold(sel_logits(k, jd, True)) for k in heads)
    m_vecs = lax.fori_loop(
        0, jd, lambda jg, mv: tuple(jnp.maximum(mv[k], fold(sel_logits(k, jg, False))) for k in heads), m_vecs)
    m_b2s = []
    for k in heads:
        m_b = jnp.broadcast_to(jnp.max(m_vecs[k], axis=-1, keepdims=True), (rows, t))
        m_b2s.append(jnp.concatenate([m_b, m_b], axis=1))

    def sel_weights(jg, carry, diag):
        out = []
        for k in heads:
            p = jnp.exp(sel_logits(k, jg, diag) - m_b2s[k])
            s = pl.multiple_of(jg * two, two)
            out.append(carry[2 * k] + (p[:, :t] + p[:, t:]))
            out.append(carry[2 * k + 1]
                       + jnp.dot(p.astype(BF16), vs_ref[pl.ds(s, two), hd(k)], preferred_element_type=F32))
        return tuple(out)

    carry = sel_weights(jd, (jnp.zeros((rows, t), F32), jnp.zeros((rows, HEAD_DIM), F32)) * nk, True)
    carry = lax.fori_loop(0, jd, lambda jg, c: sel_weights(jg, c, False), carry)
    o_sels = [carry[2 * k + 1] / jnp.sum(carry[2 * k], axis=-1, keepdims=True) for k in heads]

    n_back = WINDOW // t
    for k in heads:
        lgs = []
        starts = []
        for w in range(n_back + 1):
            s = pl.multiple_of(jnp.maximum(i - w, 0) * t, t)
            lg = (lax.dot_general(qs[k], kw_ref[pl.ds(s, t), hd(k)], _NT, preferred_element_type=F32)
                  + twin_ref[w, k])
            if w > 0:
                lg = lg + jnp.where(i >= w, 0.0, NEG_INF)
            lgs.append(lg)
            starts.append(s)
        m_vec = lgs[0]
        for lg in lgs[1:]:
            m_vec = jnp.maximum(m_vec, lg)
        m_b = jnp.broadcast_to(jnp.max(m_vec, axis=-1, keepdims=True), (rows, t))
        l_vec = jnp.zeros((rows, t), F32)
        acc = jnp.zeros((rows, HEAD_DIM), F32)
        for lg, s in zip(lgs, starts):
            p = jnp.exp(lg - m_b)
            l_vec = l_vec + p
            acc = acc + jnp.dot(p.astype(BF16), vw_ref[pl.ds(s, t), hd(k)], preferred_element_type=F32)
        o_win = acc / jnp.sum(l_vec, axis=-1, keepdims=True)

        gates = gates_ref[k]
        gn = gn_ref[k]
        for g in range(NSA_GROUP):
            sl = slice(g * t, (g + 1) * t)
            o = (gates[:, 3 * g:3 * g + 1] * o_cmps[k][sl] + gates[:, 3 * g + 1:3 * g + 2] * o_sels[k][sl]
                 + gates[:, 3 * g + 2:3 * g + 3] * o_win[sl])
            o_ref[:, k * gw + g * HEAD_DIM:k * gw + (g + 1) * HEAD_DIM] = (
                _rms(o, gn[:, g * HEAD_DIM:(g + 1) * HEAD_DIM]).astype(BF16))


def _sel_constants(n_chunks, n_blk, n_cmp, key_len):
    blk = np.arange(n_blk)
    lo = np.clip((blk * SEL_BLOCK - CMP_LEN) // CMP_STRIDE + 1, 0, n_cmp)
    hi = np.clip(-(-((blk + 1) * SEL_BLOCK) // CMP_STRIDE), 0, n_cmp)
    c = np.arange(n_chunks) - 1
    ind = ((c[None, :] >= lo[:, None]) & (c[None, :] < hi[:, None]) & (c[None, :] >= 0)).astype(np.float32)
    expand = (np.arange(key_len)[None, :] // SEL_BLOCK == blk[:, None]).astype(np.float32)
    return ind, expand


def _nsa_prompt(qns, kcvc, nsakv16, winkv16, tsel, twin, tcmp, gates_k, gain, batch, seq):
    m, nsw = qns.shape
    kv_heads = nsw // (NSA_GROUP * HEAD_DIM)
    t = LANES
    nq = seq // t
    n_chunks = seq // CMP_STRIDE
    assert n_chunks == t, "the compressed branch is tiled as a single 128-column tile"
    n_cmp = (seq - CMP_LEN) // CMP_STRIDE + 1
    n_blk = -(-seq // SEL_BLOCK)
    n_pick = min(N_SEL, n_blk)
    ind, expand = _sel_constants(n_chunks, n_blk, n_cmp, seq)
    ind_pad = np.zeros((t, n_chunks), np.float32)
    ind_pad[:n_blk] = ind
    exp_pad = np.zeros((t, seq), np.float32)
    exp_pad[:n_blk] = expand
    ind3 = jnp.asarray(np.concatenate([ind_pad] * 3, axis=1), BF16)
    gw = NSA_GROUP * HEAD_DIM
    nk = kv_heads
    kw = nk * HEAD_DIM
    body = functools.partial(_nsa_prompt_body, n_blk=n_blk, n_pick=n_pick, n_kt=nq, nk=nk)
    return pl.pallas_call(
        body,
        grid=(batch, nq),
        in_specs=[pl.BlockSpec((t, nk * gw), lambda b, i: (b * nq + i, 0)),
                  pl.BlockSpec((1, n_chunks, kw), lambda b, i: (b, 0, 0)),
                  pl.BlockSpec((1, n_chunks, kw), lambda b, i: (b, 0, 1)),
                  pl.BlockSpec((seq, kw), lambda b, i: (b, 2)),
                  pl.BlockSpec((seq, kw), lambda b, i: (b, 3)),
                  pl.BlockSpec((seq, kw), lambda b, i: (b, 0)),
                  pl.BlockSpec((seq, kw), lambda b, i: (b, 1)),
                  pl.BlockSpec((nq, nk, gw, t), lambda b, i: (0, 0, 0, 0)),
                  pl.BlockSpec((twin.shape[0], nk, gw, t), lambda b, i: (0, 0, 0, 0)),
                  pl.BlockSpec((1, nk, gw, t), lambda b, i: (i, 0, 0, 0)),
                  pl.BlockSpec((t, 3 * n_chunks), lambda b, i: (0, 0)),
                  pl.BlockSpec((t, seq), lambda b, i: (0, 0)),
                  pl.BlockSpec((nk, t, LANES), lambda b, i: (0, b * nq + i, 0)),
                  pl.BlockSpec((nk, 1, gw), lambda b, i: (0, 0, 0))],
        out_specs=pl.BlockSpec((t, nk * gw), lambda b, i: (b * nq + i, 0)),
        out_shape=jax.ShapeDtypeStruct((m, nsw), BF16),
        scratch_shapes=[pltpu.VMEM((nk, nq, t, t), F32)],
        compiler_params=_params("parallel", "arbitrary"),
        name="nsa_prompt",
    )(qns, kcvc, kcvc, nsakv16, nsakv16, winkv16, winkv16, tsel, twin, tcmp, ind3,
      jnp.asarray(exp_pad, BF16), gates_k, gain.reshape(kv_heads, 1, gw))


def _cmp_decode_body(tab_ref, q_ref, kcvc_ref, ind_ref, ocmp_ref, score_ref, *, past, kv_heads):
    n_chunks = kcvc_ref.shape[1]
    r = lax.broadcasted_iota(jnp.int32, (1, n_chunks), 1)
    rel = past - (CMP_STRIDE * r + (CMP_LEN - CMP_STRIDE - 1))
    valid = (r >= 1) & (rel >= 0)
    score_ref[...] = jnp.zeros_like(score_ref)
    for k in range(kv_heads):
        kc = kcvc_ref[0, :, k * HEAD_DIM:(k + 1) * HEAD_DIM].astype(BF16)
        vc = kcvc_ref[0, :, (kv_heads + k) * HEAD_DIM:(kv_heads + k + 1) * HEAD_DIM].astype(BF16)
        bias = jnp.concatenate([_bias_chain(rel, tab_ref, k * NSA_GROUP + g) for g in range(NSA_GROUP)], axis=0)
        lg = lax.dot_general(q_ref[0, k], kc, _NT, preferred_element_type=F32) + bias
        p = _masked_softmax(lg, valid)
        ocmp_ref[0, k * NSA_GROUP:(k + 1) * NSA_GROUP, :] = jnp.dot(p.astype(BF16), vc, preferred_element_type=F32)
        p3 = jnp.concatenate(_split3(jnp.sum(p, axis=0, keepdims=True)), axis=1)
        score_ref[0, k:k + 1, :] = jnp.dot(p3, ind_ref[...], preferred_element_type=F32)


def _cmp_decode(table, q, kcvc, ind3, past, kv_heads):
    db = q.shape[0]
    heads = kv_heads * NSA_GROUP
    n_chunks = kcvc.shape[1]
    nj = ind3.shape[1]
    body = functools.partial(_cmp_decode_body, past=past, kv_heads=kv_heads)
    return pl.pallas_call(
        body, grid=(db,),
        in_specs=[pl.BlockSpec(memory_space=pltpu.SMEM),
                  pl.BlockSpec((1, kv_heads, NSA_GROUP, HEAD_DIM), lambda b: (b, 0, 0, 0)),
                  pl.BlockSpec((1, n_chunks, kcvc.shape[2]), lambda b: (b, 0, 0)),
                  pl.BlockSpec(ind3.shape, lambda b: (0, 0))],
        out_specs=[pl.BlockSpec((1, heads, HEAD_DIM), lambda b: (b, 0, 0)),
                   pl.BlockSpec((1, SUBLANES, nj), lambda b: (b, 0, 0))],
        out_shape=[jax.ShapeDtypeStruct((db, heads, HEAD_DIM), F32),
                   jax.ShapeDtypeStruct((db, SUBLANES, nj), F32)],
        compiler_params=_params("parallel"),
        name="nsa_cmp_decode",
    )(table, q, kcvc, ind3)


def _topk_body(s_ref, o_ref, *, n_cache_blk, n_pick):
    s = s_ref[...]
    j = lax.broadcasted_iota(jnp.int32, s.shape, 1)
    lane = lax.broadcasted_iota(jnp.int32, o_ref.shape, 1)
    s = jnp.where((j == 0) | (j == n_cache_blk - 1), BIG, s)
    s = jnp.where(j < n_cache_blk, s, -jnp.inf)
    picks = jnp.zeros(o_ref.shape, jnp.int32)
    for t in range(n_pick):
        m = jnp.max(s, axis=-1, keepdims=True)
        idx = jnp.min(jnp.where(s == m, j, 2 ** 30), axis=-1, keepdims=True)
        picks = jnp.where(lane == t, idx, picks)
        s = jnp.where(j == idx, -jnp.inf, s)
    o_ref[...] = picks


def _topk(scores, n_cache_blk, n_pick):
    rows = scores.shape[0]
    return pl.pallas_call(
        functools.partial(_topk_body, n_cache_blk=n_cache_blk, n_pick=n_pick),
        out_shape=jax.ShapeDtypeStruct((rows, LANES), jnp.int32),
        name="nsa_topk",
    )(scores)


def _sel_decode_body(pt_ref, pk_ref, *refs, n_pick, past, w_buf, kv_static):
    blocks = refs[:n_pick]
    (knew_ref, vnew_ref, win_ref, wknew_ref, wvnew_ref, q_ref, gates_ref, ocmp_ref, gn_ref,
     tab_ref, o_ref, ksc, vsc, kwc, vwc) = refs[n_pick:]
    b = pl.program_id(0)
    k = pl.program_id(1)
    kv_heads = kv_static
    q = q_ref[0, 0]
    n_slot = n_pick + 1
    sel_len = n_slot * SEL_BLOCK
    rows_tok = 4 * kv_static

    first64 = lax.broadcasted_iota(jnp.int32, (SEL_BLOCK, HEAD_DIM), 0) == 0
    for n in range(n_pick):
        ksc[n * SEL_BLOCK:(n + 1) * SEL_BLOCK, :] = (
            blocks[n][pl.ds(2 * kv_static + k, SEL_BLOCK, stride=rows_tok), :].astype(BF16))
        vsc[n * SEL_BLOCK:(n + 1) * SEL_BLOCK, :] = (
            blocks[n][pl.ds(3 * kv_static + k, SEL_BLOCK, stride=rows_tok), :].astype(BF16))
    ksc[n_pick * SEL_BLOCK:sel_len, :] = jnp.where(first64, knew_ref[0], 0.0).astype(BF16)
    vsc[n_pick * SEL_BLOCK:sel_len, :] = jnp.where(first64, vnew_ref[0], 0.0).astype(BF16)

    lane = lax.broadcasted_iota(jnp.int32, (1, sel_len), 1)
    slot = lane // SEL_BLOCK
    base = jnp.full((1, sel_len), past, jnp.int32)
    for n in range(n_pick):
        base = jnp.where(slot == n, pk_ref[(b * kv_heads + k) * n_pick + n] * SEL_BLOCK, base)
    rel = past - (base + (lane & (SEL_BLOCK - 1)))
    bias = jnp.concatenate([_bias_chain(rel, tab_ref, k * NSA_GROUP + g) for g in range(NSA_GROUP)], axis=0)
    lg = lax.dot_general(q, ksc[...], _NT, preferred_element_type=F32) + bias
    p = _masked_softmax(lg, rel >= 0)
    o_sel = jnp.dot(p.astype(BF16), vsc[...], preferred_element_type=F32)

    win_len = kwc.shape[0]
    firstw = lax.broadcasted_iota(jnp.int32, (win_len - w_buf, HEAD_DIM), 0) == 0
    kwc[0:w_buf, :] = win_ref[pl.ds(k, w_buf, stride=2 * kv_static), :].astype(BF16)
    vwc[0:w_buf, :] = win_ref[pl.ds(kv_static + k, w_buf, stride=2 * kv_static), :].astype(BF16)
    kwc[w_buf:win_len, :] = jnp.where(firstw, wknew_ref[0], 0.0).astype(BF16)
    vwc[w_buf:win_len, :] = jnp.where(firstw, wvnew_ref[0], 0.0).astype(BF16)
    relw = w_buf - lax.broadcasted_iota(jnp.int32, (1, win_len), 1)
    bias = jnp.concatenate([_bias_chain(relw, tab_ref, k * NSA_GROUP + g) for g in range(NSA_GROUP)], axis=0)
    lg = lax.dot_general(q, kwc[...], _NT, preferred_element_type=F32) + bias
    p = _masked_softmax(lg, (relw >= 0) & (relw < WINDOW))
    o_win = jnp.dot(p.astype(BF16), vwc[...], preferred_element_type=F32)

    o = gates_ref[0, 0, 0] * ocmp_ref[0, 0] + gates_ref[0, 0, 1] * o_sel + gates_ref[0, 0, 2] * o_win
    o_ref[0, 0] = _rms(o, gn_ref[0])


def _sel_decode(page_table, picks, cache_rows, nsa_new, win_rows, win_new, q, gates, ocmp, gain, table,
                n_pick, past, w_buf):
    db, kv_heads = q.shape[:2]
    n_pages = page_table.shape[1]
    halves = PAGE_SIZE // SEL_BLOCK
    rows_tok = 4 * kv_heads

    def pick_spec(n):
        def idx(b, k, pt, pk):
            j = pk[(b * kv_heads + k) * n_pick + n]
            return (pt[b * n_pages + j // halves] * halves + j % halves, 0)
        return pl.BlockSpec((SEL_BLOCK * rows_tok, HEAD_DIM), idx)

    def row_spec(col0):
        return pl.BlockSpec((1, 1, HEAD_DIM), lambda b, k, pt, pk: (b, 0, col0 + k))

    def head_spec(shape):
        nd = len(shape)
        return pl.BlockSpec((1, 1) + shape, lambda b, k, pt, pk: (b, k) + (0,) * nd)

    win_len = w_buf + LANES
    grid_spec = pltpu.PrefetchScalarGridSpec(
        num_scalar_prefetch=2,
        grid=(db, kv_heads),
        in_specs=[pick_spec(n) for n in range(n_pick)]
        + [row_spec(2 * kv_heads), row_spec(3 * kv_heads),
           pl.BlockSpec((w_buf * 2 * kv_heads, HEAD_DIM), lambda b, k, pt, pk: (b, 0)),
           row_spec(0), row_spec(kv_heads),
           head_spec((NSA_GROUP, HEAD_DIM)), head_spec((3, NSA_GROUP, 1)), head_spec((NSA_GROUP, HEAD_DIM)),
           pl.BlockSpec((1, NSA_GROUP, HEAD_DIM), lambda b, k, pt, pk: (k, 0, 0)),
           pl.BlockSpec(memory_space=pltpu.SMEM)],
        out_specs=head_spec((NSA_GROUP, HEAD_DIM)),
        scratch_shapes=[pltpu.VMEM(((n_pick + 1) * SEL_BLOCK, HEAD_DIM), BF16),
                        pltpu.VMEM(((n_pick + 1) * SEL_BLOCK, HEAD_DIM), BF16),
                        pltpu.VMEM((win_len, HEAD_DIM), BF16),
                        pltpu.VMEM((win_len, HEAD_DIM), BF16)])
    body = functools.partial(_sel_decode_body, n_pick=n_pick, past=past, w_buf=w_buf, kv_static=kv_heads)
    return pl.pallas_call(
        body, grid_spec=grid_spec,
        out_shape=jax.ShapeDtypeStruct((db, kv_heads, NSA_GROUP, HEAD_DIM), F32),
        compiler_params=_params("parallel", "parallel"),
        name="nsa_sel_decode",
    )(page_table.reshape(-1), picks, *([cache_rows] * n_pick), nsa_new, nsa_new, win_rows,
      win_new, win_new, q, gates, ocmp, gain.reshape(kv_heads, NSA_GROUP, HEAD_DIM), table)


def _out_proj_body(x_ref, a_ref, b_ref, w_ref, o_ref):
    half = a_ref.shape[1]
    o_ref[...] = (x_ref[...] + jnp.dot(a_ref[...], w_ref[0:half, :], preferred_element_type=F32)
                  + jnp.dot(b_ref[...], w_ref[half:2 * half, :], preferred_element_type=F32))


def _out_proj(x, a, b, w, tm):
    m, d = x.shape
    half = a.shape[1]
    return pl.pallas_call(
        _out_proj_body, grid=(m // tm,),
        in_specs=[pl.BlockSpec((tm, d), lambda i: (i, 0)),
                  pl.BlockSpec((tm, half), lambda i: (i, 0)),
                  pl.BlockSpec((tm, half), lambda i: (i, 0)),
                  _resident(w.shape)],
        out_specs=pl.BlockSpec((tm, d), lambda i: (i, 0)),
        out_shape=jax.ShapeDtypeStruct((m, d), F32),
        compiler_params=_params("parallel"),
        name="out_proj",
    )(x, a, b, w)


def _norm_matmul_body(x_ref, g_ref, w_ref, o_ref):
    o_ref[...] = jnp.dot(_rms(x_ref[...], g_ref[...]).astype(BF16), w_ref[...], preferred_element_type=F32)


def _norm_matmul(x, g, w, tm):
    m, d = x.shape
    n = w.shape[1]
    return pl.pallas_call(
        _norm_matmul_body, grid=(m // tm,),
        in_specs=[pl.BlockSpec((tm, d), lambda i: (i, 0)),
                  pl.BlockSpec((1, d), lambda i: (0, 0)),
                  _resident(w.shape)],
        out_specs=pl.BlockSpec((tm, n), lambda i: (i, 0)),
        out_shape=jax.ShapeDtypeStruct((m, n), F32),
        compiler_params=_params("parallel"),
        name="mem_kv_proj",
    )(x, g.reshape(1, d), w)


def _cross_body(x_ref, g_ref, wq_ref, wo_ref, mem_ref, o_ref, *, scale, n_mem, row_layout):
    x = x_ref[0]
    hb = _rms(x, g_ref[...]).astype(BF16)
    qh = (jnp.dot(hb, wq_ref[...], preferred_element_type=F32) * scale).astype(BF16)
    width = MEM_HEADS * HEAD_DIM
    outs = []
    for h in range(MEM_HEADS):
        if row_layout:
            k = mem_ref[pl.ds(h, n_mem, stride=2 * MEM_HEADS), :].astype(BF16)
            v = mem_ref[pl.ds(MEM_HEADS + h, n_mem, stride=2 * MEM_HEADS), :].astype(BF16)
        else:
            k = mem_ref[0, :, h * HEAD_DIM:(h + 1) * HEAD_DIM].astype(BF16)
            v = mem_ref[0, :, width + h * HEAD_DIM:width + (h + 1) * HEAD_DIM].astype(BF16)
        lg = lax.dot_general(qh[:, h * HEAD_DIM:(h + 1) * HEAD_DIM], k, _NT, preferred_element_type=F32)
        e = jnp.exp(lg - jnp.max(lg, axis=-1, keepdims=True))
        p = e / jnp.sum(e, axis=-1, keepdims=True)
        outs.append(jnp.dot(p.astype(BF16), v, preferred_element_type=F32).astype(BF16))
    o = jnp.concatenate(outs, axis=1)
    o_ref[0] = x + jnp.dot(o, wo_ref[...], preferred_element_type=F32)


def _cross(x3, g, wq, wo, mem, n_mem, tm):
    nb, t, d = x3.shape
    row_layout = mem.ndim == 2
    if row_layout:
        mem_spec = pl.BlockSpec((n_mem * 2 * MEM_HEADS, HEAD_DIM), lambda b, i: (b, 0))
    else:
        mem_spec = pl.BlockSpec((1, n_mem, mem.shape[2]), lambda b, i: (b, 0, 0))
    body = functools.partial(_cross_body, scale=1.0 / math.sqrt(HEAD_DIM), n_mem=n_mem, row_layout=row_layout)
    return pl.pallas_call(
        body, grid=(nb, t // tm),
        in_specs=[pl.BlockSpec((1, tm, d), lambda b, i: (b, i, 0)),
                  pl.BlockSpec((1, d), lambda b, i: (0, 0)),
                  _resident(wq.shape), _resident(wo.shape),
                  mem_spec],
        out_specs=pl.BlockSpec((1, tm, d), lambda b, i: (b, i, 0)),
        out_shape=jax.ShapeDtypeStruct((nb, t, d), F32),
        compiler_params=_params("parallel", "parallel"),
        name="cross_attn",
    )(x3, g.reshape(1, d), wq, wo, mem)


def _mlp_body(x_ref, g_ref, wu_ref, wd_ref, gf_ref, o_ref, h_ref, acc_ref):
    f = pl.program_id(1)

    @pl.when(f == 0)
    def _():
        h_ref[...] = _rms(x_ref[...], g_ref[...]).astype(BF16)
        acc_ref[...] = jnp.zeros_like(acc_ref)

    u = jnp.maximum(jnp.dot(h_ref[...], wu_ref[...], preferred_element_type=F32), 0.0)
    acc_ref[...] += jnp.dot((u * u).astype(BF16), wd_ref[...], preferred_element_type=F32)

    @pl.when(f == pl.num_programs(1) - 1)
    def _():
        o_ref[...] = _rms(x_ref[...] + acc_ref[...], gf_ref[...])


def _mlp_final(x, g, wu, wd, gf, tm, tf):
    m, d = x.shape
    dff = wu.shape[1]
    return pl.pallas_call(
        _mlp_body, grid=(m // tm, dff // tf),
        in_specs=[pl.BlockSpec((tm, d), lambda i, f: (i, 0)),
                  pl.BlockSpec((1, d), lambda i, f: (0, 0)),
                  pl.BlockSpec((d, tf), lambda i, f: (0, f)),
                  pl.BlockSpec((tf, d), lambda i, f: (f, 0)),
                  pl.BlockSpec((1, d), lambda i, f: (0, 0))],
        out_specs=pl.BlockSpec((tm, d), lambda i, f: (i, 0)),
        out_shape=jax.ShapeDtypeStruct((m, d), F32),
        scratch_shapes=[pltpu.VMEM((tm, d), BF16), pltpu.VMEM((tm, d), F32)],
        compiler_params=_params("parallel", "arbitrary"),
        name="mlp_final",
    )(x, g.reshape(1, d), wu, wd, gf.reshape(1, d))


def _w1cat(w1):
    half = CMP_STRIDE * HEAD_DIM
    return jnp.concatenate([w1[:half], w1[half:]], axis=-1).astype(BF16)


def kernel(x_prompt, x_sample, cache_sb_kv, cache_nsa_kv, cache_win_kv, cache_mem_kv, page_table, mem_prompt,
           norm_mix, w_in, norm_sb_out, norm_nsa_out, w_out, rel_bias_table,
           cmp_pe_k, cmp_pe_v, w_cmp_k1, w_cmp_k2, w_cmp_v1, w_cmp_v2,
           norm_cross, norm_mem, w_cross_q, w_mem_kv, w_cross_o,
           norm_ffn, w_up, w_down, norm_final):
    batch, seq, d = x_prompt.shape
    db, dec_seq, _ = x_sample.shape
    assert dec_seq == 1
    depth = w_in.shape[0]
    assert depth == 1, "the final norm is fused into the MLP of the only layer"
    n_pool = cache_sb_kv.shape[1]
    n_pages = page_table.shape[1]
    past = n_pages * PAGE_SIZE
    sb_heads = cache_sb_kv.shape[4]
    kv_heads = cache_nsa_kv.shape[4]
    nsa_heads = kv_heads * NSA_GROUP
    sbw = sb_heads * HEAD_DIM
    nsw = nsa_heads * HEAD_DIM
    kvw = kv_heads * HEAD_DIM
    n_gates = 3 * nsa_heads
    w_buf = cache_win_kv.shape[2]
    win_keep = min(WINDOW, seq)
    n_mem = mem_prompt.shape[1]
    mem_w = MEM_HEADS * HEAD_DIM
    m = batch * seq
    l = 0

    w_in_p = jnp.pad(w_in[l].astype(BF16), ((0, 0), (0, LANES - n_gates)))
    w1cat = jnp.stack([_w1cat(w_cmp_k1[l]), _w1cat(w_cmp_v1[l])])
    w2 = jnp.stack([w_cmp_k2[l], w_cmp_v2[l]]).astype(BF16)
    pe = jnp.stack([cmp_pe_k[l].reshape(1, -1), cmp_pe_v[l].reshape(1, -1)])
    pe_term = _pe_term(jnp.broadcast_to(pe, (2, SUBLANES, pe.shape[-1])), jnp.stack([w_cmp_k1[l], w_cmp_v1[l]]))
    w_out_b = w_out[l].astype(BF16)

    xp = x_prompt.reshape(m, d)
    (qsb, sbkv, sbkv16, qns, nsakv, nsakv16, winkv, winkv16, gates) = _project(xp, norm_mix[l], w_in_p, 256)
    o_sb = _sb_prompt(qsb, sbkv16, norm_sb_out[l], batch, seq)

    pages_seq = seq // PAGE_SIZE
    prompt_pages = jnp.arange(batch * pages_seq, dtype=jnp.int32).reshape(batch, pages_seq)
    kcvc = _compress(nsakv.reshape(batch * pages_seq, PAGE_SIZE, 4 * kvw), prompt_pages, w1cat, w2, pe_term,
                     kv_heads, pages_seq)
    nq = seq // LANES
    tsel = _bias_tiles(rel_bias_table, nq, LANES, 1, 0)
    tcmp = _bias_tiles(rel_bias_table, nq, LANES, CMP_STRIDE, CMP_LEN - CMP_STRIDE - 1)
    gates_k = jnp.pad(gates[:, :n_gates].reshape(m, kv_heads, 3 * NSA_GROUP).transpose(1, 0, 2),
                      ((0, 0), (0, 0), (0, LANES - 3 * NSA_GROUP)))
    twin = _bias_tiles(rel_bias_table, WINDOW // LANES + 1, LANES, 1, 0, window=True)
    o_ns = _nsa_prompt(qns, kcvc, nsakv16, winkv16, tsel, twin, tcmp, gates_k, norm_nsa_out[l], batch, seq)
    xp = _out_proj(xp, o_sb, o_ns, w_out_b, 512)

    sb_kv_prompt = sbkv.reshape(1, batch, seq, 2, sb_heads, HEAD_DIM)
    nsa_kv_prompt = nsakv.reshape(1, batch, seq, 4, kv_heads, HEAD_DIM)
    win_kv_prompt = winkv.reshape(1, batch, seq, 2, kv_heads, HEAD_DIM)[:, :, seq - win_keep:]

    xs = x_sample.reshape(db, d)
    (qsb_s, sbkv_s, _, qns_s, nsakv_s, _, winkv_s, _, gates_s) = _project(xs, norm_mix[l], w_in_p, db)
    o_sb_s = _sb_decode(qsb_s.reshape(db, sb_heads, HEAD_DIM), cache_sb_kv[l].reshape(-1, HEAD_DIM), page_table,
                        norm_sb_out[l], 16)

    nsa_rows = cache_nsa_kv[l].reshape(-1, HEAD_DIM)
    kcvc_s = _compress(nsa_rows, page_table, w1cat, w2, pe_term, kv_heads, 16)
    n_chunks_s = past // CMP_STRIDE
    tk = past + 1
    n_cmp_s = (tk - CMP_LEN) // CMP_STRIDE + 1
    n_blk_s = -(-tk // SEL_BLOCK)
    n_cache_blk = past // SEL_BLOCK
    assert n_blk_s == n_cache_blk + 1 and n_cache_blk >= N_SEL
    nj = -(-n_blk_s // LANES) * LANES
    ind_s, _ = _sel_constants(n_chunks_s, n_blk_s, n_cmp_s, 0)
    ind_s_pad = np.zeros((nj, n_chunks_s), np.float32)
    ind_s_pad[:n_blk_s] = ind_s
    ind3_s = jnp.asarray(np.concatenate([ind_s_pad.T] * 3, axis=0), BF16)
    q_dec = qns_s.reshape(db, kv_heads, NSA_GROUP, HEAD_DIM)
    ocmp_s, scores = _cmp_decode(rel_bias_table, q_dec, kcvc_s, ind3_s, past, kv_heads)
    n_pick_cache = N_SEL - 1
    picks = _topk(scores[:, :kv_heads].reshape(db * kv_heads, nj), n_cache_blk, n_pick_cache)
    picks = picks[:, :n_pick_cache].reshape(-1)
    gates_d = gates_s[:, :n_gates].reshape(db, kv_heads, NSA_GROUP, 3).transpose(0, 1, 3, 2)[..., None]
    o_ns_s = _sel_decode(page_table, picks, nsa_rows, nsakv_s.reshape(db, 1, 4 * kvw),
                         cache_win_kv[l].reshape(-1, HEAD_DIM), winkv_s.reshape(db, 1, 2 * kvw), q_dec, gates_d,
                         ocmp_s.reshape(db, kv_heads, NSA_GROUP, HEAD_DIM), norm_nsa_out[l],
                         rel_bias_table, n_pick_cache, past, w_buf)
    xs = _out_proj(xs, o_sb_s.reshape(db, sbw).astype(BF16), o_ns_s.reshape(db, nsw).astype(BF16), w_out_b, db)

    sb_kv_sample = sbkv_s.reshape(1, db, 1, 2, sb_heads, HEAD_DIM)
    nsa_kv_sample = nsakv_s.reshape(1, db, 1, 4, kv_heads, HEAD_DIM)
    win_new = winkv_s.reshape(db, 1, 2, kv_heads, HEAD_DIM)
    win_kv_sample = jnp.concatenate([cache_win_kv[l], win_new], axis=1)[None, :, 1:]

    mem_kv = _norm_matmul(mem_prompt.reshape(batch * n_mem, d), norm_mem[l], w_mem_kv[l].astype(BF16), n_mem)
    wq = w_cross_q[l].astype(BF16)
    wo = w_cross_o[l].astype(BF16)
    xp = _cross(xp.reshape(batch, seq, d), norm_cross[l], wq, wo, mem_kv.reshape(batch, n_mem, 2 * mem_w),
                n_mem, 512)
    xs = _cross(xs.reshape(db, 1, d), norm_cross[l], wq, wo, cache_mem_kv[l].reshape(-1, HEAD_DIM), n_mem, 1)
    mem_kv_prompt = mem_kv.reshape(1, batch, n_mem, 2, MEM_HEADS, HEAD_DIM)

    wu = w_up[l].astype(BF16)
    wd = w_down[l].astype(BF16)
    y_prompt = _mlp_final(xp.reshape(m, d), norm_ffn[l], wu, wd, norm_final, 512, 1024).reshape(batch, seq, d)
    y_sample = _mlp_final(xs.reshape(db, d), norm_ffn[l], wu, wd, norm_final, db, 1024).reshape(db, 1, d)

    return (y_prompt, y_sample, sb_kv_prompt, sb_kv_sample, nsa_kv_prompt, nsa_kv_sample,
            win_kv_prompt, win_kv_sample, mem_kv_prompt)
```

```python
import functools
import math

import numpy as np
import jax
import jax.numpy as jnp
from jax import lax
from jax.experimental import pallas as pl
from jax.experimental.pallas import tpu as pltpu

HEAD_DIM = 128
NSA_GROUP = 4
MEM_HEADS = 4
PAGE_SIZE = 128
CMP_LEN = 32
CMP_STRIDE = 16
SEL_BLOCK = 64
N_SEL = 16
WINDOW = 512
N_BUCKETS = 32
MAX_DISTANCE = 1024
RMS_EPS = 1e-6
NEG_INF = -1e30
BIG = 1e30
LANES = 128
SUBLANES = 8
VMEM_LIMIT = 56 * 1024 * 1024
_PLANE_PITCH = PAGE_SIZE + SUBLANES

F32 = jnp.float32
BF16 = jnp.bfloat16
_NT = (((1,), (1,)), ((), ()))


def _bucket_thresholds():
    max_exact = N_BUCKETS // 2
    n_log = N_BUCKETS - max_exact
    ratio = MAX_DISTANCE // max_exact
    th = list(range(max_exact)) + [max_exact]
    for m in range(1, n_log):
        n = th[-1]
        while n ** n_log < (max_exact ** n_log) * (ratio ** m):
            n += 1
        th.append(n)
    return tuple(th)


THRESHOLDS = _bucket_thresholds()


def _bias_chain(rel, tab_ref, h):
    v = jnp.full(rel.shape, tab_ref[0, h], F32)
    for b in range(1, N_BUCKETS):
        v = jnp.where(rel >= THRESHOLDS[b], tab_ref[b, h], v)
    return v


def _rms(x, g):
    return x * lax.rsqrt(jnp.mean(x * x, axis=-1, keepdims=True) + RMS_EPS) * g


def _softplus(z):
    return jnp.maximum(z, 0.0) + jnp.log(1.0 + jnp.exp(-jnp.abs(z)))


def _split2(x):
    hi = x.astype(BF16)
    lo = (x - hi.astype(F32)).astype(BF16)
    return hi, lo


def _split3(x):
    hi = x.astype(BF16)
    r = x - hi.astype(F32)
    mid = r.astype(BF16)
    lo = (r - mid.astype(F32)).astype(BF16)
    return hi, mid, lo


def _masked_softmax(lg, valid):
    lgm = jnp.where(valid, lg, NEG_INF)
    m = jnp.max(lgm, axis=-1, keepdims=True)
    e = jnp.where(valid, jnp.exp(lgm - m), 0.0)
    s = jnp.sum(e, axis=-1, keepdims=True)
    return e / jnp.where(s > 0.0, s, 1.0)


def _params(*sem):
    return pltpu.CompilerParams(dimension_semantics=sem, vmem_limit_bytes=VMEM_LIMIT)


def _resident(shape):
    nd = len(shape)
    return pl.BlockSpec(shape, lambda *_: (0,) * nd, pipeline_mode=pl.Buffered(1))


def _proj_body(x_ref, g_ref, w_ref, qsb_ref, sbkv_ref, sbkv16_ref, qns_ref, nsakv_ref,
               nsakv16_ref, winkv_ref, winkv16_ref, gates_ref, *, sbw, nsw, kvw, scale):
    hb = _rms(x_ref[...], g_ref[...]).astype(BF16)

    def mm(c0, n):
        return jnp.dot(hb, w_ref[:, c0:c0 + n], preferred_element_type=F32)

    c = 0
    qsb_ref[...] = (mm(c, sbw) * scale).astype(BF16)
    c += sbw
    for half in range(2):
        kv = mm(c, sbw)
        sbkv_ref[:, half * sbw:(half + 1) * sbw] = kv
        sbkv16_ref[:, half * sbw:(half + 1) * sbw] = kv.astype(BF16)
        c += sbw
    qns_ref[...] = (mm(c, nsw) * scale).astype(BF16)
    c += nsw
    kv = mm(c, 4 * kvw)
    nsakv_ref[...] = kv
    nsakv16_ref[...] = kv.astype(BF16)
    c += 4 * kvw
    kv = mm(c, 2 * kvw)
    winkv_ref[...] = kv
    winkv16_ref[...] = kv.astype(BF16)
    c += 2 * kvw
    gates_ref[...] = jax.nn.sigmoid(mm(c, LANES))


def _project(x, g, w_pad, tm):
    m, d = x.shape
    sbw = nsw = d // 2
    kvw = nsw // NSA_GROUP
    widths = (sbw, 2 * sbw, 2 * sbw, nsw, 4 * kvw, 4 * kvw, 2 * kvw, 2 * kvw, LANES)
    dtypes = (BF16, F32, BF16, BF16, F32, BF16, F32, BF16, F32)
    body = functools.partial(_proj_body, sbw=sbw, nsw=nsw, kvw=kvw, scale=1.0 / math.sqrt(HEAD_DIM))
    return pl.pallas_call(
        body,
        grid=(m // tm,),
        in_specs=[pl.BlockSpec((tm, d), lambda i: (i, 0)),
                  pl.BlockSpec((1, d), lambda i: (0, 0)),
                  _resident(w_pad.shape)],
        out_specs=[pl.BlockSpec((tm, w), lambda i: (i, 0)) for w in widths],
        out_shape=[jax.ShapeDtypeStruct((m, w), dt) for w, dt in zip(widths, dtypes)],
        compiler_params=_params("parallel"),
        name="in_proj",
    )(x, g.reshape(1, d), w_pad)


def _bias_tile_body(tab_ref, o_ref, *, q_stride, k_stride, k_off, window, n_heads):
    m = pl.program_id(0)
    i = lax.broadcasted_iota(jnp.int32, (LANES, LANES), 0)
    j = lax.broadcasted_iota(jnp.int32, (LANES, LANES), 1)
    rel = m * q_stride + i - (k_stride * j + k_off)
    for h in range(n_heads):
        v = _bias_chain(rel, tab_ref, h)
        if window:
            v = jnp.where((rel >= 0) & (rel < WINDOW), v, NEG_INF)
        o_ref[0, h] = v


def _bias_tiles(table, n_tiles, q_stride, k_stride, k_off, window=False):
    n_heads = table.shape[1]
    body = functools.partial(_bias_tile_body, q_stride=q_stride, k_stride=k_stride, k_off=k_off, window=window,
                             n_heads=n_heads)
    out = pl.pallas_call(
        body,
        grid=(n_tiles,),
        in_specs=[pl.BlockSpec(memory_space=pltpu.SMEM)],
        out_specs=pl.BlockSpec((1, n_heads, LANES, LANES), lambda m: (m, 0, 0, 0)),
        out_shape=jax.ShapeDtypeStruct((n_tiles, n_heads, LANES, LANES), F32),
        compiler_params=_params("parallel"),
        name="bias_tiles",
    )(table)
    return out.reshape(n_tiles, n_heads // NSA_GROUP, NSA_GROUP * LANES, LANES)


def _cumsum_rhs(t):
    u = np.tril(np.ones((t, t), np.float32))
    half = np.concatenate([u, np.ones((t, t), np.float32)], axis=1)
    return jnp.asarray(np.concatenate([half, half], axis=0), BF16)


def _sb_tile(q, k, v, uo, carry, acc, valid):
    t = q.shape[0]
    sub = uo.shape[1] // 2
    z = lax.dot_general(q, k, _NT, preferred_element_type=F32)
    sp = _softplus(z)
    if valid is not None:
        sp = jnp.where(valid, sp, 0.0)
    parts = []
    for s in reversed(range(t // sub)):
        hi, lo = _split2(sp[:, s * sub:(s + 1) * sub])
        ct = jnp.dot(jnp.concatenate([hi, lo], axis=1), uo, preferred_element_type=F32)
        parts.insert(0, jnp.exp(z[:, s * sub:(s + 1) * sub] - (ct[:, :sub] + carry)))
        carry = carry + ct[:, sub:]
    a = jnp.concatenate(parts, axis=1)
    if valid is not None:
        a = jnp.where(valid, a, 0.0)
    acc = acc + jnp.dot(a.astype(BF16), v, preferred_element_type=F32)
    return carry, acc


def _sb_prompt_body(q_ref, k_ref, v_ref, g_ref, uo_ref, o_ref, *, tq, hps):
    qi = pl.program_id(2)
    uo = uo_ref[...]
    row = lax.broadcasted_iota(jnp.int32, (tq, tq), 0)
    col = lax.broadcasted_iota(jnp.int32, (tq, tq), 1)
    lanes = [slice(h * HEAD_DIM, (h + 1) * HEAD_DIM) for h in range(hps)]
    qs = [q_ref[:, sl] for sl in lanes]

    def tiles(start, state, valid):
        out = []
        for h, sl in enumerate(lanes):
            out.extend(_sb_tile(qs[h], k_ref[pl.ds(start, tq), sl], v_ref[pl.ds(start, tq), sl], uo,
                                state[2 * h], state[2 * h + 1], valid))
        return tuple(out)

    state = (jnp.zeros((tq, uo.shape[1] // 2), F32), jnp.zeros((tq, HEAD_DIM), F32)) * hps
    state = tiles(pl.multiple_of(qi * tq, tq), state, col < row)
    state = lax.fori_loop(0, qi, lambda t, st: tiles(pl.multiple_of((qi - 1 - t) * tq, tq), st, None), state)
    for h, sl in enumerate(lanes):
        o_ref[:, sl] = _rms(state[2 * h + 1], g_ref[0, :, sl]).astype(BF16)


def _sb_prompt(qsb, sbkv16, gain, batch, seq):
    m, sbw = qsb.shape
    heads = sbw // HEAD_DIM
    tq = 2 * LANES
    hps = 8
    nq = seq // tq
    wide = hps * HEAD_DIM
    groups = heads // hps
    return pl.pallas_call(
        functools.partial(_sb_prompt_body, tq=tq, hps=hps),
        grid=(batch, groups, nq),
        in_specs=[pl.BlockSpec((tq, wide), lambda b, h, i: (b * nq + i, h)),
                  pl.BlockSpec((seq, wide), lambda b, h, i: (b, h)),
                  pl.BlockSpec((seq, wide), lambda b, h, i: (b, groups + h)),
                  pl.BlockSpec((1, 1, wide), lambda b, h, i: (h, 0, 0)),
                  pl.BlockSpec((2 * LANES, 2 * LANES), lambda b, h, i: (0, 0))],
        out_specs=pl.BlockSpec((tq, wide), lambda b, h, i: (b * nq + i, h)),
        out_shape=jax.ShapeDtypeStruct((m, sbw), BF16),
        compiler_params=_params("parallel", "parallel", "arbitrary"),
        name="sb_prompt",
    )(qsb, sbkv16, sbkv16, gain.reshape(groups, 1, wide), _cumsum_rhs(LANES))


def _sb_decode_body(pt_ref, *refs, n_pages_step, n_steps, heads):
    pages = refs[:n_pages_step]
    q_ref, uo_ref, g_ref, o_ref, acc_ref, carry_ref = refs[n_pages_step:n_pages_step + 6]
    plane_refs = refs[n_pages_step + 6:]
    ppg = n_pages_step // len(plane_refs)
    s = pl.program_id(1)
    rows_tok = 2 * heads
    pitch = _PLANE_PITCH

    @pl.when(s == 0)
    def _():
        acc_ref[...] = jnp.zeros_like(acc_ref)
        carry_ref[...] = jnp.zeros_like(carry_ref)

    for p in range(n_pages_step):
        for tok in range(PAGE_SIZE):
            for kv in range(2):
                plane_refs[p // ppg][pl.ds(((p % ppg) * rows_tok + kv * heads) * pitch + tok, heads,
                                           stride=pitch), :] = pages[p][pl.ds(tok * rows_tok + kv * heads, heads), :]

    def plane(p, r):
        return plane_refs[p // ppg][pl.ds(((p % ppg) * rows_tok + r) * pitch, PAGE_SIZE), :].astype(BF16)

    q = q_ref[0]
    uo = uo_ref[...]
    rowi = lax.broadcasted_iota(jnp.int32, (heads, HEAD_DIM), 0)
    zs = []
    for p in range(n_pages_step):
        z = jnp.zeros((heads, PAGE_SIZE), F32)
        for h in range(heads):
            z = jnp.where(rowi == h, lax.dot_general(q, plane(p, h), _NT, preferred_element_type=F32), z)
        zs.append(z)
    cts = []
    for z in zs:
        hi, lo = _split2(_softplus(z))
        cts.append(jnp.dot(jnp.concatenate([hi, lo], axis=1), uo, preferred_element_type=F32))
    carry = carry_ref[...]
    acc = acc_ref[...]
    for p in range(n_pages_step):
        a = jnp.exp(zs[p] - (cts[p][:, :PAGE_SIZE] + carry)).astype(BF16)
        carry = carry + cts[p][:, PAGE_SIZE:]
        for h in range(heads):
            acc = acc + jnp.where(rowi == h, jnp.dot(a, plane(p, heads + h), preferred_element_type=F32), 0.0)
    acc_ref[...] = acc
    carry_ref[...] = carry

    @pl.when(s == n_steps - 1)
    def _():
        o_ref[0] = _rms(acc_ref[...], g_ref[...])


def _sb_decode(q, cache_rows, page_table, gain, pages_per_step):
    db, heads, _ = q.shape
    n_pages = page_table.shape[1]
    n_steps = n_pages // pages_per_step
    page_rows = PAGE_SIZE * 2 * heads
    pages_group = min(4, pages_per_step)

    def page_spec(p):
        def idx(b, s, pt):
            return (pt[b * n_pages + n_pages - 1 - (s * pages_per_step + p)], 0)
        return pl.BlockSpec((page_rows, HEAD_DIM), idx)

    grid_spec = pltpu.PrefetchScalarGridSpec(
        num_scalar_prefetch=1,
        grid=(db, n_steps),
        in_specs=[page_spec(p) for p in range(pages_per_step)] + [
            pl.BlockSpec((1, heads, HEAD_DIM), lambda b, s, pt: (b, 0, 0)),
            pl.BlockSpec((2 * PAGE_SIZE, 2 * PAGE_SIZE), lambda b, s, pt: (0, 0)),
            pl.BlockSpec((heads, HEAD_DIM), lambda b, s, pt: (0, 0))],
        out_specs=pl.BlockSpec((1, heads, HEAD_DIM), lambda b, s, pt: (b, 0, 0)),
        scratch_shapes=[pltpu.VMEM((heads, HEAD_DIM), F32), pltpu.VMEM((heads, PAGE_SIZE), F32)]
        + [pltpu.VMEM((pages_group * 2 * heads * _PLANE_PITCH, HEAD_DIM), F32)
           for _ in range(pages_per_step // pages_group)])
    body = functools.partial(_sb_decode_body, n_pages_step=pages_per_step, n_steps=n_steps, heads=heads)
    return pl.pallas_call(
        body, grid_spec=grid_spec,
        out_shape=jax.ShapeDtypeStruct((db, heads, HEAD_DIM), F32),
        compiler_params=_params("parallel", "arbitrary"),
        name="sb_decode",
    )(page_table.reshape(-1), *([cache_rows] * pages_per_step), q, _cumsum_rhs(PAGE_SIZE),
      gain.reshape(heads, HEAD_DIM))


def _pe_term_body(pe_ref, w_ref, o_ref):
    o_ref[0] = jnp.dot(pe_ref[0].astype(BF16), w_ref[0].astype(BF16), preferred_element_type=F32)


def _pe_term(pe, w1):
    n, _, kdim = pe.shape
    hid = w1.shape[-1]
    return pl.pallas_call(
        _pe_term_body, grid=(n,),
        in_specs=[pl.BlockSpec((1, SUBLANES, kdim), lambda t: (t, 0, 0)),
                  pl.BlockSpec((1, kdim, hid), lambda t: (t, 0, 0))],
        out_specs=pl.BlockSpec((1, SUBLANES, hid), lambda t: (t, 0, 0)),
        out_shape=jax.ShapeDtypeStruct((n, SUBLANES, hid), F32),
        compiler_params=_params("parallel"),
        name="cmp_pe_term",
    )(pe, w1)


def _compress_body(pt_ref, *refs, n_pages_step, kv_heads, row_layout):
    pages = refs[:n_pages_step]
    w1_ref, w2_ref, pe_ref, o_ref, carry_ref = refs[n_pages_step:n_pages_step + 5]
    tok_refs = refs[n_pages_step + 5:]
    chunks_page = PAGE_SIZE // CMP_STRIDE
    rows = n_pages_step * chunks_page
    rows_tok = 4 * kv_heads
    pitch = _PLANE_PITCH

    @pl.when(pl.program_id(1) == 0)
    def _():
        carry_ref[...] = jnp.zeros_like(carry_ref)

    for t in range(2):
        for k in range(kv_heads):
            idx = t * kv_heads + k
            for p in range(n_pages_step):
                for v in range(PAGE_SIZE // SUBLANES):
                    if row_layout:
                        x = pages[p][pl.ds(v * SUBLANES * rows_tok + idx, SUBLANES, stride=rows_tok), :]
                    else:
                        x = pages[p][0, v * SUBLANES:(v + 1) * SUBLANES, idx * HEAD_DIM:(idx + 1) * HEAD_DIM]
                    chunk, l0 = divmod(v * SUBLANES, CMP_STRIDE)
                    tok_refs[t][pl.ds((k * CMP_STRIDE + l0) * pitch + p * chunks_page + chunk, SUBLANES,
                                      stride=pitch), :] = x

    rowi = lax.broadcasted_iota(jnp.int32, (rows, HEAD_DIM), 0)
    for t in range(2):
        x = jnp.concatenate(
            [jnp.concatenate([tok_refs[t][pl.ds((k * CMP_STRIDE + l) * pitch, rows), :].astype(BF16)
                              for l in range(CMP_STRIDE)], axis=1) for k in range(kv_heads)], axis=0)
        acc = jnp.dot(x, w1_ref[t], preferred_element_type=F32)
        hids = []
        for k in range(kv_heads):
            idx = t * kv_heads + k
            first = acc[k * rows:(k + 1) * rows, :HEAD_DIM]
            second = acc[k * rows:(k + 1) * rows, HEAD_DIM:]
            prev_first = jnp.where(rowi == 0, carry_ref[idx], pltpu.roll(first, 1, axis=0))
            carry_ref[idx] = first[rows - 1:rows, :]
            hids.append(jax.nn.gelu(prev_first + second + pe_ref[t, 0:1, :]).astype(BF16))
        out = jnp.dot(jnp.concatenate(hids, axis=0), w2_ref[t], preferred_element_type=F32)
        for k in range(kv_heads):
            c0 = (t * kv_heads + k) * HEAD_DIM
            o_ref[0, :, c0:c0 + HEAD_DIM] = out[k * rows:(k + 1) * rows]


def _compress(cache, page_table, w1cat, w2, pe_term, kv_heads, pages_per_step):
    nb, n_pages = page_table.shape
    n_steps = n_pages // pages_per_step
    chunks = n_pages * (PAGE_SIZE // CMP_STRIDE)
    rows_step = pages_per_step * (PAGE_SIZE // CMP_STRIDE)
    width = 2 * kv_heads * HEAD_DIM
    row_layout = cache.ndim == 2

    def page_spec(p):
        if row_layout:
            return pl.BlockSpec((PAGE_SIZE * 4 * kv_heads, HEAD_DIM),
                                lambda b, s, pt: (pt[b * n_pages + s * pages_per_step + p], 0))
        return pl.BlockSpec((1, PAGE_SIZE, width),
                            lambda b, s, pt: (pt[b * n_pages + s * pages_per_step + p], 0, 0))

    grid_spec = pltpu.PrefetchScalarGridSpec(
        num_scalar_prefetch=1,
        grid=(nb, n_steps),
        in_specs=[page_spec(p) for p in range(pages_per_step)] + [
            pl.BlockSpec(w1cat.shape, lambda b, s, pt: (0, 0, 0)),
            pl.BlockSpec(w2.shape, lambda b, s, pt: (0, 0, 0)),
            pl.BlockSpec(pe_term.shape, lambda b, s, pt: (0, 0, 0))],
        out_specs=pl.BlockSpec((1, rows_step, width), lambda b, s, pt: (b, s, 0)),
        scratch_shapes=[pltpu.VMEM((2 * kv_heads, 1, HEAD_DIM), F32),
                        pltpu.VMEM((kv_heads * CMP_STRIDE * _PLANE_PITCH, HEAD_DIM), F32),
                        pltpu.VMEM((kv_heads * CMP_STRIDE * _PLANE_PITCH, HEAD_DIM), F32)])
    assert rows_step <= PAGE_SIZE, "a step's chunks must fit one plane"
    body = functools.partial(_compress_body, n_pages_step=pages_per_step, kv_heads=kv_heads,
                             row_layout=row_layout)
    return pl.pallas_call(
        body, grid_spec=grid_spec,
        out_shape=jax.ShapeDtypeStruct((nb, chunks, width), F32),
        compiler_params=_params("parallel", "arbitrary"),
        name="nsa_compress",
    )(page_table.reshape(-1), *([cache] * pages_per_step), w1cat, w2, pe_term)


def _nsa_prompt_body(q_ref, kc_ref, vc_ref, ks_ref, vs_ref, kw_ref, vw_ref, tsel_ref, twin_ref, tcmp_ref,
                     ind_ref, exp_ref, gates_ref, gn_ref, o_ref, mask_ref, *, n_blk, n_pick, n_kt, nk):
    i = pl.program_id(1)
    t = LANES
    rows = NSA_GROUP * t
    gw = NSA_GROUP * HEAD_DIM
    heads = range(nk)

    def hd(k):
        return slice(k * HEAD_DIM, (k + 1) * HEAD_DIM)

    qs = [jnp.concatenate([q_ref[:, k * gw + g * HEAD_DIM:k * gw + (g + 1) * HEAD_DIM] for g in range(NSA_GROUP)],
                          axis=0) for k in heads]
    qrow = lax.broadcasted_iota(jnp.int32, (rows, t), 0) & (t - 1)
    col = lax.broadcasted_iota(jnp.int32, (rows, t), 1)
    qpos = i * t + qrow
    valid_c = (col >= 1) & (qpos >= CMP_STRIDE * col + (CMP_LEN - CMP_STRIDE - 1))
    jj = lax.broadcasted_iota(jnp.int32, (n_blk, t), 0)
    qq = i * t + lax.broadcasted_iota(jnp.int32, (n_blk, t), 1)
    qblk = qq // SEL_BLOCK
    forced = (jj == 0) | (jj == qblk) | (jj == qblk - 1)
    eligible = jj * SEL_BLOCK <= qq

    o_cmps = []
    for k in heads:
        lc = (lax.dot_general(qs[k], kc_ref[0, :, hd(k)].astype(BF16), _NT, preferred_element_type=F32)
              + tcmp_ref[0, k])
        pc = _masked_softmax(lc, valid_c)
        o_cmps.append(jnp.dot(pc.astype(BF16), vc_ref[0, :, hd(k)].astype(BF16), preferred_element_type=F32))

        pcs = pc[0:t]
        for g in range(1, NSA_GROUP):
            pcs = pcs + pc[g * t:(g + 1) * t]
        p3 = jnp.concatenate(_split3(pcs), axis=1)
        score = lax.dot_general(ind_ref[...], p3, _NT, preferred_element_type=F32)[:n_blk]
        score = jnp.where(forced, BIG, jnp.where(eligible, score, NEG_INF))
        rank = jnp.zeros((n_blk, t), jnp.int32)
        for jp in range(n_blk):
            r = score[jp:jp + 1, :]
            beats = (r > score) | ((r == score) & (jj > jp))
            rank = rank + beats.astype(jnp.int32)
        sel_t = jnp.concatenate([(rank < n_pick).astype(F32), jnp.zeros((t - n_blk, t), F32)], axis=0)
        maskf = jnp.dot(sel_t.T.astype(BF16), exp_ref[...], preferred_element_type=F32)
        for kt in range(n_kt):
            mask_ref[k, kt] = (maskf[:, kt * t:(kt + 1) * t] - 1.0) * BIG


    two = 2 * t
    qrow2 = lax.broadcasted_iota(jnp.int32, (rows, two), 0) & (t - 1)
    col2 = lax.broadcasted_iota(jnp.int32, (rows, two), 1)

    def sel_logits(k, jg, diag):
        s = pl.multiple_of(jg * two, two)
        d0 = i - 2 * jg
        bias = jnp.concatenate([tsel_ref[d0, k], tsel_ref[jnp.maximum(d0 - 1, 0), k]], axis=1)
        picked = jnp.concatenate([mask_ref[k, 2 * jg], mask_ref[k, 2 * jg + 1]], axis=1)
        lg = (lax.dot_general(qs[k], ks_ref[pl.ds(s, two), hd(k)], _NT, preferred_element_type=F32)
              + (bias + jnp.concatenate([picked] * NSA_GROUP, axis=0)))
        if diag:
            lg = jnp.where(jg * two + col2 <= i * t + qrow2, lg, NEG_INF)
        return lg

    def fold(x):
        return jnp.maximum(x[:, :t], x[:, t:])

    jd = i // 2
    m_vecs = tuple(fold(sel_logits(k, jd, True)) for k in heads)
    m_vecs = lax.fori_loop(
        0, jd, lambda jg, mv: tuple(jnp.maximum(mv[k], fold(sel_logits(k, jg, False))) for k in heads), m_vecs)
    m_b2s = []
    for k in heads:
        m_b = jnp.broadcast_to(jnp.max(m_vecs[k], axis=-1, keepdims=True), (rows, t))
        m_b2s.append(jnp.concatenate([m_b, m_b], axis=1))

    def sel_weights(jg, carry, diag):
        out = []
        for k in heads:
            p = jnp.exp(sel_logits(k, jg, diag) - m_b2s[k])
            s = pl.multiple_of(jg * two, two)
            out.append(carry[2 * k] + (p[:, :t] + p[:, t:]))
            out.append(carry[2 * k + 1]
                       + jnp.dot(p.astype(BF16), vs_ref[pl.ds(s, two), hd(k)], preferred_element_type=F32))
        return tuple(out)

    carry = sel_weights(jd, (jnp.zeros((rows, t), F32), jnp.zeros((rows, HEAD_DIM), F32)) * nk, True)
    carry = lax.fori_loop(0, jd, lambda jg, c: sel_weights(jg, c, False), carry)
    o_sels = [carry[2 * k + 1] / jnp.sum(carry[2 * k], axis=-1, keepdims=True) for k in heads]

    n_back = WINDOW // t
    for k in heads:
        lgs = []
        starts = []
        for w in range(n_back + 1):
            s = pl.multiple_of(jnp.maximum(i - w, 0) * t, t)
            lg = (lax.dot_general(qs[k], kw_ref[pl.ds(s, t), hd(k)], _NT, preferred_element_type=F32)
                  + twin_ref[w, k])
            if w > 0:
                lg = lg + jnp.where(i >= w, 0.0, NEG_INF)
            lgs.append(lg)
            starts.append(s)
        m_vec = lgs[0]
        for lg in lgs[1:]:
            m_vec = jnp.maximum(m_vec, lg)
        m_b = jnp.broadcast_to(jnp.max(m_vec, axis=-1, keepdims=True), (rows, t))
        l_vec = jnp.zeros((rows, t), F32)
        acc = jnp.zeros((rows, HEAD_DIM), F32)
        for lg, s in zip(lgs, starts):
            p = jnp.exp(lg - m_b)
            l_vec = l_vec + p
            acc = acc + jnp.dot(p.astype(BF16), vw_ref[pl.ds(s, t), hd(k)], preferred_element_type=F32)
        o_win = acc / jnp.sum(l_vec, axis=-1, keepdims=True)

        gates = gates_ref[k]
        gn = gn_ref[k]
        for g in range(NSA_GROUP):
            sl = slice(g * t, (g + 1) * t)
            o = (gates[:, 3 * g:3 * g + 1] * o_cmps[k][sl] + gates[:, 3 * g + 1:3 * g + 2] * o_sels[k][sl]
                 + gates[:, 3 * g + 2:3 * g + 3] * o_win[sl])
            o_ref[:, k * gw + g * HEAD_DIM:k * gw + (g + 1) * HEAD_DIM] = (
                _rms(o, gn[:, g * HEAD_DIM:(g + 1) * HEAD_DIM]).astype(BF16))


def _sel_constants(n_chunks, n_blk, n_cmp, key_len):
    blk = np.arange(n_blk)
    lo = np.clip((blk * SEL_BLOCK - CMP_LEN) // CMP_STRIDE + 1, 0, n_cmp)
    hi = np.clip(-(-((blk + 1) * SEL_BLOCK) // CMP_STRIDE), 0, n_cmp)
    c = np.arange(n_chunks) - 1
    ind = ((c[None, :] >= lo[:, None]) & (c[None, :] < hi[:, None]) & (c[None, :] >= 0)).astype(np.float32)
    expand = (np.arange(key_len)[None, :] // SEL_BLOCK == blk[:, None]).astype(np.float32)
    return ind, expand


def _nsa_prompt(qns, kcvc, nsakv16, winkv16, tsel, twin, tcmp, gates_k, gain, batch, seq):
    m, nsw = qns.shape
    kv_heads = nsw // (NSA_GROUP * HEAD_DIM)
    t = LANES
    nq = seq // t
    n_chunks = seq // CMP_STRIDE
    assert n_chunks == t, "the compressed branch is tiled as a single 128-column tile"
    n_cmp = (seq - CMP_LEN) // CMP_STRIDE + 1
    n_blk = -(-seq // SEL_BLOCK)
    n_pick = min(N_SEL, n_blk)
    ind, expand = _sel_constants(n_chunks, n_blk, n_cmp, seq)
    ind_pad = np.zeros((t, n_chunks), np.float32)
    ind_pad[:n_blk] = ind
    exp_pad = np.zeros((t, seq), np.float32)
    exp_pad[:n_blk] = expand
    ind3 = jnp.asarray(np.concatenate([ind_pad] * 3, axis=1), BF16)
    gw = NSA_GROUP * HEAD_DIM
    nk = kv_heads
    kw = nk * HEAD_DIM
    body = functools.partial(_nsa_prompt_body, n_blk=n_blk, n_pick=n_pick, n_kt=nq, nk=nk)
    return pl.pallas_call(
        body,
        grid=(batch, nq),
        in_specs=[pl.BlockSpec((t, nk * gw), lambda b, i: (b * nq + i, 0)),
                  pl.BlockSpec((1, n_chunks, kw), lambda b, i: (b, 0, 0)),
                  pl.BlockSpec((1, n_chunks, kw), lambda b, i: (b, 0, 1)),
                  pl.BlockSpec((seq, kw), lambda b, i: (b, 2)),
                  pl.BlockSpec((seq, kw), lambda b, i: (b, 3)),
                  pl.BlockSpec((seq, kw), lambda b, i: (b, 0)),
                  pl.BlockSpec((seq, kw), lambda b, i: (b, 1)),
                  pl.BlockSpec((nq, nk, gw, t), lambda b, i: (0, 0, 0, 0)),
                  pl.BlockSpec((twin.shape[0], nk, gw, t), lambda b, i: (0, 0, 0, 0)),
                  pl.BlockSpec((1, nk, gw, t), lambda b, i: (i, 0, 0, 0)),
                  pl.BlockSpec((t, 3 * n_chunks), lambda b, i: (0, 0)),
                  pl.BlockSpec((t, seq), lambda b, i: (0, 0)),
                  pl.BlockSpec((nk, t, LANES), lambda b, i: (0, b * nq + i, 0)),
                  pl.BlockSpec((nk, 1, gw), lambda b, i: (0, 0, 0))],
        out_specs=pl.BlockSpec((t, nk * gw), lambda b, i: (b * nq + i, 0)),
        out_shape=jax.ShapeDtypeStruct((m, nsw), BF16),
        scratch_shapes=[pltpu.VMEM((nk, nq, t, t), F32)],
        compiler_params=_params("parallel", "arbitrary"),
        name="nsa_prompt",
    )(qns, kcvc, kcvc, nsakv16, nsakv16, winkv16, winkv16, tsel, twin, tcmp, ind3,
      jnp.asarray(exp_pad, BF16), gates_k, gain.reshape(kv_heads, 1, gw))


def _cmp_decode_body(tab_ref, q_ref, kcvc_ref, ind_ref, ocmp_ref, score_ref, *, past, kv_heads):
    n_chunks = kcvc_ref.shape[1]
    r = lax.broadcasted_iota(jnp.int32, (1, n_chunks), 1)
    rel = past - (CMP_STRIDE * r + (CMP_LEN - CMP_STRIDE - 1))
    valid = (r >= 1) & (rel >= 0)
    score_ref[...] = jnp.zeros_like(score_ref)
    for k in range(kv_heads):
        kc = kcvc_ref[0, :, k * HEAD_DIM:(k + 1) * HEAD_DIM].astype(BF16)
        vc = kcvc_ref[0, :, (kv_heads + k) * HEAD_DIM:(kv_heads + k + 1) * HEAD_DIM].astype(BF16)
        bias = jnp.concatenate([_bias_chain(rel, tab_ref, k * NSA_GROUP + g) for g in range(NSA_GROUP)], axis=0)
        lg = lax.dot_general(q_ref[0, k], kc, _NT, preferred_element_type=F32) + bias
        p = _masked_softmax(lg, valid)
        ocmp_ref[0, k * NSA_GROUP:(k + 1) * NSA_GROUP, :] = jnp.dot(p.astype(BF16), vc, preferred_element_type=F32)
        p3 = jnp.concatenate(_split3(jnp.sum(p, axis=0, keepdims=True)), axis=1)
        score_ref[0, k:k + 1, :] = jnp.dot(p3, ind_ref[...], preferred_element_type=F32)


def _cmp_decode(table, q, kcvc, ind3, past, kv_heads):
    db = q.shape[0]
    heads = kv_heads * NSA_GROUP
    n_chunks = kcvc.shape[1]
    nj = ind3.shape[1]
    body = functools.partial(_cmp_decode_body, past=past, kv_heads=kv_heads)
    return pl.pallas_call(
        body, grid=(db,),
        in_specs=[pl.BlockSpec(memory_space=pltpu.SMEM),
                  pl.BlockSpec((1, kv_heads, NSA_GROUP, HEAD_DIM), lambda b: (b, 0, 0, 0)),
                  pl.BlockSpec((1, n_chunks, kcvc.shape[2]), lambda b: (b, 0, 0)),
                  pl.BlockSpec(ind3.shape, lambda b: (0, 0))],
        out_specs=[pl.BlockSpec((1, heads, HEAD_DIM), lambda b: (b, 0, 0)),
                   pl.BlockSpec((1, SUBLANES, nj), lambda b: (b, 0, 0))],
        out_shape=[jax.ShapeDtypeStruct((db, heads, HEAD_DIM), F32),
                   jax.ShapeDtypeStruct((db, SUBLANES, nj), F32)],
        compiler_params=_params("parallel"),
        name="nsa_cmp_decode",
    )(table, q, kcvc, ind3)


def _topk_body(s_ref, o_ref, *, n_cache_blk, n_pick):
    s = s_ref[...]
    j = lax.broadcasted_iota(jnp.int32, s.shape, 1)
    lane = lax.broadcasted_iota(jnp.int32, o_ref.shape, 1)
    s = jnp.where((j == 0) | (j == n_cache_blk - 1), BIG, s)
    s = jnp.where(j < n_cache_blk, s, -jnp.inf)
    picks = jnp.zeros(o_ref.shape, jnp.int32)
    for t in range(n_pick):
        m = jnp.max(s, axis=-1, keepdims=True)
        idx = jnp.min(jnp.where(s == m, j, 2 ** 30), axis=-1, keepdims=True)
        picks = jnp.where(lane == t, idx, picks)
        s = jnp.where(j == idx, -jnp.inf, s)
    o_ref[...] = picks


def _topk(scores, n_cache_blk, n_pick):
    rows = scores.shape[0]
    return pl.pallas_call(
        functools.partial(_topk_body, n_cache_blk=n_cache_blk, n_pick=n_pick),
        out_shape=jax.ShapeDtypeStruct((rows, LANES), jnp.int32),
        name="nsa_topk",
    )(scores)


def _sel_decode_body(pt_ref, pk_ref, *refs, n_pick, past, w_buf, kv_static):
    blocks = refs[:n_pick]
    (knew_ref, vnew_ref, win_ref, wknew_ref, wvnew_ref, q_ref, gates_ref, ocmp_ref, gn_ref,
     tab_ref, o_ref, ksc, vsc, kwc, vwc) = refs[n_pick:]
    b = pl.program_id(0)
    k = pl.program_id(1)
    kv_heads = kv_static
    q = q_ref[0, 0]
    n_slot = n_pick + 1
    sel_len = n_slot * SEL_BLOCK
    rows_tok = 4 * kv_static

    first64 = lax.broadcasted_iota(jnp.int32, (SEL_BLOCK, HEAD_DIM), 0) == 0
    for n in range(n_pick):
        ksc[n * SEL_BLOCK:(n + 1) * SEL_BLOCK, :] = (
            blocks[n][pl.ds(2 * kv_static + k, SEL_BLOCK, stride=rows_tok), :].astype(BF16))
        vsc[n * SEL_BLOCK:(n + 1) * SEL_BLOCK, :] = (
            blocks[n][pl.ds(3 * kv_static + k, SEL_BLOCK, stride=rows_tok), :].astype(BF16))
    ksc[n_pick * SEL_BLOCK:sel_len, :] = jnp.where(first64, knew_ref[0], 0.0).astype(BF16)
    vsc[n_pick * SEL_BLOCK:sel_len, :] = jnp.where(first64, vnew_ref[0], 0.0).astype(BF16)

    lane = lax.broadcasted_iota(jnp.int32, (1, sel_len), 1)
    slot = lane // SEL_BLOCK
    base = jnp.full((1, sel_len), past, jnp.int32)
    for n in range(n_pick):
        base = jnp.where(slot == n, pk_ref[(b * kv_heads + k) * n_pick + n] * SEL_BLOCK, base)
    rel = past - (base + (lane & (SEL_BLOCK - 1)))
    bias = jnp.concatenate([_bias_chain(rel, tab_ref, k * NSA_GROUP + g) for g in range(NSA_GROUP)], axis=0)
    lg = lax.dot_general(q, ksc[...], _NT, preferred_element_type=F32) + bias
    p = _masked_softmax(lg, rel >= 0)
    o_sel = jnp.dot(p.astype(BF16), vsc[...], preferred_element_type=F32)

    win_len = kwc.shape[0]
    firstw = lax.broadcasted_iota(jnp.int32, (win_len - w_buf, HEAD_DIM), 0) == 0
    kwc[0:w_buf, :] = win_ref[pl.ds(k, w_buf, stride=2 * kv_static), :].astype(BF16)
    vwc[0:w_buf, :] = win_ref[pl.ds(kv_static + k, w_buf, stride=2 * kv_static), :].astype(BF16)
    kwc[w_buf:win_len, :] = jnp.where(firstw, wknew_ref[0], 0.0).astype(BF16)
    vwc[w_buf:win_len, :] = jnp.where(firstw, wvnew_ref[0], 0.0).astype(BF16)
    relw = w_buf - lax.broadcasted_iota(jnp.int32, (1, win_len), 1)
    bias = jnp.concatenate([_bias_chain(relw, tab_ref, k * NSA_GROUP + g) for g in range(NSA_GROUP)], axis=0)
    lg = lax.dot_general(q, kwc[...], _NT, preferred_element_type=F32) + bias
    p = _masked_softmax(lg, (relw >= 0) & (relw < WINDOW))
    o_win = jnp.dot(p.astype(BF16), vwc[...], preferred_element_type=F32)

    o = gates_ref[0, 0, 0] * ocmp_ref[0, 0] + gates_ref[0, 0, 1] * o_sel + gates_ref[0, 0, 2] * o_win
    o_ref[0, 0] = _rms(o, gn_ref[0])


def _sel_decode(page_table, picks, cache_rows, nsa_new, win_rows, win_new, q, gates, ocmp, gain, table,
                n_pick, past, w_buf):
    db, kv_heads = q.shape[:2]
    n_pages = page_table.shape[1]
    halves = PAGE_SIZE // SEL_BLOCK
    rows_tok = 4 * kv_heads

    def pick_spec(n):
        def idx(b, k, pt, pk):
            j = pk[(b * kv_heads + k) * n_pick + n]
            return (pt[b * n_pages + j // halves] * halves + j % halves, 0)
        return pl.BlockSpec((SEL_BLOCK * rows_tok, HEAD_DIM), idx)

    def row_spec(col0):
        return pl.BlockSpec((1, 1, HEAD_DIM), lambda b, k, pt, pk: (b, 0, col0 + k))

    def head_spec(shape):
        nd = len(shape)
        return pl.BlockSpec((1, 1) + shape, lambda b, k, pt, pk: (b, k) + (0,) * nd)

    win_len = w_buf + LANES
    grid_spec = pltpu.PrefetchScalarGridSpec(
        num_scalar_prefetch=2,
        grid=(db, kv_heads),
        in_specs=[pick_spec(n) for n in range(n_pick)]
        + [row_spec(2 * kv_heads), row_spec(3 * kv_heads),
           pl.BlockSpec((w_buf * 2 * kv_heads, HEAD_DIM), lambda b, k, pt, pk: (b, 0)),
           row_spec(0), row_spec(kv_heads),
           head_spec((NSA_GROUP, HEAD_DIM)), head_spec((3, NSA_GROUP, 1)), head_spec((NSA_GROUP, HEAD_DIM)),
           pl.BlockSpec((1, NSA_GROUP, HEAD_DIM), lambda b, k, pt, pk: (k, 0, 0)),
           pl.BlockSpec(memory_space=pltpu.SMEM)],
        out_specs=head_spec((NSA_GROUP, HEAD_DIM)),
        scratch_shapes=[pltpu.VMEM(((n_pick + 1) * SEL_BLOCK, HEAD_DIM), BF16),
                        pltpu.VMEM(((n_pick + 1) * SEL_BLOCK, HEAD_DIM), BF16),
                        pltpu.VMEM((win_len, HEAD_DIM), BF16),
                        pltpu.VMEM((win_len, HEAD_DIM), BF16)])
    body = functools.partial(_sel_decode_body, n_pick=n_pick, past=past, w_buf=w_buf, kv_static=kv_heads)
    return pl.pallas_call(
        body, grid_spec=grid_spec,
        out_shape=jax.ShapeDtypeStruct((db, kv_heads, NSA_GROUP, HEAD_DIM), F32),
        compiler_params=_params("parallel", "parallel"),
        name="nsa_sel_decode",
    )(page_table.reshape(-1), picks, *([cache_rows] * n_pick), nsa_new, nsa_new, win_rows,
      win_new, win_new, q, gates, ocmp, gain.reshape(kv_heads, NSA_GROUP, HEAD_DIM), table)


def _out_proj_body(x_ref, a_ref, b_ref, w_ref, o_ref):
    half = a_ref.shape[1]
    o_ref[...] = (x_ref[...] + jnp.dot(a_ref[...], w_ref[0:half, :], preferred_element_type=F32)
                  + jnp.dot(b_ref[...], w_ref[half:2 * half, :], preferred_element_type=F32))


def _out_proj(x, a, b, w, tm):
    m, d = x.shape
    half = a.shape[1]
    return pl.pallas_call(
        _out_proj_body, grid=(m // tm,),
        in_specs=[pl.BlockSpec((tm, d), lambda i: (i, 0)),
                  pl.BlockSpec((tm, half), lambda i: (i, 0)),
                  pl.BlockSpec((tm, half), lambda i: (i, 0)),
                  _resident(w.shape)],
        out_specs=pl.BlockSpec((tm, d), lambda i: (i, 0)),
        out_shape=jax.ShapeDtypeStruct((m, d), F32),
        compiler_params=_params("parallel"),
        name="out_proj",
    )(x, a, b, w)


def _norm_matmul_body(x_ref, g_ref, w_ref, o_ref):
    o_ref[...] = jnp.dot(_rms(x_ref[...], g_ref[...]).astype(BF16), w_ref[...], preferred_element_type=F32)


def _norm_matmul(x, g, w, tm):
    m, d = x.shape
    n = w.shape[1]
    return pl.pallas_call(
        _norm_matmul_body, grid=(m // tm,),
        in_specs=[pl.BlockSpec((tm, d), lambda i: (i, 0)),
                  pl.BlockSpec((1, d), lambda i: (0, 0)),
                  _resident(w.shape)],
        out_specs=pl.BlockSpec((tm, n), lambda i: (i, 0)),
        out_shape=jax.ShapeDtypeStruct((m, n), F32),
        compiler_params=_params("parallel"),
        name="mem_kv_proj",
    )(x, g.reshape(1, d), w)


def _cross_body(x_ref, g_ref, wq_ref, wo_ref, mem_ref, o_ref, *, scale, n_mem, row_layout):
    x = x_ref[0]
    hb = _rms(x, g_ref[...]).astype(BF16)
    qh = (jnp.dot(hb, wq_ref[...], preferred_element_type=F32) * scale).astype(BF16)
    width = MEM_HEADS * HEAD_DIM
    outs = []
    for h in range(MEM_HEADS):
        if row_layout:
            k = mem_ref[pl.ds(h, n_mem, stride=2 * MEM_HEADS), :].astype(BF16)
            v = mem_ref[pl.ds(MEM_HEADS + h, n_mem, stride=2 * MEM_HEADS), :].astype(BF16)
        else:
            k = mem_ref[0, :, h * HEAD_DIM:(h + 1) * HEAD_DIM].astype(BF16)
            v = mem_ref[0, :, width + h * HEAD_DIM:width + (h + 1) * HEAD_DIM].astype(BF16)
        lg = lax.dot_general(qh[:, h * HEAD_DIM:(h + 1) * HEAD_DIM], k, _NT, preferred_element_type=F32)
        e = jnp.exp(lg - jnp.max(lg, axis=-1, keepdims=True))
        p = e / jnp.sum(e, axis=-1, keepdims=True)
        outs.append(jnp.dot(p.astype(BF16), v, preferred_element_type=F32).astype(BF16))
    o = jnp.concatenate(outs, axis=1)
    o_ref[0] = x + jnp.dot(o, wo_ref[...], preferred_element_type=F32)


def _cross(x3, g, wq, wo, mem, n_mem, tm):
    nb, t, d = x3.shape
    row_layout = mem.ndim == 2
    if row_layout:
        mem_spec = pl.BlockSpec((n_mem * 2 * MEM_HEADS, HEAD_DIM), lambda b, i: (b, 0))
    else:
        mem_spec = pl.BlockSpec((1, n_mem, mem.shape[2]), lambda b, i: (b, 0, 0))
    body = functools.partial(_cross_body, scale=1.0 / math.sqrt(HEAD_DIM), n_mem=n_mem, row_layout=row_layout)
    return pl.pallas_call(
        body, grid=(nb, t // tm),
        in_specs=[pl.BlockSpec((1, tm, d), lambda b, i: (b, i, 0)),
                  pl.BlockSpec((1, d), lambda b, i: (0, 0)),
                  _resident(wq.shape), _resident(wo.shape),
                  mem_spec],
        out_specs=pl.BlockSpec((1, tm, d), lambda b, i: (b, i, 0)),
        out_shape=jax.ShapeDtypeStruct((nb, t, d), F32),
        compiler_params=_params("parallel", "parallel"),
        name="cross_attn",
    )(x3, g.reshape(1, d), wq, wo, mem)


def _mlp_body(x_ref, g_ref, wu_ref, wd_ref, gf_ref, o_ref, h_ref, acc_ref):
    f = pl.program_id(1)

    @pl.when(f == 0)
    def _():
        h_ref[...] = _rms(x_ref[...], g_ref[...]).astype(BF16)
        acc_ref[...] = jnp.zeros_like(acc_ref)

    u = jnp.maximum(jnp.dot(h_ref[...], wu_ref[...], preferred_element_type=F32), 0.0)
    acc_ref[...] += jnp.dot((u * u).astype(BF16), wd_ref[...], preferred_element_type=F32)

    @pl.when(f == pl.num_programs(1) - 1)
    def _():
        o_ref[...] = _rms(x_ref[...] + acc_ref[...], gf_ref[...])


def _mlp_final(x, g, wu, wd, gf, tm, tf):
    m, d = x.shape
    dff = wu.shape[1]
    return pl.pallas_call(
        _mlp_body, grid=(m // tm, dff // tf),
        in_specs=[pl.BlockSpec((tm, d), lambda i, f: (i, 0)),
                  pl.BlockSpec((1, d), lambda i, f: (0, 0)),
                  pl.BlockSpec((d, tf), lambda i, f: (0, f)),
                  pl.BlockSpec((tf, d), lambda i, f: (f, 0)),
                  pl.BlockSpec((1, d), lambda i, f: (0, 0))],
        out_specs=pl.BlockSpec((tm, d), lambda i, f: (i, 0)),
        out_shape=jax.ShapeDtypeStruct((m, d), F32),
        scratch_shapes=[pltpu.VMEM((tm, d), BF16), pltpu.VMEM((tm, d), F32)],
        compiler_params=_params("parallel", "arbitrary"),
        name="mlp_final",
    )(x, g.reshape(1, d), wu, wd, gf.reshape(1, d))


def _w1cat(w1):
    half = CMP_STRIDE * HEAD_DIM
    return jnp.concatenate([w1[:half], w1[half:]], axis=-1).astype(BF16)


def kernel(x_prompt, x_sample, cache_sb_kv, cache_nsa_kv, cache_win_kv, cache_mem_kv, page_table, mem_prompt,
           norm_mix, w_in, norm_sb_out, norm_nsa_out, w_out, rel_bias_table,
           cmp_pe_k, cmp_pe_v, w_cmp_k1, w_cmp_k2, w_cmp_v1, w_cmp_v2,
           norm_cross, norm_mem, w_cross_q, w_mem_kv, w_cross_o,
           norm_ffn, w_up, w_down, norm_final):
    batch, seq, d = x_prompt.shape
    db, dec_seq, _ = x_sample.shape
    assert dec_seq == 1
    depth = w_in.shape[0]
    assert depth == 1, "the final norm is fused into the MLP of the only layer"
    n_pool = cache_sb_kv.shape[1]
    n_pages = page_table.shape[1]
    past = n_pages * PAGE_SIZE
    sb_heads = cache_sb_kv.shape[4]
    kv_heads = cache_nsa_kv.shape[4]
    nsa_heads = kv_heads * NSA_GROUP
    sbw = sb_heads * HEAD_DIM
    nsw = nsa_heads * HEAD_DIM
    kvw = kv_heads * HEAD_DIM
    n_gates = 3 * nsa_heads
    w_buf = cache_win_kv.shape[2]
    win_keep = min(WINDOW, seq)
    n_mem = mem_prompt.shape[1]
    mem_w = MEM_HEADS * HEAD_DIM
    m = batch * seq
    l = 0

    w_in_p = jnp.pad(w_in[l].astype(BF16), ((0, 0), (0, LANES - n_gates)))
    w1cat = jnp.stack([_w1cat(w_cmp_k1[l]), _w1cat(w_cmp_v1[l])])
    w2 = jnp.stack([w_cmp_k2[l], w_cmp_v2[l]]).astype(BF16)
    pe = jnp.stack([cmp_pe_k[l].reshape(1, -1), cmp_pe_v[l].reshape(1, -1)])
    pe_term = _pe_term(jnp.broadcast_to(pe, (2, SUBLANES, pe.shape[-1])), jnp.stack([w_cmp_k1[l], w_cmp_v1[l]]))
    w_out_b = w_out[l].astype(BF16)

    xp = x_prompt.reshape(m, d)
    (qsb, sbkv, sbkv16, qns, nsakv, nsakv16, winkv, winkv16, gates) = _project(xp, norm_mix[l], w_in_p, 256)
    o_sb = _sb_prompt(qsb, sbkv16, norm_sb_out[l], batch, seq)

    pages_seq = seq // PAGE_SIZE
    prompt_pages = jnp.arange(batch * pages_seq, dtype=jnp.int32).reshape(batch, pages_seq)
    kcvc = _compress(nsakv.reshape(batch * pages_seq, PAGE_SIZE, 4 * kvw), prompt_pages, w1cat, w2, pe_term,
                     kv_heads, pages_seq)
    nq = seq // LANES
    tsel = _bias_tiles(rel_bias_table, nq, LANES, 1, 0)
    tcmp = _bias_tiles(rel_bias_table, nq, LANES, CMP_STRIDE, CMP_LEN - CMP_STRIDE - 1)
    gates_k = jnp.pad(gates[:, :n_gates].reshape(m, kv_heads, 3 * NSA_GROUP).transpose(1, 0, 2),
                      ((0, 0), (0, 0), (0, LANES - 3 * NSA_GROUP)))
    twin = _bias_tiles(rel_bias_table, WINDOW // LANES + 1, LANES, 1, 0, window=True)
    o_ns = _nsa_prompt(qns, kcvc, nsakv16, winkv16, tsel, twin, tcmp, gates_k, norm_nsa_out[l], batch, seq)
    xp = _out_proj(xp, o_sb, o_ns, w_out_b, 512)

    sb_kv_prompt = sbkv.reshape(1, batch, seq, 2, sb_heads, HEAD_DIM)
    nsa_kv_prompt = nsakv.reshape(1, batch, seq, 4, kv_heads, HEAD_DIM)
    win_kv_prompt = winkv.reshape(1, batch, seq, 2, kv_heads, HEAD_DIM)[:, :, seq - win_keep:]

    xs = x_sample.reshape(db, d)
    (qsb_s, sbkv_s, _, qns_s, nsakv_s, _, winkv_s, _, gates_s) = _project(xs, norm_mix[l], w_in_p, db)
    o_sb_s = _sb_decode(qsb_s.reshape(db, sb_heads, HEAD_DIM), cache_sb_kv[l].reshape(-1, HEAD_DIM), page_table,
                        norm_sb_out[l], 16)

    nsa_rows = cache_nsa_kv[l].reshape(-1, HEAD_DIM)
    kcvc_s = _compress(nsa_rows, page_table, w1cat, w2, pe_term, kv_heads, 16)
    n_chunks_s = past // CMP_STRIDE
    tk = past + 1
    n_cmp_s = (tk - CMP_LEN) // CMP_STRIDE + 1
    n_blk_s = -(-tk // SEL_BLOCK)
    n_cache_blk = past // SEL_BLOCK
    assert n_blk_s == n_cache_blk + 1 and n_cache_blk >= N_SEL
    nj = -(-n_blk_s // LANES) * LANES
    ind_s, _ = _sel_constants(n_chunks_s, n_blk_s, n_cmp_s, 0)
    ind_s_pad = np.zeros((nj, n_chunks_s), np.float32)
    ind_s_pad[:n_blk_s] = ind_s
    ind3_s = jnp.asarray(np.concatenate([ind_s_pad.T] * 3, axis=0), BF16)
    q_dec = qns_s.reshape(db, kv_heads, NSA_GROUP, HEAD_DIM)
    ocmp_s, scores = _cmp_decode(rel_bias_table, q_dec, kcvc_s, ind3_s, past, kv_heads)
    n_pick_cache = N_SEL - 1
    picks = _topk(scores[:, :kv_heads].reshape(db * kv_heads, nj), n_cache_blk, n_pick_cache)
    picks = picks[:, :n_pick_cache].reshape(-1)
    gates_d = gates_s[:, :n_gates].reshape(db, kv_heads, NSA_GROUP, 3).transpose(0, 1, 3, 2)[..., None]
    o_ns_s = _sel_decode(page_table, picks, nsa_rows, nsakv_s.reshape(db, 1, 4 * kvw),
                         cache_win_kv[l].reshape(-1, HEAD_DIM), winkv_s.reshape(db, 1, 2 * kvw), q_dec, gates_d,
                         ocmp_s.reshape(db, kv_heads, NSA_GROUP, HEAD_DIM), norm_nsa_out[l],
                         rel_bias_table, n_pick_cache, past, w_buf)
    xs = _out_proj(xs, o_sb_s.reshape(db, sbw).astype(BF16), o_ns_s.reshape(db, nsw).astype(BF16), w_out_b, db)

    sb_kv_sample = sbkv_s.reshape(1, db, 1, 2, sb_heads, HEAD_DIM)
    nsa_kv_sample = nsakv_s.reshape(1, db, 1, 4, kv_heads, HEAD_DIM)
    win_new = winkv_s.reshape(db, 1, 2, kv_heads, HEAD_DIM)
    win_kv_sample = jnp.concatenate([cache_win_kv[l], win_new], axis=1)[None, :, 1:]

    mem_kv = _norm_matmul(mem_prompt.reshape(batch * n_mem, d), norm_mem[l], w_mem_kv[l].astype(BF16), n_mem)
    wq = w_cross_q[l].astype(BF16)
    wo = w_cross_o[l].astype(BF16)
    xp = _cross(xp.reshape(batch, seq, d), norm_cross[l], wq, wo, mem_kv.reshape(batch, n_mem, 2 * mem_w),
                n_mem, 512)
    xs = _cross(xs.reshape(db, 1, d), norm_cross[l], wq, wo, cache_mem_kv[l].reshape(-1, HEAD_DIM), n_mem, 1)
    mem_kv_prompt = mem_kv.reshape(1, batch, n_mem, 2, MEM_HEADS, HEAD_DIM)

    wu = w_up[l].astype(BF16)
    wd = w_down[l].astype(BF16)
    y_prompt = _mlp_final(xp.reshape(m, d), norm_ffn[l], wu, wd, norm_final, 512, 1024).reshape(batch, seq, d)
    y_sample = _mlp_final(xs.reshape(db, d), norm_ffn[l], wu, wd, norm_final, db, 1024).reshape(db, 1, d)

    return (y_prompt, y_sample, sb_kv_prompt, sb_kv_sample, nsa_kv_prompt, nsa_kv_sample,
            win_kv_prompt, win_kv_sample, mem_kv_prompt)
```

```python
import functools
import math

import numpy as np
import jax
import jax.numpy as jnp
from jax import lax
from jax.experimental import pallas as pl
from jax.experimental.pallas import tpu as pltpu

HEAD_DIM = 128
NSA_GROUP = 4
MEM_HEADS = 4
PAGE_SIZE = 128
CMP_LEN = 32
CMP_STRIDE = 16
SEL_BLOCK = 64
N_SEL = 16
WINDOW = 512
N_BUCKETS = 32
MAX_DISTANCE = 1024
RMS_EPS = 1e-6
NEG_INF = -1e30
BIG = 1e30
LANES = 128
SUBLANES = 8
VMEM_LIMIT = 56 * 1024 * 1024
_PLANE_PITCH = PAGE_SIZE + SUBLANES

F32 = jnp.float32
BF16 = jnp.bfloat16
_NT = (((1,), (1,)), ((), ()))


def _bucket_thresholds():
    max_exact = N_BUCKETS // 2
    n_log = N_BUCKETS - max_exact
    ratio = MAX_DISTANCE // max_exact
    th = list(range(max_exact)) + [max_exact]
    for m in range(1, n_log):
        n = th[-1]
        while n ** n_log < (max_exact ** n_log) * (ratio ** m):
            n += 1
        th.append(n)
    return tuple(th)


THRESHOLDS = _bucket_thresholds()


def _bias_chain(rel, tab_ref, h):
    v = jnp.full(rel.shape, tab_ref[0, h], F32)
    for b in range(1, N_BUCKETS):
        v = jnp.where(rel >= THRESHOLDS[b], tab_ref[b, h], v)
    return v


def _rms(x, g):
    return x * lax.rsqrt(jnp.mean(x * x, axis=-1, keepdims=True) + RMS_EPS) * g


def _softplus(z):
    return jnp.maximum(z, 0.0) + jnp.log(1.0 + jnp.exp(-jnp.abs(z)))


def _split2(x):
    hi = x.astype(BF16)
    lo = (x - hi.astype(F32)).astype(BF16)
    return hi, lo


def _split3(x):
    hi = x.astype(BF16)
    r = x - hi.astype(F32)
    mid = r.astype(BF16)
    lo = (r - mid.astype(F32)).astype(BF16)
    return hi, mid, lo


def _masked_softmax(lg, valid):
    lgm = jnp.where(valid, lg, NEG_INF)
    m = jnp.max(lgm, axis=-1, keepdims=True)
    e = jnp.where(valid, jnp.exp(lgm - m), 0.0)
    s = jnp.sum(e, axis=-1, keepdims=True)
    return e / jnp.where(s > 0.0, s, 1.0)


def _params(*sem):
    return pltpu.CompilerParams(dimension_semantics=sem, vmem_limit_bytes=VMEM_LIMIT)


def _resident(shape):
    nd = len(shape)
    return pl.BlockSpec(shape, lambda *_: (0,) * nd, pipeline_mode=pl.Buffered(1))


def _proj_body(x_ref, g_ref, w_ref, qsb_ref, sbkv_ref, sbkv16_ref, qns_ref, nsakv_ref,
               nsakv16_ref, winkv_ref, winkv16_ref, gates_ref, *, sbw, nsw, kvw, scale):
    hb = _rms(x_ref[...], g_ref[...]).astype(BF16)

    def mm(c0, n):
        return jnp.dot(hb, w_ref[:, c0:c0 + n], preferred_element_type=F32)

    c = 0
    qsb_ref[...] = (mm(c, sbw) * scale).astype(BF16)
    c += sbw
    for half in range(2):
        kv = mm(c, sbw)
        sbkv_ref[:, half * sbw:(half + 1) * sbw] = kv
        sbkv16_ref[:, half * sbw:(half + 1) * sbw] = kv.astype(BF16)
        c += sbw
    qns_ref[...] = (mm(c, nsw) * scale).astype(BF16)
    c += nsw
    kv = mm(c, 4 * kvw)
    nsakv_ref[...] = kv
    nsakv16_ref[...] = kv.astype(BF16)
    c += 4 * kvw
    kv = mm(c, 2 * kvw)
    winkv_ref[...] = kv
    winkv16_ref[...] = kv.astype(BF16)
    c += 2 * kvw
    gates_ref[...] = jax.nn.sigmoid(mm(c, LANES))


def _project(x, g, w_pad, tm):
    m, d = x.shape
    sbw = nsw = d // 2
    kvw = nsw // NSA_GROUP
    widths = (sbw, 2 * sbw, 2 * sbw, nsw, 4 * kvw, 4 * kvw, 2 * kvw, 2 * kvw, LANES)
    dtypes = (BF16, F32, BF16, BF16, F32, BF16, F32, BF16, F32)
    body = functools.partial(_proj_body, sbw=sbw, nsw=nsw, kvw=kvw, scale=1.0 / math.sqrt(HEAD_DIM))
    return pl.pallas_call(
        body,
        grid=(m // tm,),
        in_specs=[pl.BlockSpec((tm, d), lambda i: (i, 0)),
                  pl.BlockSpec((1, d), lambda i: (0, 0)),
                  _resident(w_pad.shape)],
        out_specs=[pl.BlockSpec((tm, w), lambda i: (i, 0)) for w in widths],
        out_shape=[jax.ShapeDtypeStruct((m, w), dt) for w, dt in zip(widths, dtypes)],
        compiler_params=_params("parallel"),
        name="in_proj",
    )(x, g.reshape(1, d), w_pad)


def _bias_tile_body(tab_ref, o_ref, *, q_stride, k_stride, k_off, window, n_heads):
    m = pl.program_id(0)
    i = lax.broadcasted_iota(jnp.int32, (LANES, LANES), 0)
    j = lax.broadcasted_iota(jnp.int32, (LANES, LANES), 1)
    rel = m * q_stride + i - (k_stride * j + k_off)
    for h in range(n_heads):
        v = _bias_chain(rel, tab_ref, h)
        if window:
            v = jnp.where((rel >= 0) & (rel < WINDOW), v, NEG_INF)
        o_ref[0, h] = v


def _bias_tiles(table, n_tiles, q_stride, k_stride, k_off, window=False):
    n_heads = table.shape[1]
    body = functools.partial(_bias_tile_body, q_stride=q_stride, k_stride=k_stride, k_off=k_off, window=window,
                             n_heads=n_heads)
    out = pl.pallas_call(
        body,
        grid=(n_tiles,),
        in_specs=[pl.BlockSpec(memory_space=pltpu.SMEM)],
        out_specs=pl.BlockSpec((1, n_heads, LANES, LANES), lambda m: (m, 0, 0, 0)),
        out_shape=jax.ShapeDtypeStruct((n_tiles, n_heads, LANES, LANES), F32),
        compiler_params=_params("parallel"),
        name="bias_tiles",
    )(table)
    return out.reshape(n_tiles, n_heads // NSA_GROUP, NSA_GROUP * LANES, LANES)


def _cumsum_rhs(t):
    u = np.tril(np.ones((t, t), np.float32))
    half = np.concatenate([u, np.ones((t, t), np.float32)], axis=1)
    return jnp.asarray(np.concatenate([half, half], axis=0), BF16)


def _sb_tile(q, k, v, uo, carry, acc, valid):
    t = q.shape[0]
    sub = uo.shape[1] // 2
    z = lax.dot_general(q, k, _NT, preferred_element_type=F32)
    sp = _softplus(z)
    if valid is not None:
        sp = jnp.where(valid, sp, 0.0)
    parts = []
    for s in reversed(range(t // sub)):
        hi, lo = _split2(sp[:, s * sub:(s + 1) * sub])
        ct = jnp.dot(jnp.concatenate([hi, lo], axis=1), uo, preferred_element_type=F32)
        parts.insert(0, jnp.exp(z[:, s * sub:(s + 1) * sub] - (ct[:, :sub] + carry)))
        carry = carry + ct[:, sub:]
    a = jnp.concatenate(parts, axis=1)
    if valid is not None:
        a = jnp.where(valid, a, 0.0)
    acc = acc + jnp.dot(a.astype(BF16), v, preferred_element_type=F32)
    return carry, acc


def _sb_prompt_body(q_ref, k_ref, v_ref, g_ref, uo_ref, o_ref, *, tq, hps):
    qi = pl.program_id(2)
    uo = uo_ref[...]
    row = lax.broadcasted_iota(jnp.int32, (tq, tq), 0)
    col = lax.broadcasted_iota(jnp.int32, (tq, tq), 1)
    lanes = [slice(h * HEAD_DIM, (h + 1) * HEAD_DIM) for h in range(hps)]
    qs = [q_ref[:, sl] for sl in lanes]

    def tiles(start, state, valid):
        out = []
        for h, sl in enumerate(lanes):
            out.extend(_sb_tile(qs[h], k_ref[pl.ds(start, tq), sl], v_ref[pl.ds(start, tq), sl], uo,
                                state[2 * h], state[2 * h + 1], valid))
        return tuple(out)

    state = (jnp.zeros((tq, uo.shape[1] // 2), F32), jnp.zeros((tq, HEAD_DIM), F32)) * hps
    state = tiles(pl.multiple_of(qi * tq, tq), state, col < row)
    state = lax.fori_loop(0, qi, lambda t, st: tiles(pl.multiple_of((qi - 1 - t) * tq, tq), st, None), state)
    for h, sl in enumerate(lanes):
        o_ref[:, sl] = _rms(state[2 * h + 1], g_ref[0, :, sl]).astype(BF16)


def _sb_prompt(qsb, sbkv16, gain, batch, seq):
    m, sbw = qsb.shape
    heads = sbw // HEAD_DIM
    tq = 2 * LANES
    hps = 8
    nq = seq // tq
    wide = hps * HEAD_DIM
    groups = heads // hps
    return pl.pallas_call(
        functools.partial(_sb_prompt_body, tq=tq, hps=hps),
        grid=(batch, groups, nq),
        in_specs=[pl.BlockSpec((tq, wide), lambda b, h, i: (b * nq + i, h)),
                  pl.BlockSpec((seq, wide), lambda b, h, i: (b, h)),
                  pl.BlockSpec((seq, wide), lambda b, h, i: (b, groups + h)),
                  pl.BlockSpec((1, 1, wide), lambda b, h, i: (h, 0, 0)),
                  pl.BlockSpec((2 * LANES, 2 * LANES), lambda b, h, i: (0, 0))],
        out_specs=pl.BlockSpec((tq, wide), lambda b, h, i: (b * nq + i, h)),
        out_shape=jax.ShapeDtypeStruct((m, sbw), BF16),
        compiler_params=_params("parallel", "parallel", "arbitrary"),
        name="sb_prompt",
    )(qsb, sbkv16, sbkv16, gain.reshape(groups, 1, wide), _cumsum_rhs(LANES))


def _sb_decode_body(pt_ref, *refs, n_pages_step, n_steps, heads):
    pages = refs[:n_pages_step]
    q_ref, uo_ref, g_ref, o_ref, acc_ref, carry_ref = refs[n_pages_step:n_pages_step + 6]
    plane_refs = refs[n_pages_step + 6:]
    ppg = n_pages_step // len(plane_refs)
    s = pl.program_id(1)
    rows_tok = 2 * heads
    pitch = _PLANE_PITCH

    @pl.when(s == 0)
    def _():
        acc_ref[...] = jnp.zeros_like(acc_ref)
        carry_ref[...] = jnp.zeros_like(carry_ref)

    for p in range(n_pages_step):
        for tok in range(PAGE_SIZE):
            for kv in range(2):
                plane_refs[p // ppg][pl.ds(((p % ppg) * rows_tok + kv * heads) * pitch + tok, heads,
                                           stride=pitch), :] = pages[p][pl.ds(tok * rows_tok + kv * heads, heads), :]

    def plane(p, r):
        return plane_refs[p // ppg][pl.ds(((p % ppg) * rows_tok + r) * pitch, PAGE_SIZE), :].astype(BF16)

    q = q_ref[0]
    uo = uo_ref[...]
    rowi = lax.broadcasted_iota(jnp.int32, (heads, HEAD_DIM), 0)
    zs = []
    for p in range(n_pages_step):
        z = jnp.zeros((heads, PAGE_SIZE), F32)
        for h in range(heads):
            z = jnp.where(rowi == h, lax.dot_general(q, plane(p, h), _NT, preferred_element_type=F32), z)
        zs.append(z)
    cts = []
    for z in zs:
        hi, lo = _split2(_softplus(z))
        cts.append(jnp.dot(jnp.concatenate([hi, lo], axis=1), uo, preferred_element_type=F32))
    carry = carry_ref[...]
    acc = acc_ref[...]
    for p in range(n_pages_step):
        a = jnp.exp(zs[p] - (cts[p][:, :PAGE_SIZE] + carry)).astype(BF16)
        carry = carry + cts[p][:, PAGE_SIZE:]
        for h in range(heads):
            acc = acc + jnp.where(rowi == h, jnp.dot(a, plane(p, heads + h), preferred_element_type=F32), 0.0)
    acc_ref[...] = acc
    carry_ref[...] = carry

    @pl.when(s == n_steps - 1)
    def _():
        o_ref[0] = _rms(acc_ref[...], g_ref[...])


def _sb_decode(q, cache_rows, page_table, gain, pages_per_step):
    db, heads, _ = q.shape
    n_pages = page_table.shape[1]
    n_steps = n_pages // pages_per_step
    page_rows = PAGE_SIZE * 2 * heads
    pages_group = min(4, pages_per_step)

    def page_spec(p):
        def idx(b, s, pt):
            return (pt[b * n_pages + n_pages - 1 - (s * pages_per_step + p)], 0)
        return pl.BlockSpec((page_rows, HEAD_DIM), idx)

    grid_spec = pltpu.PrefetchScalarGridSpec(
        num_scalar_prefetch=1,
        grid=(db, n_steps),
        in_specs=[page_spec(p) for p in range(pages_per_step)] + [
            pl.BlockSpec((1, heads, HEAD_DIM), lambda b, s, pt: (b, 0, 0)),
            pl.BlockSpec((2 * PAGE_SIZE, 2 * PAGE_SIZE), lambda b, s, pt: (0, 0)),
            pl.BlockSpec((heads, HEAD_DIM), lambda b, s, pt: (0, 0))],
        out_specs=pl.BlockSpec((1, heads, HEAD_DIM), lambda b, s, pt: (b, 0, 0)),
        scratch_shapes=[pltpu.VMEM((heads, HEAD_DIM), F32), pltpu.VMEM((heads, PAGE_SIZE), F32)]
        + [pltpu.VMEM((pages_group * 2 * heads * _PLANE_PITCH, HEAD_DIM), F32)
           for _ in range(pages_per_step // pages_group)])
    body = functools.partial(_sb_decode_body, n_pages_step=pages_per_step, n_steps=n_steps, heads=heads)
    return pl.pallas_call(
        body, grid_spec=grid_spec,
        out_shape=jax.ShapeDtypeStruct((db, heads, HEAD_DIM), F32),
        compiler_params=_params("parallel", "arbitrary"),
        name="sb_decode",
    )(page_table.reshape(-1), *([cache_rows] * pages_per_step), q, _cumsum_rhs(PAGE_SIZE),
      gain.reshape(heads, HEAD_DIM))


def _pe_term_body(pe_ref, w_ref, o_ref):
    o_ref[0] = jnp.dot(pe_ref[0].astype(BF16), w_ref[0].astype(BF16), preferred_element_type=F32)


def _pe_term(pe, w1):
    n, _, kdim = pe.shape
    hid = w1.shape[-1]
    return pl.pallas_call(
        _pe_term_body, grid=(n,),
        in_specs=[pl.BlockSpec((1, SUBLANES, kdim), lambda t: (t, 0, 0)),
                  pl.BlockSpec((1, kdim, hid), lambda t: (t, 0, 0))],
        out_specs=pl.BlockSpec((1, SUBLANES, hid), lambda t: (t, 0, 0)),
        out_shape=jax.ShapeDtypeStruct((n, SUBLANES, hid), F32),
        compiler_params=_params("parallel"),
        name="cmp_pe_term",
    )(pe, w1)


def _compress_body(pt_ref, *refs, n_pages_step, kv_heads, row_layout):
    pages = refs[:n_pages_step]
    w1_ref, w2_ref, pe_ref, o_ref, carry_ref = refs[n_pages_step:n_pages_step + 5]
    tok_refs = refs[n_pages_step + 5:]
    chunks_page = PAGE_SIZE // CMP_STRIDE
    rows = n_pages_step * chunks_page
    rows_tok = 4 * kv_heads
    pitch = _PLANE_PITCH

    @pl.when(pl.program_id(1) == 0)
    def _():
        carry_ref[...] = jnp.zeros_like(carry_ref)

    for t in range(2):
        for k in range(kv_heads):
            idx = t * kv_heads + k
            for p in range(n_pages_step):
                for v in range(PAGE_SIZE // SUBLANES):
                    if row_layout:
                        x = pages[p][pl.ds(v * SUBLANES * rows_tok + idx, SUBLANES, stride=rows_tok), :]
                    else:
                        x = pages[p][0, v * SUBLANES:(v + 1) * SUBLANES, idx * HEAD_DIM:(idx + 1) * HEAD_DIM]
                    chunk, l0 = divmod(v * SUBLANES, CMP_STRIDE)
                    tok_refs[t][pl.ds((k * CMP_STRIDE + l0) * pitch + p * chunks_page + chunk, SUBLANES,
                                      stride=pitch), :] = x

    rowi = lax.broadcasted_iota(jnp.int32, (rows, HEAD_DIM), 0)
    for t in range(2):
        x = jnp.concatenate(
            [jnp.concatenate([tok_refs[t][pl.ds((k * CMP_STRIDE + l) * pitch, rows), :].astype(BF16)
                              for l in range(CMP_STRIDE)], axis=1) for k in range(kv_heads)], axis=0)
        acc = jnp.dot(x, w1_ref[t], preferred_element_type=F32)
        hids = []
        for k in range(kv_heads):
            idx = t * kv_heads + k
            first = acc[k * rows:(k + 1) * rows, :HEAD_DIM]
            second = acc[k * rows:(k + 1) * rows, HEAD_DIM:]
            prev_first = jnp.where(rowi == 0, carry_ref[idx], pltpu.roll(first, 1, axis=0))
            carry_ref[idx] = first[rows - 1:rows, :]
            hids.append(jax.nn.gelu(prev_first + second + pe_ref[t, 0:1, :]).astype(BF16))
        out = jnp.dot(jnp.concatenate(hids, axis=0), w2_ref[t], preferred_element_type=F32)
        for k in range(kv_heads):
            c0 = (t * kv_heads + k) * HEAD_DIM
            o_ref[0, :, c0:c0 + HEAD_DIM] = out[k * rows:(k + 1) * rows]


def _compress(cache, page_table, w1cat, w2, pe_term, kv_heads, pages_per_step):
    nb, n_pages = page_table.shape
    n_steps = n_pages // pages_per_step
    chunks = n_pages * (PAGE_SIZE // CMP_STRIDE)
    rows_step = pages_per_step * (PAGE_SIZE // CMP_STRIDE)
    width = 2 * kv_heads * HEAD_DIM
    row_layout = cache.ndim == 2

    def page_spec(p):
        if row_layout:
            return pl.BlockSpec((PAGE_SIZE * 4 * kv_heads, HEAD_DIM),
                                lambda b, s, pt: (pt[b * n_pages + s * pages_per_step + p], 0))
        return pl.BlockSpec((1, PAGE_SIZE, width),
                            lambda b, s, pt: (pt[b * n_pages + s * pages_per_step + p], 0, 0))

    grid_spec = pltpu.PrefetchScalarGridSpec(
        num_scalar_prefetch=1,
        grid=(nb, n_steps),
        in_specs=[page_spec(p) for p in range(pages_per_step)] + [
            pl.BlockSpec(w1cat.shape, lambda b, s, pt: (0, 0, 0)),
            pl.BlockSpec(w2.shape, lambda b, s, pt: (0, 0, 0)),
            pl.BlockSpec(pe_term.shape, lambda b, s, pt: (0, 0, 0))],
        out_specs=pl.BlockSpec((1, rows_step, width), lambda b, s, pt: (b, s, 0)),
        scratch_shapes=[pltpu.VMEM((2 * kv_heads, 1, HEAD_DIM), F32),
                        pltpu.VMEM((kv_heads * CMP_STRIDE * _PLANE_PITCH, HEAD_DIM), F32),
                        pltpu.VMEM((kv_heads * CMP_STRIDE * _PLANE_PITCH, HEAD_DIM), F32)])
    assert rows_step <= PAGE_SIZE, "a step's chunks must fit one plane"
    body = functools.partial(_compress_body, n_pages_step=pages_per_step, kv_heads=kv_heads,
                             row_layout=row_layout)
    return pl.pallas_call(
        body, grid_spec=grid_spec,
        out_shape=jax.ShapeDtypeStruct((nb, chunks, width), F32),
        compiler_params=_params("parallel", "arbitrary"),
        name="nsa_compress",
    )(page_table.reshape(-1), *([cache] * pages_per_step), w1cat, w2, pe_term)


def _nsa_prompt_body(q_ref, kc_ref, vc_ref, ks_ref, vs_ref, kw_ref, vw_ref, tsel_ref, twin_ref, tcmp_ref,
                     ind_ref, exp_ref, gates_ref, gn_ref, o_ref, mask_ref, *, n_blk, n_pick, n_kt, nk):
    i = pl.program_id(1)
    t = LANES
    rows = NSA_GROUP * t
    gw = NSA_GROUP * HEAD_DIM
    heads = range(nk)

    def hd(k):
        return slice(k * HEAD_DIM, (k + 1) * HEAD_DIM)

    qs = [jnp.concatenate([q_ref[:, k * gw + g * HEAD_DIM:k * gw + (g + 1) * HEAD_DIM] for g in range(NSA_GROUP)],
                          axis=0) for k in heads]
    qrow = lax.broadcasted_iota(jnp.int32, (rows, t), 0) & (t - 1)
    col = lax.broadcasted_iota(jnp.int32, (rows, t), 1)
    qpos = i * t + qrow
    valid_c = (col >= 1) & (qpos >= CMP_STRIDE * col + (CMP_LEN - CMP_STRIDE - 1))
    jj = lax.broadcasted_iota(jnp.int32, (n_blk, t), 0)
    qq = i * t + lax.broadcasted_iota(jnp.int32, (n_blk, t), 1)
    qblk = qq // SEL_BLOCK
    forced = (jj == 0) | (jj == qblk) | (jj == qblk - 1)
    eligible = jj * SEL_BLOCK <= qq

    o_cmps = []
    for k in heads:
        lc = (lax.dot_general(qs[k], kc_ref[0, :, hd(k)].astype(BF16), _NT, preferred_element_type=F32)
              + tcmp_ref[0, k])
        pc = _masked_softmax(lc, valid_c)
        o_cmps.append(jnp.dot(pc.astype(BF16), vc_ref[0, :, hd(k)].astype(BF16), preferred_element_type=F32))

        pcs = pc[0:t]
        for g in range(1, NSA_GROUP):
            pcs = pcs + pc[g * t:(g + 1) * t]
        p3 = jnp.concatenate(_split3(pcs), axis=1)
        score = lax.dot_general(ind_ref[...], p3, _NT, preferred_element_type=F32)[:n_blk]
        score = jnp.where(forced, BIG, jnp.where(eligible, score, NEG_INF))
        rank = jnp.zeros((n_blk, t), jnp.int32)
        for jp in range(n_blk):
            r = score[jp:jp + 1, :]
            beats = (r > score) | ((r == score) & (jj > jp))
            rank = rank + beats.astype(jnp.int32)
        sel_t = jnp.concatenate([(rank < n_pick).astype(F32), jnp.zeros((t - n_blk, t), F32)], axis=0)
        maskf = jnp.dot(sel_t.T.astype(BF16), exp_ref[...], preferred_element_type=F32)
        for kt in range(n_kt):
            mask_ref[k, kt] = (maskf[:, kt * t:(kt + 1) * t] - 1.0) * BIG


    two = 2 * t
    qrow2 = lax.broadcasted_iota(jnp.int32, (rows, two), 0) & (t - 1)
    col2 = lax.broadcasted_iota(jnp.int32, (rows, two), 1)

    def sel_logits(k, jg, diag):
        s = pl.multiple_of(jg * two, two)
        d0 = i - 2 * jg
        bias = jnp.concatenate([tsel_ref[d0, k], tsel_ref[jnp.maximum(d0 - 1, 0), k]], axis=1)
        picked = jnp.concatenate([mask_ref[k, 2 * jg], mask_ref[k, 2 * jg + 1]], axis=1)
        lg = (lax.dot_general(qs[k], ks_ref[pl.ds(s, two), hd(k)], _NT, preferred_element_type=F32)
              + (bias + jnp.concatenate([picked] * NSA_GROUP, axis=0)))
        if diag:
            lg = jnp.where(jg * two + col2 <= i * t + qrow2, lg, NEG_INF)
        return lg

    def fold(x):
        return jnp.maximum(x[:, :t], x[:, t:])

    jd = i // 2
    m_vecs = tuple(fold(sel_logits(k, jd, True)) for k in heads)
    m_vecs = lax.fori_loop(
        0, jd, lambda jg, mv: tuple(jnp.maximum(mv[k], fold(sel_logits(k, jg, False))) for k in heads), m_vecs)
    m_b2s = []
    for k in heads:
        m_b = jnp.broadcast_to(jnp.max(m_vecs[k], axis=-1, keepdims=True), (rows, t))
        m_b2s.append(jnp.concatenate([m_b, m_b], axis=1))

    def sel_weights(jg, carry, diag):
        out = []
        for k in heads:
            p = jnp.exp(sel_logits(k, jg, diag) - m_b2s[k])
            s = pl.multiple_of(jg * two, two)
            out.append(carry[2 * k] + (p[:, :t] + p[:, t:]))
            out.append(carry[2 * k + 1]
                       + jnp.dot(p.astype(BF16), vs_ref[pl.ds(s, two), hd(k)], preferred_element_type=F32))
        return tuple(out)

    carry = sel_weights(jd, (jnp.zeros((rows, t), F32), jnp.zeros((rows, HEAD_DIM), F32)) * nk, True)
    carry = lax.fori_loop(0, jd, lambda jg, c: sel_weights(jg, c, False), carry)
    o_sels = [carry[2 * k + 1] / jnp.sum(carry[2 * k], axis=-1, keepdims=True) for k in heads]

    n_back = WINDOW // t
    for k in heads:
        lgs = []
        starts = []
        for w in range(n_back + 1):
            s = pl.multiple_of(jnp.maximum(i - w, 0) * t, t)
            lg = (lax.dot_general(qs[k], kw_ref[pl.ds(s, t), hd(k)], _NT, preferred_element_type=F32)
                  + twin_ref[w, k])
            if w > 0:
                lg = lg + jnp.where(i >= w, 0.0, NEG_INF)
            lgs.append(lg)
            starts.append(s)
        m_vec = lgs[0]
        for lg in lgs[1:]:
            m_vec = jnp.maximum(m_vec, lg)
        m_b = jnp.broadcast_to(jnp.max(m_vec, axis=-1, keepdims=True), (rows, t))
        l_vec = jnp.zeros((rows, t), F32)
        acc = jnp.zeros((rows, HEAD_DIM), F32)
        for lg, s in zip(lgs, starts):
            p = jnp.exp(lg - m_b)
            l_vec = l_vec + p
            acc = acc + jnp.dot(p.astype(BF16), vw_ref[pl.ds(s, t), hd(k)], preferred_element_type=F32)
        o_win = acc / jnp.sum(l_vec, axis=-1, keepdims=True)

        gates = gates_ref[...]
        gn = gn_ref[k]
        for g in range(NSA_GROUP):
            sl = slice(g * t, (g + 1) * t)
            c = 3 * (k * NSA_GROUP + g)
            o = (gates[:, c:c + 1] * o_cmps[k][sl] + gates[:, c + 1:c + 2] * o_sels[k][sl]
                 + gates[:, c + 2:c + 3] * o_win[sl])
            o_ref[:, k * gw + g * HEAD_DIM:k * gw + (g + 1) * HEAD_DIM] = (
                _rms(o, gn[:, g * HEAD_DIM:(g + 1) * HEAD_DIM]).astype(BF16))


def _sel_constants(n_chunks, n_blk, n_cmp, key_len):
    blk = np.arange(n_blk)
    lo = np.clip((blk * SEL_BLOCK - CMP_LEN) // CMP_STRIDE + 1, 0, n_cmp)
    hi = np.clip(-(-((blk + 1) * SEL_BLOCK) // CMP_STRIDE), 0, n_cmp)
    c = np.arange(n_chunks) - 1
    ind = ((c[None, :] >= lo[:, None]) & (c[None, :] < hi[:, None]) & (c[None, :] >= 0)).astype(np.float32)
    expand = (np.arange(key_len)[None, :] // SEL_BLOCK == blk[:, None]).astype(np.float32)
    return ind, expand


def _nsa_prompt(qns, kcvc, nsakv16, winkv16, tsel, twin, tcmp, gates_k, gain, batch, seq):
    m, nsw = qns.shape
    kv_heads = nsw // (NSA_GROUP * HEAD_DIM)
    t = LANES
    nq = seq // t
    n_chunks = seq // CMP_STRIDE
    assert n_chunks == t, "the compressed branch is tiled as a single 128-column tile"
    n_cmp = (seq - CMP_LEN) // CMP_STRIDE + 1
    n_blk = -(-seq // SEL_BLOCK)
    n_pick = min(N_SEL, n_blk)
    ind, expand = _sel_constants(n_chunks, n_blk, n_cmp, seq)
    ind_pad = np.zeros((t, n_chunks), np.float32)
    ind_pad[:n_blk] = ind
    exp_pad = np.zeros((t, seq), np.float32)
    exp_pad[:n_blk] = expand
    ind3 = jnp.asarray(np.concatenate([ind_pad] * 3, axis=1), BF16)
    gw = NSA_GROUP * HEAD_DIM
    nk = kv_heads
    kw = nk * HEAD_DIM
    body = functools.partial(_nsa_prompt_body, n_blk=n_blk, n_pick=n_pick, n_kt=nq, nk=nk)
    return pl.pallas_call(
        body,
        grid=(batch, nq),
        in_specs=[pl.BlockSpec((t, nk * gw), lambda b, i: (b * nq + i, 0)),
                  pl.BlockSpec((1, n_chunks, kw), lambda b, i: (b, 0, 0)),
                  pl.BlockSpec((1, n_chunks, kw), lambda b, i: (b, 0, 1)),
                  pl.BlockSpec((seq, kw), lambda b, i: (b, 2)),
                  pl.BlockSpec((seq, kw), lambda b, i: (b, 3)),
                  pl.BlockSpec((seq, kw), lambda b, i: (b, 0)),
                  pl.BlockSpec((seq, kw), lambda b, i: (b, 1)),
                  pl.BlockSpec((nq, nk, gw, t), lambda b, i: (0, 0, 0, 0)),
                  pl.BlockSpec((twin.shape[0], nk, gw, t), lambda b, i: (0, 0, 0, 0)),
                  pl.BlockSpec((1, nk, gw, t), lambda b, i: (i, 0, 0, 0)),
                  pl.BlockSpec((t, 3 * n_chunks), lambda b, i: (0, 0)),
                  pl.BlockSpec((t, seq), lambda b, i: (0, 0)),
                  pl.BlockSpec((t, LANES), lambda b, i: (b * nq + i, 0)),
                  pl.BlockSpec((nk, 1, gw), lambda b, i: (0, 0, 0))],
        out_specs=pl.BlockSpec((t, nk * gw), lambda b, i: (b * nq + i, 0)),
        out_shape=jax.ShapeDtypeStruct((m, nsw), BF16),
        scratch_shapes=[pltpu.VMEM((nk, nq, t, t), F32)],
        compiler_params=_params("parallel", "arbitrary"),
        name="nsa_prompt",
    )(qns, kcvc, kcvc, nsakv16, nsakv16, winkv16, winkv16, tsel, twin, tcmp, ind3,
      jnp.asarray(exp_pad, BF16), gates_k, gain.reshape(kv_heads, 1, gw))


def _cmp_decode_body(tab_ref, q_ref, kcvc_ref, ind_ref, ocmp_ref, score_ref, *, past, kv_heads):
    n_chunks = kcvc_ref.shape[1]
    r = lax.broadcasted_iota(jnp.int32, (1, n_chunks), 1)
    rel = past - (CMP_STRIDE * r + (CMP_LEN - CMP_STRIDE - 1))
    valid = (r >= 1) & (rel >= 0)
    score_ref[...] = jnp.zeros_like(score_ref)
    for k in range(kv_heads):
        kc = kcvc_ref[0, :, k * HEAD_DIM:(k + 1) * HEAD_DIM].astype(BF16)
        vc = kcvc_ref[0, :, (kv_heads + k) * HEAD_DIM:(kv_heads + k + 1) * HEAD_DIM].astype(BF16)
        bias = jnp.concatenate([_bias_chain(rel, tab_ref, k * NSA_GROUP + g) for g in range(NSA_GROUP)], axis=0)
        lg = lax.dot_general(q_ref[0, k], kc, _NT, preferred_element_type=F32) + bias
        p = _masked_softmax(lg, valid)
        ocmp_ref[0, k * NSA_GROUP:(k + 1) * NSA_GROUP, :] = jnp.dot(p.astype(BF16), vc, preferred_element_type=F32)
        p3 = jnp.concatenate(_split3(jnp.sum(p, axis=0, keepdims=True)), axis=1)
        score_ref[0, k:k + 1, :] = jnp.dot(p3, ind_ref[...], preferred_element_type=F32)


def _cmp_decode(table, q, kcvc, ind3, past, kv_heads):
    db = q.shape[0]
    heads = kv_heads * NSA_GROUP
    n_chunks = kcvc.shape[1]
    nj = ind3.shape[1]
    body = functools.partial(_cmp_decode_body, past=past, kv_heads=kv_heads)
    return pl.pallas_call(
        body, grid=(db,),
        in_specs=[pl.BlockSpec(memory_space=pltpu.SMEM),
                  pl.BlockSpec((1, kv_heads, NSA_GROUP, HEAD_DIM), lambda b: (b, 0, 0, 0)),
                  pl.BlockSpec((1, n_chunks, kcvc.shape[2]), lambda b: (b, 0, 0)),
                  pl.BlockSpec(ind3.shape, lambda b: (0, 0))],
        out_specs=[pl.BlockSpec((1, heads, HEAD_DIM), lambda b: (b, 0, 0)),
                   pl.BlockSpec((1, SUBLANES, nj), lambda b: (b, 0, 0))],
        out_shape=[jax.ShapeDtypeStruct((db, heads, HEAD_DIM), F32),
                   jax.ShapeDtypeStruct((db, SUBLANES, nj), F32)],
        compiler_params=_params("parallel"),
        name="nsa_cmp_decode",
    )(table, q, kcvc, ind3)


def _topk_body(s_ref, o_ref, *, n_cache_blk, n_pick):
    s = s_ref[...]
    j = lax.broadcasted_iota(jnp.int32, s.shape, 1)
    lane = lax.broadcasted_iota(jnp.int32, o_ref.shape, 1)
    s = jnp.where((j == 0) | (j == n_cache_blk - 1), BIG, s)
    s = jnp.where(j < n_cache_blk, s, -jnp.inf)
    picks = jnp.zeros(o_ref.shape, jnp.int32)
    for t in range(n_pick):
        m = jnp.max(s, axis=-1, keepdims=True)
        idx = jnp.min(jnp.where(s == m, j, 2 ** 30), axis=-1, keepdims=True)
        picks = jnp.where(lane == t, idx, picks)
        s = jnp.where(j == idx, -jnp.inf, s)
    o_ref[...] = picks


def _topk(scores, n_cache_blk, n_pick):
    rows = scores.shape[0]
    return pl.pallas_call(
        functools.partial(_topk_body, n_cache_blk=n_cache_blk, n_pick=n_pick),
        out_shape=jax.ShapeDtypeStruct((rows, LANES), jnp.int32),
        name="nsa_topk",
    )(scores)


def _sel_decode_body(pt_ref, pk_ref, *refs, n_pick, past, w_buf, kv_static):
    blocks = refs[:n_pick]
    (knew_ref, vnew_ref, win_ref, wknew_ref, wvnew_ref, q_ref, gates_ref, ocmp_ref, gn_ref,
     tab_ref, o_ref, ksc, vsc, kwc, vwc) = refs[n_pick:]
    b = pl.program_id(0)
    k = pl.program_id(1)
    kv_heads = kv_static
    q = q_ref[0, 0]
    n_slot = n_pick + 1
    sel_len = n_slot * SEL_BLOCK
    rows_tok = 4 * kv_static

    first64 = lax.broadcasted_iota(jnp.int32, (SEL_BLOCK, HEAD_DIM), 0) == 0
    for n in range(n_pick):
        ksc[n * SEL_BLOCK:(n + 1) * SEL_BLOCK, :] = (
            blocks[n][pl.ds(2 * kv_static + k, SEL_BLOCK, stride=rows_tok), :].astype(BF16))
        vsc[n * SEL_BLOCK:(n + 1) * SEL_BLOCK, :] = (
            blocks[n][pl.ds(3 * kv_static + k, SEL_BLOCK, stride=rows_tok), :].astype(BF16))
    ksc[n_pick * SEL_BLOCK:sel_len, :] = jnp.where(first64, knew_ref[0], 0.0).astype(BF16)
    vsc[n_pick * SEL_BLOCK:sel_len, :] = jnp.where(first64, vnew_ref[0], 0.0).astype(BF16)

    lane = lax.broadcasted_iota(jnp.int32, (1, sel_len), 1)
    slot = lane // SEL_BLOCK
    base = jnp.full((1, sel_len), past, jnp.int32)
    for n in range(n_pick):
        base = jnp.where(slot == n, pk_ref[(b * kv_heads + k) * n_pick + n] * SEL_BLOCK, base)
    rel = past - (base + (lane & (SEL_BLOCK - 1)))
    bias = jnp.concatenate([_bias_chain(rel, tab_ref, k * NSA_GROUP + g) for g in range(NSA_GROUP)], axis=0)
    lg = lax.dot_general(q, ksc[...], _NT, preferred_element_type=F32) + bias
    p = _masked_softmax(lg, rel >= 0)
    o_sel = jnp.dot(p.astype(BF16), vsc[...], preferred_element_type=F32)

    win_len = kwc.shape[0]
    firstw = lax.broadcasted_iota(jnp.int32, (win_len - w_buf, HEAD_DIM), 0) == 0
    kwc[0:w_buf, :] = win_ref[pl.ds(k, w_buf, stride=2 * kv_static), :].astype(BF16)
    vwc[0:w_buf, :] = win_ref[pl.ds(kv_static + k, w_buf, stride=2 * kv_static), :].astype(BF16)
    kwc[w_buf:win_len, :] = jnp.where(firstw, wknew_ref[0], 0.0).astype(BF16)
    vwc[w_buf:win_len, :] = jnp.where(firstw, wvnew_ref[0], 0.0).astype(BF16)
    relw = w_buf - lax.broadcasted_iota(jnp.int32, (1, win_len), 1)
    bias = jnp.concatenate([_bias_chain(relw, tab_ref, k * NSA_GROUP + g) for g in range(NSA_GROUP)], axis=0)
    lg = lax.dot_general(q, kwc[...], _NT, preferred_element_type=F32) + bias
    p = _masked_softmax(lg, (relw >= 0) & (relw < WINDOW))
    o_win = jnp.dot(p.astype(BF16), vwc[...], preferred_element_type=F32)

    o = gates_ref[0, 0, 0] * ocmp_ref[0, 0] + gates_ref[0, 0, 1] * o_sel + gates_ref[0, 0, 2] * o_win
    o_ref[0, 0] = _rms(o, gn_ref[0])


def _sel_decode(page_table, picks, cache_rows, nsa_new, win_rows, win_new, q, gates, ocmp, gain, table,
                n_pick, past, w_buf):
    db, kv_heads = q.shape[:2]
    n_pages = page_table.shape[1]
    halves = PAGE_SIZE // SEL_BLOCK
    rows_tok = 4 * kv_heads

    def pick_spec(n):
        def idx(b, k, pt, pk):
            j = pk[(b * kv_heads + k) * n_pick + n]
            return (pt[b * n_pages + j // halves] * halves + j % halves, 0)
        return pl.BlockSpec((SEL_BLOCK * rows_tok, HEAD_DIM), idx)

    def row_spec(col0):
        return pl.BlockSpec((1, 1, HEAD_DIM), lambda b, k, pt, pk: (b, 0, col0 + k))

    def head_spec(shape):
        nd = len(shape)
        return pl.BlockSpec((1, 1) + shape, lambda b, k, pt, pk: (b, k) + (0,) * nd)

    win_len = w_buf + LANES
    grid_spec = pltpu.PrefetchScalarGridSpec(
        num_scalar_prefetch=2,
        grid=(db, kv_heads),
        in_specs=[pick_spec(n) for n in range(n_pick)]
        + [row_spec(2 * kv_heads), row_spec(3 * kv_heads),
           pl.BlockSpec((w_buf * 2 * kv_heads, HEAD_DIM), lambda b, k, pt, pk: (b, 0)),
           row_spec(0), row_spec(kv_heads),
           head_spec((NSA_GROUP, HEAD_DIM)), head_spec((3, NSA_GROUP, 1)), head_spec((NSA_GROUP, HEAD_DIM)),
           pl.BlockSpec((1, NSA_GROUP, HEAD_DIM), lambda b, k, pt, pk: (k, 0, 0)),
           pl.BlockSpec(memory_space=pltpu.SMEM)],
        out_specs=head_spec((NSA_GROUP, HEAD_DIM)),
        scratch_shapes=[pltpu.VMEM(((n_pick + 1) * SEL_BLOCK, HEAD_DIM), BF16),
                        pltpu.VMEM(((n_pick + 1) * SEL_BLOCK, HEAD_DIM), BF16),
                        pltpu.VMEM((win_len, HEAD_DIM), BF16),
                        pltpu.VMEM((win_len, HEAD_DIM), BF16)])
    body = functools.partial(_sel_decode_body, n_pick=n_pick, past=past, w_buf=w_buf, kv_static=kv_heads)
    return pl.pallas_call(
        body, grid_spec=grid_spec,
        out_shape=jax.ShapeDtypeStruct((db, kv_heads, NSA_GROUP, HEAD_DIM), F32),
        compiler_params=_params("parallel", "parallel"),
        name="nsa_sel_decode",
    )(page_table.reshape(-1), picks, *([cache_rows] * n_pick), nsa_new, nsa_new, win_rows,
      win_new, win_new, q, gates, ocmp, gain.reshape(kv_heads, NSA_GROUP, HEAD_DIM), table)


def _out_proj_body(x_ref, a_ref, b_ref, w_ref, o_ref):
    half = a_ref.shape[1]
    o_ref[...] = (x_ref[...] + jnp.dot(a_ref[...], w_ref[0:half, :], preferred_element_type=F32)
                  + jnp.dot(b_ref[...], w_ref[half:2 * half, :], preferred_element_type=F32))


def _out_proj(x, a, b, w, tm):
    m, d = x.shape
    half = a.shape[1]
    return pl.pallas_call(
        _out_proj_body, grid=(m // tm,),
        in_specs=[pl.BlockSpec((tm, d), lambda i: (i, 0)),
                  pl.BlockSpec((tm, half), lambda i: (i, 0)),
                  pl.BlockSpec((tm, half), lambda i: (i, 0)),
                  _resident(w.shape)],
        out_specs=pl.BlockSpec((tm, d), lambda i: (i, 0)),
        out_shape=jax.ShapeDtypeStruct((m, d), F32),
        compiler_params=_params("parallel"),
        name="out_proj",
    )(x, a, b, w)


def _norm_matmul_body(x_ref, g_ref, w_ref, o_ref):
    o_ref[...] = jnp.dot(_rms(x_ref[...], g_ref[...]).astype(BF16), w_ref[...], preferred_element_type=F32)


def _norm_matmul(x, g, w, tm):
    m, d = x.shape
    n = w.shape[1]
    return pl.pallas_call(
        _norm_matmul_body, grid=(m // tm,),
        in_specs=[pl.BlockSpec((tm, d), lambda i: (i, 0)),
                  pl.BlockSpec((1, d), lambda i: (0, 0)),
                  _resident(w.shape)],
        out_specs=pl.BlockSpec((tm, n), lambda i: (i, 0)),
        out_shape=jax.ShapeDtypeStruct((m, n), F32),
        compiler_params=_params("parallel"),
        name="mem_kv_proj",
    )(x, g.reshape(1, d), w)


def _cross_body(x_ref, g_ref, wq_ref, wo_ref, mem_ref, o_ref, *, scale, n_mem, row_layout):
    x = x_ref[0]
    hb = _rms(x, g_ref[...]).astype(BF16)
    qh = (jnp.dot(hb, wq_ref[...], preferred_element_type=F32) * scale).astype(BF16)
    width = MEM_HEADS * HEAD_DIM
    outs = []
    for h in range(MEM_HEADS):
        if row_layout:
            k = mem_ref[pl.ds(h, n_mem, stride=2 * MEM_HEADS), :].astype(BF16)
            v = mem_ref[pl.ds(MEM_HEADS + h, n_mem, stride=2 * MEM_HEADS), :].astype(BF16)
        else:
            k = mem_ref[0, :, h * HEAD_DIM:(h + 1) * HEAD_DIM].astype(BF16)
            v = mem_ref[0, :, width + h * HEAD_DIM:width + (h + 1) * HEAD_DIM].astype(BF16)
        lg = lax.dot_general(qh[:, h * HEAD_DIM:(h + 1) * HEAD_DIM], k, _NT, preferred_element_type=F32)
        e = jnp.exp(lg - jnp.max(lg, axis=-1, keepdims=True))
        p = e / jnp.sum(e, axis=-1, keepdims=True)
        outs.append(jnp.dot(p.astype(BF16), v, preferred_element_type=F32).astype(BF16))
    o = jnp.concatenate(outs, axis=1)
    o_ref[0] = x + jnp.dot(o, wo_ref[...], preferred_element_type=F32)


def _cross(x3, g, wq, wo, mem, n_mem, tm):
    nb, t, d = x3.shape
    row_layout = mem.ndim == 2
    if row_layout:
        mem_spec = pl.BlockSpec((n_mem * 2 * MEM_HEADS, HEAD_DIM), lambda b, i: (b, 0))
    else:
        mem_spec = pl.BlockSpec((1, n_mem, mem.shape[2]), lambda b, i: (b, 0, 0))
    body = functools.partial(_cross_body, scale=1.0 / math.sqrt(HEAD_DIM), n_mem=n_mem, row_layout=row_layout)
    return pl.pallas_call(
        body, grid=(nb, t // tm),
        in_specs=[pl.BlockSpec((1, tm, d), lambda b, i: (b, i, 0)),
                  pl.BlockSpec((1, d), lambda b, i: (0, 0)),
                  _resident(wq.shape), _resident(wo.shape),
                  mem_spec],
        out_specs=pl.BlockSpec((1, tm, d), lambda b, i: (b, i, 0)),
        out_shape=jax.ShapeDtypeStruct((nb, t, d), F32),
        compiler_params=_params("parallel", "parallel"),
        name="cross_attn",
    )(x3, g.reshape(1, d), wq, wo, mem)


def _mlp_body(x_ref, g_ref, wu_ref, wd_ref, gf_ref, o_ref, h_ref, acc_ref):
    f = pl.program_id(1)

    @pl.when(f == 0)
    def _():
        h_ref[...] = _rms(x_ref[...], g_ref[...]).astype(BF16)
        acc_ref[...] = jnp.zeros_like(acc_ref)

    u = jnp.maximum(jnp.dot(h_ref[...], wu_ref[...], preferred_element_type=F32), 0.0)
    acc_ref[...] += jnp.dot((u * u).astype(BF16), wd_ref[...], preferred_element_type=F32)

    @pl.when(f == pl.num_programs(1) - 1)
    def _():
        o_ref[...] = _rms(x_ref[...] + acc_ref[...], gf_ref[...])


def _mlp_final(x, g, wu, wd, gf, tm, tf):
    m, d = x.shape
    dff = wu.shape[1]
    return pl.pallas_call(
        _mlp_body, grid=(m // tm, dff // tf),
        in_specs=[pl.BlockSpec((tm, d), lambda i, f: (i, 0)),
                  pl.BlockSpec((1, d), lambda i, f: (0, 0)),
                  pl.BlockSpec((d, tf), lambda i, f: (0, f)),
                  pl.BlockSpec((tf, d), lambda i, f: (f, 0)),
                  pl.BlockSpec((1, d), lambda i, f: (0, 0))],
        out_specs=pl.BlockSpec((tm, d), lambda i, f: (i, 0)),
        out_shape=jax.ShapeDtypeStruct((m, d), F32),
        scratch_shapes=[pltpu.VMEM((tm, d), BF16), pltpu.VMEM((tm, d), F32)],
        compiler_params=_params("parallel", "arbitrary"),
        name="mlp_final",
    )(x, g.reshape(1, d), wu, wd, gf.reshape(1, d))


def _w1cat(w1):
    half = CMP_STRIDE * HEAD_DIM
    return jnp.concatenate([w1[:half], w1[half:]], axis=-1).astype(BF16)


def kernel(x_prompt, x_sample, cache_sb_kv, cache_nsa_kv, cache_win_kv, cache_mem_kv, page_table, mem_prompt,
           norm_mix, w_in, norm_sb_out, norm_nsa_out, w_out, rel_bias_table,
           cmp_pe_k, cmp_pe_v, w_cmp_k1, w_cmp_k2, w_cmp_v1, w_cmp_v2,
           norm_cross, norm_mem, w_cross_q, w_mem_kv, w_cross_o,
           norm_ffn, w_up, w_down, norm_final):
    batch, seq, d = x_prompt.shape
    db, dec_seq, _ = x_sample.shape
    assert dec_seq == 1
    depth = w_in.shape[0]
    assert depth == 1, "the final norm is fused into the MLP of the only layer"
    n_pool = cache_sb_kv.shape[1]
    n_pages = page_table.shape[1]
    past = n_pages * PAGE_SIZE
    sb_heads = cache_sb_kv.shape[4]
    kv_heads = cache_nsa_kv.shape[4]
    nsa_heads = kv_heads * NSA_GROUP
    sbw = sb_heads * HEAD_DIM
    nsw = nsa_heads * HEAD_DIM
    kvw = kv_heads * HEAD_DIM
    n_gates = 3 * nsa_heads
    w_buf = cache_win_kv.shape[2]
    win_keep = min(WINDOW, seq)
    n_mem = mem_prompt.shape[1]
    mem_w = MEM_HEADS * HEAD_DIM
    m = batch * seq
    l = 0

    w_in_p = jnp.pad(w_in[l].astype(BF16), ((0, 0), (0, LANES - n_gates)))
    w1cat = jnp.stack([_w1cat(w_cmp_k1[l]), _w1cat(w_cmp_v1[l])])
    w2 = jnp.stack([w_cmp_k2[l], w_cmp_v2[l]]).astype(BF16)
    pe = jnp.stack([cmp_pe_k[l].reshape(1, -1), cmp_pe_v[l].reshape(1, -1)])
    pe_term = _pe_term(jnp.broadcast_to(pe, (2, SUBLANES, pe.shape[-1])), jnp.stack([w_cmp_k1[l], w_cmp_v1[l]]))
    w_out_b = w_out[l].astype(BF16)

    xp = x_prompt.reshape(m, d)
    (qsb, sbkv, sbkv16, qns, nsakv, nsakv16, winkv, winkv16, gates) = _project(xp, norm_mix[l], w_in_p, 256)
    o_sb = _sb_prompt(qsb, sbkv16, norm_sb_out[l], batch, seq)

    pages_seq = seq // PAGE_SIZE
    prompt_pages = jnp.arange(batch * pages_seq, dtype=jnp.int32).reshape(batch, pages_seq)
    kcvc = _compress(nsakv.reshape(batch * pages_seq, PAGE_SIZE, 4 * kvw), prompt_pages, w1cat, w2, pe_term,
                     kv_heads, pages_seq)
    nq = seq // LANES
    tsel = _bias_tiles(rel_bias_table, nq, LANES, 1, 0)
    tcmp = _bias_tiles(rel_bias_table, nq, LANES, CMP_STRIDE, CMP_LEN - CMP_STRIDE - 1)
    twin = _bias_tiles(rel_bias_table, WINDOW // LANES + 1, LANES, 1, 0, window=True)
    o_ns = _nsa_prompt(qns, kcvc, nsakv16, winkv16, tsel, twin, tcmp, gates, norm_nsa_out[l], batch, seq)
    xp = _out_proj(xp, o_sb, o_ns, w_out_b, 512)

    sb_kv_prompt = sbkv.reshape(1, batch, seq, 2, sb_heads, HEAD_DIM)
    nsa_kv_prompt = nsakv.reshape(1, batch, seq, 4, kv_heads, HEAD_DIM)
    win_kv_prompt = winkv.reshape(1, batch, seq, 2, kv_heads, HEAD_DIM)[:, :, seq - win_keep:]

    xs = x_sample.reshape(db, d)
    (qsb_s, sbkv_s, _, qns_s, nsakv_s, _, winkv_s, _, gates_s) = _project(xs, norm_mix[l], w_in_p, db)
    o_sb_s = _sb_decode(qsb_s.reshape(db, sb_heads, HEAD_DIM), cache_sb_kv[l].reshape(-1, HEAD_DIM), page_table,
                        norm_sb_out[l], 16)

    nsa_rows = cache_nsa_kv[l].reshape(-1, HEAD_DIM)
    kcvc_s = _compress(nsa_rows, page_table, w1cat, w2, pe_term, kv_heads, 16)
    n_chunks_s = past // CMP_STRIDE
    tk = past + 1
    n_cmp_s = (tk - CMP_LEN) // CMP_STRIDE + 1
    n_blk_s = -(-tk // SEL_BLOCK)
    n_cache_blk = past // SEL_BLOCK
    assert n_blk_s == n_cache_blk + 1 and n_cache_blk >= N_SEL
    nj = -(-n_blk_s // LANES) * LANES
    ind_s, _ = _sel_constants(n_chunks_s, n_blk_s, n_cmp_s, 0)
    ind_s_pad = np.zeros((nj, n_chunks_s), np.float32)
    ind_s_pad[:n_blk_s] = ind_s
    ind3_s = jnp.asarray(np.concatenate([ind_s_pad.T] * 3, axis=0), BF16)
    q_dec = qns_s.reshape(db, kv_heads, NSA_GROUP, HEAD_DIM)
    ocmp_s, scores = _cmp_decode(rel_bias_table, q_dec, kcvc_s, ind3_s, past, kv_heads)
    n_pick_cache = N_SEL - 1
    picks = _topk(scores[:, :kv_heads].reshape(db * kv_heads, nj), n_cache_blk, n_pick_cache)
    picks = picks[:, :n_pick_cache].reshape(-1)
    gates_d = gates_s[:, :n_gates].reshape(db, kv_heads, NSA_GROUP, 3).transpose(0, 1, 3, 2)[..., None]
    o_ns_s = _sel_decode(page_table, picks, nsa_rows, nsakv_s.reshape(db, 1, 4 * kvw),
                         cache_win_kv[l].reshape(-1, HEAD_DIM), winkv_s.reshape(db, 1, 2 * kvw), q_dec, gates_d,
                         ocmp_s.reshape(db, kv_heads, NSA_GROUP, HEAD_DIM), norm_nsa_out[l],
                         rel_bias_table, n_pick_cache, past, w_buf)
    xs = _out_proj(xs, o_sb_s.reshape(db, sbw).astype(BF16), o_ns_s.reshape(db, nsw).astype(BF16), w_out_b, db)

    sb_kv_sample = sbkv_s.reshape(1, db, 1, 2, sb_heads, HEAD_DIM)
    nsa_kv_sample = nsakv_s.reshape(1, db, 1, 4, kv_heads, HEAD_DIM)
    win_new = winkv_s.reshape(db, 1, 2, kv_heads, HEAD_DIM)
    win_kv_sample = jnp.concatenate([cache_win_kv[l], win_new], axis=1)[None, :, 1:]

    mem_kv = _norm_matmul(mem_prompt.reshape(batch * n_mem, d), norm_mem[l], w_mem_kv[l].astype(BF16), n_mem)
    wq = w_cross_q[l].astype(BF16)
    wo = w_cross_o[l].astype(BF16)
    xp = _cross(xp.reshape(batch, seq, d), norm_cross[l], wq, wo, mem_kv.reshape(batch, n_mem, 2 * mem_w),
                n_mem, 512)
    xs = _cross(xs.reshape(db, 1, d), norm_cross[l], wq, wo, cache_mem_kv[l].reshape(-1, HEAD_DIM), n_mem, 1)
    mem_kv_prompt = mem_kv.reshape(1, batch, n_mem, 2, MEM_HEADS, HEAD_DIM)

    wu = w_up[l].astype(BF16)
    wd = w_down[l].astype(BF16)
    y_prompt = _mlp_final(xp.reshape(m, d), norm_ffn[l], wu, wd, norm_final, 512, 1024).reshape(batch, seq, d)
    y_sample = _mlp_final(xs.reshape(db, d), norm_ffn[l], wu, wd, norm_final, db, 1024).reshape(db, 1, d)

    return (y_prompt, y_sample, sb_kv_prompt, sb_kv_sample, nsa_kv_prompt, nsa_kv_sample,
            win_kv_prompt, win_kv_sample, mem_kv_prompt)
```
